```python
import math
import jax
import jax.numpy as jnp
from jax import lax
import numpy as np

D_MODEL = 1024
BATCH = 8
SEQ = 4096
DEPTH = 2

HEAD_DIM = 64
BRANCH_WIDTH = 256
N_MIXERS = 4
NSA_Q_HEADS = 4
NSA_KV_HEADS = 2
NSA_GROUP = NSA_Q_HEADS // NSA_KV_HEADS
NSA_CMP_LEN = 32
NSA_CMP_STRIDE = 16
NSA_SEL_LEN = 64
NSA_SEL_TOPN = 16
NSA_WINDOW = 512
NSA_SEL_QBLOCK = 32
NSA_WIN_QBLOCK = 128
NSA_FORCE_SCORE = 1.0e4
SC_WIDTH = BRANCH_WIDTH
SC_CONV_LEN = 3
SB_HEADS = 4
SB_WIDTH = SB_HEADS * HEAD_DIM
SB_QBLOCK = 128
S5_WIDTH = BRANCH_WIDTH
S5_GROUP_CH = 16
S5_GROUPS = S5_WIDTH // S5_GROUP_CH
S5_STATE = 64

ROPE_THETA = 10000.0
NORM_EPS = 1e-6
POS_OFFSET_MAX = 1024

PROJ_SIZES = (
    NSA_Q_HEADS * HEAD_DIM,
    3 * 2 * NSA_KV_HEADS * HEAD_DIM,
    3 * NSA_Q_HEADS,
    BRANCH_WIDTH,
    3 * SC_WIDTH,
    BRANCH_WIDTH,
    3 * SB_WIDTH,
    BRANCH_WIDTH,
    S5_WIDTH,
    BRANCH_WIDTH,
    N_MIXERS * D_MODEL,
)
IN_PROJ_WIDTH = sum(PROJ_SIZES)

kernel_name = "hybrid_nsa_shortconv_stickbreak_s5"


def rms_norm(x, g):
    xf = x.astype(jnp.float32)
    y = xf * lax.rsqrt(jnp.mean(xf * xf, axis=-1, keepdims=True) + NORM_EPS)
    return (y * g.astype(jnp.float32)).astype(x.dtype)


def rope(x, pos):
    half = HEAD_DIM // 2
    inv_freq = jnp.power(ROPE_THETA, -jnp.arange(half, dtype=jnp.float32) / half)
    ang = pos.astype(jnp.float32)[..., None] * inv_freq
    ang = ang.reshape(ang.shape[:2] + (1,) * (x.ndim - 3) + (half,))
    cos, sin = jnp.cos(ang), jnp.sin(ang)
    xf = x.astype(jnp.float32)
    x1, x2 = xf[..., :half], xf[..., half:]
    return jnp.concatenate([x1 * cos - x2 * sin, x2 * cos + x1 * sin], axis=-1).astype(x.dtype)


def masked_softmax(scores, mask):
    s = jnp.where(mask, scores.astype(jnp.float32), -jnp.inf)
    m = jnp.max(s, axis=-1, keepdims=True)
    m = jnp.where(jnp.isfinite(m), m, 0.0)
    e = jnp.where(mask, jnp.exp(s - m), 0.0)
    return e / jnp.maximum(jnp.sum(e, axis=-1, keepdims=True), 1e-30)


def nsa_mixer(q_in, kv_in, gate_in, positions, qk_g, cmp_pe, cmp_w1, cmp_w2):
    b, s, _ = q_in.shape
    kh, grp, dh = NSA_KV_HEADS, NSA_GROUP, HEAD_DIM
    scale = dh ** -0.5
    t_idx = jnp.arange(s)
    q = rope(rms_norm(q_in.reshape(b, s, kh, grp, dh), qk_g[0]), positions)
    kv = kv_in.reshape(b, s, 3, 2, kh, dh)
    k_c, v_c = kv[:, :, 0, 0], kv[:, :, 0, 1]
    k_s = rope(rms_norm(kv[:, :, 1, 0], qk_g[2]), positions)
    v_s = kv[:, :, 1, 1]
    k_w = rope(rms_norm(kv[:, :, 2, 0], qk_g[3]), positions)
    v_w = kv[:, :, 2, 1]

    n_cmp = (s - NSA_CMP_LEN) // NSA_CMP_STRIDE + 1
    cmp_start = jnp.arange(n_cmp) * NSA_CMP_STRIDE
    cmp_end = cmp_start + NSA_CMP_LEN - 1
    blk_idx = cmp_start[:, None] + jnp.arange(NSA_CMP_LEN)[None, :]

    def compress(t, j):
        blk = t[:, blk_idx] + cmp_pe[j][None, None, :, None, :]
        blk = jnp.moveaxis(blk, 3, 2).reshape(b, n_cmp, kh, NSA_CMP_LEN * dh)
        return jax.nn.silu(blk @ cmp_w1[j]) @ cmp_w2[j]

    k_cmp = rope(rms_norm(compress(k_c, 0), qk_g[1]), positions[:, cmp_end])
    v_cmp = compress(v_c, 1)
    sc_cmp = jnp.einsum('bskgd,bnkd->bkgsn', q, k_cmp).astype(jnp.float32) * scale
    p_cmp = masked_softmax(sc_cmp, cmp_end[None, :] <= t_idx[:, None])
    o_cmp = jnp.einsum('bkgsn,bnkd->bskgd', p_cmp.astype(v_cmp.dtype), v_cmp)

    n_sel = s // NSA_SEL_LEN
    sel_start = jnp.arange(n_sel) * NSA_SEL_LEN
    overlap = ((cmp_start[:, None] < sel_start[None, :] + NSA_SEL_LEN)
               & (cmp_start[:, None] + NSA_CMP_LEN > sel_start[None, :])).astype(jnp.float32)
    imp = jnp.einsum('bkgsn,nj->bskj', p_cmp, overlap)
    cur = t_idx // NSA_SEL_LEN
    jb = jnp.arange(n_sel)
    forced = (jb[None, :] == 0) | (jb[None, :] == cur[:, None]) | (jb[None, :] == cur[:, None] - 1)
    valid = sel_start[None, :] <= t_idx[:, None]
    imp = jnp.where(forced[None, :, None, :], NSA_FORCE_SCORE,
                    jnp.where(valid[None, :, None, :], imp, -NSA_FORCE_SCORE))
    top_n = min(NSA_SEL_TOPN, n_sel)
    _, sel_idx = lax.top_k(imp, top_n)

    k_blk = k_s.reshape(b, n_sel, NSA_SEL_LEN, kh, dh).transpose(0, 3, 1, 2, 4)
    v_blk = v_s.reshape(b, n_sel, NSA_SEL_LEN, kh, dh).transpose(0, 3, 1, 2, 4)
    qbs = NSA_SEL_QBLOCK
    nq = s // qbs
    q_b = jnp.moveaxis(q.reshape(b, nq, qbs, kh, grp, dh), 1, 0)
    i_b = jnp.moveaxis(sel_idx.reshape(b, nq, qbs, kh, top_n), 1, 0)
    t_b = t_idx.reshape(nq, qbs)
    bi = jnp.arange(b)[:, None, None, None]
    hi = jnp.arange(kh)[None, None, :, None]
    n_keys = top_n * NSA_SEL_LEN

    def sel_block(args):
        qb, ib, tb = args
        kg = k_blk[bi, hi, ib].reshape(b, qbs, kh, n_keys, dh)
        vg = v_blk[bi, hi, ib].reshape(b, qbs, kh, n_keys, dh)
        pos = (ib[..., None] * NSA_SEL_LEN + jnp.arange(NSA_SEL_LEN)).reshape(b, qbs, kh, n_keys)
        mask = (pos <= tb[None, :, None, None]).transpose(0, 2, 1, 3)[:, :, None]
        sc = jnp.einsum('bqkgd,bqkmd->bkgqm', qb, kg).astype(jnp.float32) * scale
        p = masked_softmax(sc, mask)
        return jnp.einsum('bkgqm,bqkmd->bqkgd', p.astype(vg.dtype), vg)

    o_slc = jnp.moveaxis(lax.map(sel_block, (q_b, i_b, t_b)), 0, 1).reshape(b, s, kh, grp, dh)

    qbw = NSA_WIN_QBLOCK
    nw = s // qbw
    kp = jnp.pad(k_w, ((0, 0), (NSA_WINDOW, 0), (0, 0), (0, 0)))
    vp = jnp.pad(v_w, ((0, 0), (NSA_WINDOW, 0), (0, 0), (0, 0)))
    q_w = jnp.moveaxis(q.reshape(b, nw, qbw, kh, grp, dh), 1, 0)
    w_starts = jnp.arange(nw) * qbw

    def win_block(args):
        qb, start = args
        kb = lax.dynamic_slice_in_dim(kp, start, NSA_WINDOW + qbw, axis=1)
        vb = lax.dynamic_slice_in_dim(vp, start, NSA_WINDOW + qbw, axis=1)
        tq = start + jnp.arange(qbw)
        tk = start - NSA_WINDOW + jnp.arange(NSA_WINDOW + qbw)
        mask = (tk[None, :] <= tq[:, None]) & (tk[None, :] > tq[:, None] - NSA_WINDOW) & (tk[None, :] >= 0)
        sc = jnp.einsum('bqkgd,bmkd->bkgqm', qb, kb).astype(jnp.float32) * scale
        p = masked_softmax(sc, mask)
        return jnp.einsum('bkgqm,bmkd->bqkgd', p.astype(vb.dtype), vb)

    o_win = jnp.moveaxis(lax.map(win_block, (q_w, w_starts)), 0, 1).reshape(b, s, kh, grp, dh)

    g = jax.nn.sigmoid(gate_in).reshape(b, s, 3, kh, grp)[..., None]
    o = g[:, :, 0] * o_cmp + g[:, :, 1] * o_slc + g[:, :, 2] * o_win
    return o.reshape(b, s, NSA_Q_HEADS * dh)


def short_conv_mixer(bcx, conv_w):
    bg, cg, xin = jnp.split(bcx, 3, axis=-1)
    u = cg * xin
    y = lax.conv_general_dilated(u, conv_w[:, None, :], window_strides=(1,),
                                 padding=[(SC_CONV_LEN - 1, 0)],
                                 dimension_numbers=('NWC', 'WIO', 'NWC'),
                                 feature_group_count=SC_WIDTH)
    return bg * y


def stick_breaking_mixer(qkv):
    b, s, _ = qkv.shape
    t = qkv.reshape(b, s, 3, SB_HEADS, HEAD_DIM)
    q, k, v = t[:, :, 0], t[:, :, 1], t[:, :, 2]
    scale = HEAD_DIM ** -0.5
    nq = s // SB_QBLOCK
    q_b = jnp.moveaxis(q.reshape(b, nq, SB_QBLOCK, SB_HEADS, HEAD_DIM), 1, 0)
    q_starts = jnp.arange(nq) * SB_QBLOCK
    t_k = jnp.arange(s)

    def block(args):
        qb, start = args
        z = jnp.einsum('bqhd,bshd->bhqs', qb, k).astype(jnp.float32) * scale
        tq = start + jnp.arange(SB_QBLOCK)
        mask = t_k[None, :] < tq[:, None]
        log_1mb = jnp.where(mask, jax.nn.log_sigmoid(-z), 0.0)
        after = lax.cumsum(log_1mb, axis=3, reverse=True) - log_1mb
        w = jnp.where(mask, jnp.exp(jax.nn.log_sigmoid(z) + after), 0.0)
        return jnp.einsum('bhqs,bshd->bqhd', w.astype(v.dtype), v)

    o = jnp.moveaxis(lax.map(block, (q_b, q_starts)), 0, 1)
    return o.reshape(b, s, SB_WIDTH)


def _ssm_combine(e1, e2):
    a1r, a1i, b1r, b1i = e1
    a2r, a2i, b2r, b2i = e2
    return (a2r * a1r - a2i * a1i,
            a2r * a1i + a2i * a1r,
            a2r * b1r - a2i * b1i + b2r,
            a2r * b1i + a2i * b1r + b2i)


def s5_mixer(u, a_re, a_im, log_dt, b_re, b_im, c_re, c_im, d_skip, glu_w, glu_b):
    f32 = jnp.float32
    bsz, s, _ = u.shape
    ug = u.reshape(bsz, s, S5_GROUPS, S5_GROUP_CH).astype(f32)
    dt = jnp.exp(log_dt.astype(f32))[:, None]
    lr, li = a_re.astype(f32), a_im.astype(f32)
    mag = jnp.exp(lr * dt)
    ab_re, ab_im = mag * jnp.cos(li * dt), mag * jnp.sin(li * dt)
    den = lr * lr + li * li
    coef_re = ((ab_re - 1.0) * lr + ab_im * li) / den
    coef_im = (ab_im * lr - (ab_re - 1.0) * li) / den
    br, bim = b_re.astype(f32), b_im.astype(f32)
    bb_re = coef_re[..., None] * br - coef_im[..., None] * bim
    bb_im = coef_re[..., None] * bim + coef_im[..., None] * br
    bu_re = jnp.einsum('gpc,bsgc->bsgp', bb_re, ug)
    bu_im = jnp.einsum('gpc,bsgc->bsgp', bb_im, ug)
    a_r = jnp.broadcast_to(ab_re, bu_re.shape)
    a_i = jnp.broadcast_to(ab_im, bu_re.shape)
    _, _, x_re, x_im = lax.associative_scan(_ssm_combine, (a_r, a_i, bu_re, bu_im), axis=1)
    y = (jnp.einsum('gcp,bsgp->bsgc', c_re.astype(f32), x_re)
         - jnp.einsum('gcp,bsgp->bsgc', c_im.astype(f32), x_im)
         + d_skip.astype(f32) * ug)
    y = y.reshape(bsz, s, S5_WIDTH).astype(u.dtype)
    a, g = jnp.split(y @ glu_w + glu_b, 2, axis=-1)
    return a * jax.nn.sigmoid(g)


def setup_inputs(seed: int = 0) -> dict:
    key = jax.random.key(seed)
    ks = iter(jax.random.split(key, 32))
    f32 = jnp.float32
    L, dh = DEPTH, HEAD_DIM

    def nrm(shape, scale):
        return jax.random.normal(next(ks), shape, f32) * scale

    x = nrm((BATCH, SEQ, D_MODEL), 1.0)
    positions = (jnp.arange(SEQ, dtype=jnp.int32)[None, :]
                 + jax.random.randint(next(ks), (BATCH, 1), 0, POS_OFFSET_MAX, dtype=jnp.int32))
    norm_g = 1.0 + nrm((L, D_MODEL), 0.02)
    w_in = nrm((L, D_MODEL, IN_PROJ_WIDTH), D_MODEL ** -0.5)
    nsa_qk_g = 1.0 + nrm((L, 4, dh), 0.02)
    nsa_cmp_pe = nrm((L, 2, NSA_CMP_LEN, dh), 0.1)
    nsa_cmp_w1 = nrm((L, 2, NSA_CMP_LEN * dh, dh), (NSA_CMP_LEN * dh) ** -0.5)
    nsa_cmp_w2 = nrm((L, 2, dh, dh), dh ** -0.5)
    sc_conv_w = nrm((L, SC_CONV_LEN, SC_WIDTH), SC_CONV_LEN ** -0.5)
    n_idx = jnp.arange(S5_STATE, dtype=f32)
    s5_a_re = -0.5 + nrm((L, S5_GROUPS, S5_STATE), 0.01)
    s5_a_im = math.pi * n_idx + nrm((L, S5_GROUPS, S5_STATE), 0.01)
    s5_log_dt = jax.random.uniform(next(ks), (L, S5_GROUPS), f32, math.log(1e-3), math.log(1e-1))
    s5_b_re = nrm((L, S5_GROUPS, S5_STATE, S5_GROUP_CH), (2 * S5_GROUP_CH) ** -0.5)
    s5_b_im = nrm((L, S5_GROUPS, S5_STATE, S5_GROUP_CH), (2 * S5_GROUP_CH) ** -0.5)
    s5_c_re = nrm((L, S5_GROUPS, S5_GROUP_CH, S5_STATE), 0.5)
    s5_c_im = nrm((L, S5_GROUPS, S5_GROUP_CH, S5_STATE), 0.5)
    s5_d = nrm((L, S5_GROUPS, S5_GROUP_CH), 0.5)
    s5_glu_w = nrm((L, S5_WIDTH, 2 * S5_WIDTH), S5_WIDTH ** -0.5)
    s5_glu_b = nrm((L, 2 * S5_WIDTH), 0.01)
    w_branch = nrm((L, N_MIXERS, BRANCH_WIDTH, D_MODEL), BRANCH_WIDTH ** -0.5)
    w_out = nrm((L, D_MODEL, D_MODEL), D_MODEL ** -0.5)
    return {"x": x, "positions": positions, "norm_g": norm_g, "w_in": w_in,
            "nsa_qk_g": nsa_qk_g, "nsa_cmp_pe": nsa_cmp_pe, "nsa_cmp_w1": nsa_cmp_w1,
            "nsa_cmp_w2": nsa_cmp_w2, "sc_conv_w": sc_conv_w, "s5_a_re": s5_a_re,
            "s5_a_im": s5_a_im, "s5_log_dt": s5_log_dt, "s5_b_re": s5_b_re, "s5_b_im": s5_b_im,
            "s5_c_re": s5_c_re, "s5_c_im": s5_c_im, "s5_d": s5_d, "s5_glu_w": s5_glu_w,
            "s5_glu_b": s5_glu_b, "w_branch": w_branch, "w_out": w_out}


def reference(x, positions, norm_g, w_in, nsa_qk_g, nsa_cmp_pe, nsa_cmp_w1, nsa_cmp_w2,
              sc_conv_w, s5_a_re, s5_a_im, s5_log_dt, s5_b_re, s5_b_im, s5_c_re, s5_c_im,
              s5_d, s5_glu_w, s5_glu_b, w_branch, w_out):
    b, s, _ = x.shape
    split_at = [int(o) for o in np.cumsum(PROJ_SIZES)[:-1]]
    for l in range(DEPTH):
        h = rms_norm(x, norm_g[l])
        proj = h @ w_in[l]
        (nsa_q, nsa_kv, nsa_gate, nsa_z, sc_bcx, sc_z, sb_qkv, sb_z,
         s5_u, s5_z, merge) = jnp.split(proj, split_at, axis=-1)
        outs = (
            nsa_mixer(nsa_q, nsa_kv, nsa_gate, positions, nsa_qk_g[l], nsa_cmp_pe[l],
                      nsa_cmp_w1[l], nsa_cmp_w2[l]) * jax.nn.silu(nsa_z),
            short_conv_mixer(sc_bcx, sc_conv_w[l]) * jax.nn.silu(sc_z),
            stick_breaking_mixer(sb_qkv) * jax.nn.silu(sb_z),
            s5_mixer(s5_u, s5_a_re[l], s5_a_im[l], s5_log_dt[l], s5_b_re[l], s5_b_im[l],
                     s5_c_re[l], s5_c_im[l], s5_d[l], s5_glu_w[l], s5_glu_b[l]) * jax.nn.silu(s5_z),
        )
        gates = jax.nn.sigmoid(merge).reshape(b, s, N_MIXERS, D_MODEL)
        mixed = gates[:, :, 0] * (outs[0] @ w_branch[l, 0])
        for m in range(1, N_MIXERS):
            mixed = mixed + gates[:, :, m] * (outs[m] @ w_branch[l, m])
        x = x + mixed @ w_out[l]
    return x
```

```python
import functools
import math

import jax
import jax.numpy as jnp
from jax import lax
from jax.experimental import pallas as pl
from jax.experimental.pallas import tpu as pltpu

F32 = jnp.float32
BF16 = jnp.bfloat16

HEAD_DIM = 64
N_HEADS = 4
WIDTH = N_HEADS * HEAD_DIM
NSA_KV_HEADS = 2
NSA_CMP_LEN = 32
NSA_CMP_STRIDE = 16
NSA_SEL_LEN = 64
NSA_SEL_TOPN = 16
NSA_WINDOW = 512
NSA_FORCE_SCORE = 1.0e4
S5_GROUPS = 16
S5_GROUP_CH = 16
S5_STATE = 64
ROPE_THETA = 10000.0
NORM_EPS = 1e-6
QK_SCALE = HEAD_DIM ** -0.5

LANES = 128
SUBLANES = 8
NEG = -1.0e30
SEL_BIAS = -30000.0
SB_DEAD = -120.0
VMEM_LIMIT = 56 * 1024 * 1024

TM_PROJ = 256
TT_CMP = 128
TQ_ATT = 128
TK_ATT = 256
TS_S5 = 128


def _dot(a, b):
    return jnp.dot(a, b, preferred_element_type=F32)


def _dot_nt(a, b):
    return lax.dot_general(a, b, (((1,), (1,)), ((), ())), preferred_element_type=F32)


def _silu(x):
    return x * (1.0 / (1.0 + jnp.exp(-x)))


def _sigmoid(x):
    return 1.0 / (1.0 + jnp.exp(-x))


def _iota(shape, dim):
    return lax.broadcasted_iota(jnp.int32, shape, dim)


def _group_mean_sq(x):
    outs = []
    lane = _iota((x.shape[0], LANES), 1)
    low = lane < HEAD_DIM
    for c in range(x.shape[1] // LANES):
        xc = x[:, c * LANES:(c + 1) * LANES]
        sq = xc * xc
        s_lo = jnp.sum(jnp.where(low, sq, 0.0), axis=-1, keepdims=True)
        s_hi = jnp.sum(jnp.where(low, 0.0, sq), axis=-1, keepdims=True)
        outs.append(jnp.where(low, s_lo, s_hi) * (1.0 / HEAD_DIM))
    return outs[0] if len(outs) == 1 else jnp.concatenate(outs, axis=1)


def _rot_half(y):
    w = y.shape[1]
    first = (_iota(y.shape, 1) & (HEAD_DIM // 2)) == 0
    return jnp.where(first, pltpu.roll(y, w - HEAD_DIM // 2, 1), pltpu.roll(y, HEAD_DIM // 2, 1))


def _norm_rope(x, gain, cos, sin_signed):
    reps = x.shape[1] // LANES
    y = x * lax.rsqrt(_group_mean_sq(x) + NORM_EPS) * gain
    c = cos if reps == 1 else jnp.concatenate([cos] * reps, axis=1)
    s = sin_signed if reps == 1 else jnp.concatenate([sin_signed] * reps, axis=1)
    return y * c + _rot_half(y) * s


def _stack_heads(q):
    head = _iota(q.shape, 1) // HEAD_DIM
    return jnp.concatenate([jnp.where(head == h, q, jnp.zeros_like(q)) for h in range(N_HEADS)], axis=0)


def _unstack_heads(o4, m):
    head = _iota((m, WIDTH), 1) // HEAD_DIM
    out = jnp.zeros((m, WIDTH), F32)
    for h in range(N_HEADS):
        out = jnp.where(head == h, o4[h * m:(h + 1) * m], out)
    return out


def _rope_table_kernel(pos_ref, freq_ref, sign_ref, cos_ref, sin_ref):
    ang = pos_ref[...].astype(F32) * freq_ref[...]
    cos_ref[...] = jnp.cos(ang)
    sin_ref[...] = jnp.sin(ang) * sign_ref[...]


def _rope_tables(pos_flat):
    n = pos_flat.shape[0]
    half = HEAD_DIM // 2
    inv_freq = jnp.power(ROPE_THETA, -jnp.arange(half, dtype=F32) / half)
    freq = jnp.tile(inv_freq, LANES // half)[None, :]
    sign = jnp.tile(jnp.concatenate([-jnp.ones((half,), F32), jnp.ones((half,), F32)]), LANES // HEAD_DIM)[None, :]
    tm = 512 if n % 512 == 0 else n
    return pl.pallas_call(
        _rope_table_kernel,
        grid=(n // tm,),
        in_specs=[pl.BlockSpec((tm, 1), lambda i: (i, 0)),
                  pl.BlockSpec((1, LANES), lambda i: (0, 0)),
                  pl.BlockSpec((1, LANES), lambda i: (0, 0))],
        out_specs=[pl.BlockSpec((tm, LANES), lambda i: (i, 0))] * 2,
        out_shape=[jax.ShapeDtypeStruct((n, LANES), F32)] * 2,
        name="rope_tables",
    )(pos_flat[:, None], freq, sign)


def _inproj_kernel(x_ref, ng_ref, wqk_ref, wv_ref, wc_ref, wg_ref, wz_ref, wsc_ref, wsb_ref, ws5_ref,
                   qkg_ref, cos_ref, sin_ref, convw_ref,
                   q_ref, ks_ref, kw_ref, vs_ref, vw_ref, kvc_ref, gate_ref, zn_ref, zsb_ref, zs5_ref,
                   sc_ref, sbq_ref, sbk_ref, sbv_ref, s5u_ref, ubuf_ref, *, tm):
    x = x_ref[0]
    ms = jnp.mean(x * x, axis=-1, keepdims=True)
    h = (x * lax.rsqrt(ms + NORM_EPS) * ng_ref[...]).astype(BF16)

    qk = _norm_rope(_dot(h, wqk_ref[...]), qkg_ref[...], cos_ref[0], sin_ref[0])
    q_ref[0] = (qk[:, :WIDTH] * QK_SCALE).astype(BF16)
    ks_ref[0] = qk[:, WIDTH:2 * WIDTH].astype(BF16)
    kw_ref[0] = qk[:, 2 * WIDTH:].astype(BF16)

    v = _dot(h, wv_ref[...])
    vs_ref[0] = v[:, :WIDTH].astype(BF16)
    vw_ref[0] = v[:, WIDTH:].astype(BF16)
    kvc_ref[0] = _dot(h, wc_ref[...])
    gate_ref[0] = _sigmoid(_dot(h, wg_ref[...]))

    z = _silu(_dot(h, wz_ref[...]))
    zn_ref[0] = z[:, :WIDTH]
    zsb_ref[0] = z[:, 2 * WIDTH:3 * WIDTH]
    zs5_ref[0] = z[:, 3 * WIDTH:]

    bcx = _dot(h, wsc_ref[...])
    u = bcx[:, WIDTH:2 * WIDTH] * bcx[:, 2 * WIDTH:]

    @pl.when(pl.program_id(1) == 0)
    def _():
        ubuf_ref[0:SUBLANES, :] = jnp.zeros((SUBLANES, WIDTH), F32)

    ubuf_ref[SUBLANES:SUBLANES + tm, :] = u
    u1 = ubuf_ref[SUBLANES - 1:SUBLANES - 1 + tm, :]
    u2 = ubuf_ref[SUBLANES - 2:SUBLANES - 2 + tm, :]
    cw = convw_ref[...]
    y = cw[2:3, :] * u + cw[1:2, :] * u1 + cw[0:1, :] * u2
    ubuf_ref[0:SUBLANES, :] = ubuf_ref[tm:tm + SUBLANES, :]
    sc_ref[0] = (bcx[:, :WIDTH] * y * z[:, WIDTH:2 * WIDTH]).astype(BF16)

    sb = _dot(h, wsb_ref[...])
    sbq_ref[0] = (sb[:, :WIDTH] * QK_SCALE).astype(BF16)
    sbk_ref[0] = sb[:, WIDTH:2 * WIDTH].astype(BF16)
    sbv_ref[0] = sb[:, 2 * WIDTH:].astype(BF16)
    s5u_ref[0] = _dot(h, ws5_ref[...])


def _inproj(x, ng, weights, qkg, cos, sin, convw):
    b, s, d = x.shape
    tm = min(TM_PROJ, s)
    full2 = lambda a: pl.BlockSpec(a.shape, lambda bi, si: (0, 0))
    tok = lambda w: pl.BlockSpec((1, tm, w), lambda bi, si: (bi, si, 0))
    out_widths = [(WIDTH, BF16)] * 5 + [(WIDTH, F32), (3 * WIDTH, F32)] + [(WIDTH, F32)] * 3 \
        + [(WIDTH, BF16)] * 4 + [(WIDTH, F32)]
    return pl.pallas_call(
        functools.partial(_inproj_kernel, tm=tm),
        grid=(b, s // tm),
        in_specs=[tok(d), full2(ng)] + [full2(w) for w in weights] + [full2(qkg), tok(LANES), tok(LANES), full2(convw)],
        out_specs=[tok(w) for w, _ in out_widths],
        out_shape=[jax.ShapeDtypeStruct((b, s, w), dt) for w, dt in out_widths],
        scratch_shapes=[pltpu.VMEM((tm + 2 * SUBLANES, WIDTH), F32)],
        compiler_params=pltpu.CompilerParams(dimension_semantics=("arbitrary", "arbitrary"),
                                             vmem_limit_bytes=VMEM_LIMIT),
        name="inproj",
    )(x, ng, *weights, qkg, cos, sin, convw)


def _compress_kernel(x_ref, pe_ref, w1_ref, w2_ref, g_ref, cos_ref, sin_ref, kc_ref, vc_ref, *, nk):
    for j, out_ref in ((0, kc_ref), (1, vc_ref)):
        acc = jnp.zeros((nk, WIDTH), F32)
        for kh in range(NSA_KV_HEADS):
            xa = x_ref[0, 2 * j + kh]
            lo = _dot((xa + pe_ref[j, 0:1, :]).astype(BF16), w1_ref[j, 0])
            hi = _dot((xa + pe_ref[j, 1:2, :]).astype(BF16), w1_ref[j, 1])
            hid = lo + pltpu.roll(hi, nk - 1, 0)
            acc = acc + _dot(_silu(hid).astype(BF16), w2_ref[j, kh])
        if j == 0:
            acc = _norm_rope(acc, g_ref[...], cos_ref[0], sin_ref[0])
        out_ref[0] = acc.astype(BF16)


def _compress(xh, pe, w1, w2, g, cos, sin):
    b, _, nk, _ = xh.shape
    full = lambda a: pl.BlockSpec(a.shape, lambda bi: (0,) * a.ndim)
    return pl.pallas_call(
        functools.partial(_compress_kernel, nk=nk),
        grid=(b,),
        in_specs=[pl.BlockSpec((1,) + xh.shape[1:], lambda bi: (bi, 0, 0, 0)), full(pe), full(w1), full(w2), full(g),
                  pl.BlockSpec((1, nk, LANES), lambda bi: (bi, 0, 0)),
                  pl.BlockSpec((1, nk, LANES), lambda bi: (bi, 0, 0))],
        out_specs=[pl.BlockSpec((1, nk, WIDTH), lambda bi: (bi, 0, 0))] * 2,
        out_shape=[jax.ShapeDtypeStruct((b, nk, WIDTH), BF16)] * 2,
        compiler_params=pltpu.CompilerParams(dimension_semantics=("arbitrary",), vmem_limit_bytes=VMEM_LIMIT),
        name="compress",
    )(xh, pe, w1, w2, g, cos, sin)


def _cmp_topk_kernel(q_ref, kc_ref, vc_ref, ocmp_ref, bias_ref, p_scr, imp_scr, *, tt, nk, n_cmp, n_sel, top_n):
    t0 = pl.program_id(1) * tt
    q = q_ref[0]
    kc = kc_ref[0]
    head = _iota(q.shape, 1) // HEAD_DIM

    s = _dot_nt(_stack_heads(q), kc)
    trow = t0 + (_iota(s.shape, 0) & (tt - 1))
    ncol = _iota(s.shape, 1)
    valid = (ncol * NSA_CMP_STRIDE + (NSA_CMP_LEN - 1) <= trow) & (ncol < n_cmp)
    s = jnp.where(valid, s, NEG)
    e = jnp.where(valid, jnp.exp(s - jnp.max(s, axis=-1, keepdims=True)), 0.0)
    p = e * (1.0 / jnp.maximum(jnp.sum(e, axis=-1, keepdims=True), 1e-30))
    ocmp_ref[0] = _unstack_heads(_dot(p.astype(BF16), vc_ref[0]), tt)

    nrow = _iota((nk, tt), 0)
    tcol = t0 + _iota((nk, tt), 1)
    valid_t = (nrow * NSA_CMP_STRIDE + (NSA_CMP_LEN - 1) <= tcol) & (nrow < n_cmp)
    brow = _iota((n_sel, tt), 0)
    tsel = t0 + _iota((n_sel, tt), 1)
    cur = tsel // NSA_SEL_LEN
    forced = (brow == 0) | (brow == cur) | (brow == cur - 1)
    in_past = brow * NSA_SEL_LEN <= tsel
    p_scr[0:SUBLANES, :] = jnp.zeros((SUBLANES, tt), F32)
    biases = []
    for kh in range(NSA_KV_HEADS):
        psum = jnp.zeros((nk, tt), F32)
        for g in range(N_HEADS // NSA_KV_HEADS):
            qh = jnp.where(head == kh * (N_HEADS // NSA_KV_HEADS) + g, q, jnp.zeros_like(q))
            st = jnp.where(valid_t, _dot_nt(kc, qh), NEG)
            et = jnp.where(valid_t, jnp.exp(st - jnp.max(st, axis=0, keepdims=True)), 0.0)
            psum = psum + et * (1.0 / jnp.maximum(jnp.sum(et, axis=0, keepdims=True), 1e-30))
        p_scr[SUBLANES:SUBLANES + nk, :] = psum
        ratio = NSA_SEL_LEN // NSA_CMP_STRIDE
        imp = p_scr[pl.ds(SUBLANES - 1, n_sel, stride=ratio), :]
        for k in range(1, NSA_CMP_LEN // NSA_CMP_STRIDE + ratio - 1):
            imp = imp + p_scr[pl.ds(SUBLANES - 1 + k, n_sel, stride=ratio), :]
        imp = jnp.where(forced, NSA_FORCE_SCORE, jnp.where(in_past, imp, -NSA_FORCE_SCORE))
        imp_scr[...] = imp

        def body(bp, cnt, imp=imp):
            row = imp_scr[pl.ds(bp, 1), :]
            beats = (row > imp) | ((row == imp) & (brow > bp))
            return cnt + jnp.where(beats, 1.0, 0.0)

        rank = lax.fori_loop(0, n_sel, body, jnp.zeros((n_sel, tt), F32))
        biases.append(jnp.where(rank < top_n, 0.0, SEL_BIAS))
    bias_ref[0] = jnp.concatenate(biases, axis=0).T.astype(BF16)


def _cmp_topk(q, kc, vc, n_cmp):
    b, s, _ = q.shape
    nk = kc.shape[1]
    n_sel = s // NSA_SEL_LEN
    tt = min(TT_CMP, s)
    kern = functools.partial(_cmp_topk_kernel, tt=tt, nk=nk, n_cmp=n_cmp, n_sel=n_sel, top_n=min(NSA_SEL_TOPN, n_sel))
    return pl.pallas_call(
        kern,
        grid=(b, s // tt),
        in_specs=[pl.BlockSpec((1, tt, WIDTH), lambda bi, ti: (bi, ti, 0)),
                  pl.BlockSpec((1, nk, WIDTH), lambda bi, ti: (bi, 0, 0)),
                  pl.BlockSpec((1, nk, WIDTH), lambda bi, ti: (bi, 0, 0))],
        out_specs=[pl.BlockSpec((1, tt, WIDTH), lambda bi, ti: (bi, ti, 0)),
                   pl.BlockSpec((1, tt, NSA_KV_HEADS * n_sel), lambda bi, ti: (bi, ti, 0))],
        out_shape=[jax.ShapeDtypeStruct((b, s, WIDTH), F32),
                   jax.ShapeDtypeStruct((b, s, NSA_KV_HEADS * n_sel), BF16)],
        scratch_shapes=[pltpu.VMEM((nk + SUBLANES, tt), F32), pltpu.VMEM((n_sel, tt), F32)],
        compiler_params=pltpu.CompilerParams(dimension_semantics=("arbitrary", "arbitrary"),
                                             vmem_limit_bytes=VMEM_LIMIT),
        name="cmp_topk",
    )(q, kc, vc)


def _selwin_kernel(q_ref, bias_ref, ocmp_ref, gate_ref, zn_ref, ks_ref, vs_ref, kw_ref, vw_ref, out_ref,
                   acc_scr, m_scr, l_scr, *, tq, tk, n_sel):
    t0 = pl.program_id(1) * tq
    jd = t0 // tk
    qs = _stack_heads(q_ref[0])
    bias = bias_ref[0]
    kvh = _iota(bias.shape, 1) // n_sel
    bs = jnp.concatenate([jnp.where(kvh == h // (N_HEADS // NSA_KV_HEADS), bias, jnp.zeros_like(bias))
                          for h in range(N_HEADS)], axis=0)
    trow = t0 + (_iota((N_HEADS * tq, tk), 0) & (tq - 1))

    def run(branch, k_ref, v_ref, n_tiles, windowed):
        acc_scr[branch] = jnp.zeros((N_HEADS * tq, WIDTH), F32)
        m_scr[branch] = jnp.full((N_HEADS * tq, 1), NEG, F32)
        l_scr[branch] = jnp.zeros((N_HEADS * tq, 1), F32)

        def body(i, carry):
            k0 = pl.multiple_of((jd - i) * tk, tk)
            s = _dot_nt(qs, k_ref[0, pl.ds(k0, tk), :])
            kpos = k0 + _iota(s.shape, 1)
            if windowed:
                mask = (kpos <= trow) & (kpos > trow - NSA_WINDOW)
            else:
                blk = (k0 + _iota((tk, NSA_KV_HEADS * n_sel), 0)) // NSA_SEL_LEN
                onehot = jnp.where((_iota(blk.shape, 1) & (n_sel - 1)) == blk, 1.0, 0.0).astype(BF16)
                s = s + _dot_nt(bs, onehot)
                mask = kpos <= trow
            s = jnp.where(mask, s, NEG)
            m_old = m_scr[branch]
            m_new = jnp.maximum(m_old, jnp.max(s, axis=-1, keepdims=True))
            alpha = jnp.exp(m_old - m_new)
            p = jnp.where(mask, jnp.exp(s - m_new), 0.0)
            l_scr[branch] = alpha * l_scr[branch] + jnp.sum(p, axis=-1, keepdims=True)
            acc_scr[branch] = alpha * acc_scr[branch] + _dot(p.astype(BF16), v_ref[0, pl.ds(k0, tk), :])
            m_scr[branch] = m_new
            return carry

        lax.fori_loop(0, n_tiles, body, 0)
        inv = 1.0 / jnp.maximum(l_scr[branch], 1e-30)
        return _unstack_heads(acc_scr[branch] * inv, tq)

    o_slc = run(0, ks_ref, vs_ref, jd + 1, False)
    win_tiles = (NSA_WINDOW + tq - 2) // tk + 2
    o_win = run(1, kw_ref, vw_ref, jnp.minimum(jd + 1, win_tiles), True)
    gate = gate_ref[0]
    o = gate[:, :WIDTH] * ocmp_ref[0] + gate[:, WIDTH:2 * WIDTH] * o_slc + gate[:, 2 * WIDTH:] * o_win
    out_ref[0] = (o * zn_ref[0]).astype(BF16)


def _selwin(q, bias, ocmp, gate, zn, ks, vs, kw, vw):
    b, s, _ = q.shape
    tq, tk = min(TQ_ATT, s), min(TK_ATT, s)
    n_sel = s // NSA_SEL_LEN
    tok = lambda w: pl.BlockSpec((1, tq, w), lambda bi, qi: (bi, qi, 0))
    seq = pl.BlockSpec((1, s, WIDTH), lambda bi, qi: (bi, 0, 0))
    return pl.pallas_call(
        functools.partial(_selwin_kernel, tq=tq, tk=tk, n_sel=n_sel),
        grid=(b, s // tq),
        in_specs=[tok(WIDTH), tok(NSA_KV_HEADS * n_sel), tok(WIDTH), tok(3 * WIDTH), tok(WIDTH), seq, seq, seq, seq],
        out_specs=tok(WIDTH),
        out_shape=jax.ShapeDtypeStruct((b, s, WIDTH), BF16),
        scratch_shapes=[pltpu.VMEM((2, N_HEADS * tq, WIDTH), F32), pltpu.VMEM((2, N_HEADS * tq, 1), F32),
                        pltpu.VMEM((2, N_HEADS * tq, 1), F32)],
        compiler_params=pltpu.CompilerParams(dimension_semantics=("arbitrary", "arbitrary"),
                                             vmem_limit_bytes=VMEM_LIMIT),
        name="selwin",
    )(q, bias, ocmp, gate, zn, ks, vs, kw, vw)


def _stickbrk_kernel(q_ref, k_ref, v_ref, z_ref, out_ref, acc_scr, carry_scr, *, tq, tk):
    t0 = pl.program_id(1) * tq
    jd = t0 // tk
    qs = _stack_heads(q_ref[0])
    rows = N_HEADS * tq
    trow = t0 + (_iota((rows, tk), 0) & (tq - 1))
    tri = jnp.where(_iota((tk, tk), 0) >= _iota((tk, tk), 1), 1.0, 0.0).astype(BF16)
    acc_scr[...] = jnp.zeros((rows, WIDTH), F32)
    carry_scr[...] = jnp.zeros((rows, 1), F32)

    def cond(c):
        i, dead = c
        return (i <= jd) & (dead == 0)

    def body(c):
        i, _ = c
        k0 = pl.multiple_of((jd - i) * tk, tk)
        z = _dot_nt(qs, k_ref[0, pl.ds(k0, tk), :])
        mask = (k0 + _iota(z.shape, 1)) < trow
        softplus = jnp.maximum(z, 0.0) + jnp.log1p(jnp.exp(-jnp.abs(z)))
        log1mb = jnp.where(mask, -softplus, 0.0)
        hi = log1mb.astype(BF16)
        lo = (log1mb - hi.astype(F32)).astype(BF16)
        suffix = _dot(hi, tri) + _dot(lo, tri)
        carry = carry_scr[...]
        w = jnp.where(mask, jnp.exp(z + suffix + carry), 0.0)
        acc_scr[...] += _dot(w.astype(BF16), v_ref[0, pl.ds(k0, tk), :])
        carry = carry + jnp.sum(log1mb, axis=-1, keepdims=True)
        carry_scr[...] = carry
        dead = (jnp.max(carry) < SB_DEAD).astype(jnp.int32)
        return i + 1, dead

    lax.while_loop(cond, body, (jnp.int32(0), jnp.int32(0)))
    out_ref[0] = (_unstack_heads(acc_scr[...], tq) * z_ref[0]).astype(BF16)


def _stickbrk(q, k, v, z):
    b, s, _ = q.shape
    tq, tk = min(TQ_ATT, s), min(TK_ATT, s)
    tok = pl.BlockSpec((1, tq, WIDTH), lambda bi, qi: (bi, qi, 0))
    seq = pl.BlockSpec((1, s, WIDTH), lambda bi, qi: (bi, 0, 0))
    return pl.pallas_call(
        functools.partial(_stickbrk_kernel, tq=tq, tk=tk),
        grid=(b, s // tq),
        in_specs=[tok, seq, seq, tok],
        out_specs=tok,
        out_shape=jax.ShapeDtypeStruct((b, s, WIDTH), BF16),
        scratch_shapes=[pltpu.VMEM((N_HEADS * tq, WIDTH), F32), pltpu.VMEM((N_HEADS * tq, 1), F32)],
        compiler_params=pltpu.CompilerParams(dimension_semantics=("arbitrary", "arbitrary"),
                                             vmem_limit_bytes=VMEM_LIMIT),
        name="stickbrk",
    )(q, k, v, z)


def _s5_disc_kernel(are_ref, aim_ref, ldt_ref, bre_ref, bim_ref, abre_ref, abim_ref, bbre_ref, bbim_ref):
    dt = jnp.exp(ldt_ref[...])
    lr, li = are_ref[...], aim_ref[...]
    mag = jnp.exp(lr * dt)
    ab_re, ab_im = mag * jnp.cos(li * dt), mag * jnp.sin(li * dt)
    den = lr * lr + li * li
    coef_re = ((ab_re - 1.0) * lr + ab_im * li) / den
    coef_im = (ab_im * lr - (ab_re - 1.0) * li) / den
    abre_ref[...] = ab_re
    abim_ref[...] = ab_im
    br, bi = bre_ref[...], bim_ref[...]
    bbre_ref[...] = coef_re[:, None, :] * br - coef_im[:, None, :] * bi
    bbim_ref[...] = coef_re[:, None, :] * bi + coef_im[:, None, :] * br


def _s5_discretise(a_re, a_im, log_dt, b_re, b_im):
    g, p = a_re.shape
    brt, bit = jnp.swapaxes(b_re, 1, 2), jnp.swapaxes(b_im, 1, 2)
    return pl.pallas_call(
        _s5_disc_kernel,
        out_shape=[jax.ShapeDtypeStruct((g, p), F32)] * 2 + [jax.ShapeDtypeStruct(brt.shape, F32)] * 2,
        name="s5_disc",
    )(a_re, a_im, log_dt[:, None], brt, bit)


def _s5_kernel(u_ref, z_ref, bmat_ref, a_ref, cmat_ref, d_ref, gw_ref, gb_ref, out_ref, xs_scr, state_scr,
               *, nb, ts, ns):
    @pl.when(pl.program_id(0) == 0)
    def _():
        state_scr[...] = jnp.zeros((nb, 2 * ns), F32)

    nc = 2 * ns // LANES
    for b in range(nb):
        bu = _dot(u_ref[b].astype(BF16), bmat_ref[...])
        for c in range(nc):
            xs_scr[c, pl.ds(b, ts, stride=nb), :] = bu[:, c * LANES:(c + 1) * LANES]

    a_re = jnp.broadcast_to(a_ref[0:1, :], (nb, ns))
    a_im = jnp.broadcast_to(a_ref[1:2, :], (nb, ns))

    def step(t, state):
        x_re, x_im = state
        r0 = pl.multiple_of(t * nb, nb)
        bu = jnp.concatenate([xs_scr[c, pl.ds(r0, nb), :] for c in range(nc)], axis=1)
        n_re = a_re * x_re - a_im * x_im + bu[:, :ns]
        n_im = a_re * x_im + a_im * x_re + bu[:, ns:]
        for c in range(nc // 2):
            xs_scr[c, pl.ds(r0, nb), :] = n_re[:, c * LANES:(c + 1) * LANES]
            xs_scr[nc // 2 + c, pl.ds(r0, nb), :] = n_im[:, c * LANES:(c + 1) * LANES]
        return n_re, n_im

    st = state_scr[...]
    x_re, x_im = lax.fori_loop(0, ts, step, (st[:, :ns], st[:, ns:]))
    state_scr[...] = jnp.concatenate([x_re, x_im], axis=1)

    width = u_ref.shape[-1]
    for b in range(nb):
        xs = jnp.concatenate([xs_scr[c, pl.ds(b, ts, stride=nb), :] for c in range(nc)], axis=1)
        y = _dot(xs.astype(BF16), cmat_ref[...]) + d_ref[...] * u_ref[b]
        glu = _dot(y.astype(BF16), gw_ref[...]) + gb_ref[...]
        out_ref[b] = (glu[:, :width] * _sigmoid(glu[:, width:]) * z_ref[b]).astype(BF16)


def _s5(u, z, bmat, a_rows, cmat, d_row, glu_w, glu_b):
    nb, s, w = u.shape
    ts = min(TS_S5, s)
    ns = a_rows.shape[1]
    full = lambda a: pl.BlockSpec(a.shape, lambda i: (0, 0))
    tok = pl.BlockSpec((nb, ts, w), lambda i: (0, i, 0))
    return pl.pallas_call(
        functools.partial(_s5_kernel, nb=nb, ts=ts, ns=ns),
        grid=(s // ts,),
        in_specs=[tok, tok, full(bmat), full(a_rows), full(cmat), full(d_row), full(glu_w), full(glu_b)],
        out_specs=tok,
        out_shape=jax.ShapeDtypeStruct((nb, s, w), BF16),
        scratch_shapes=[pltpu.VMEM((2 * ns // LANES, ts * nb, LANES), F32), pltpu.VMEM((nb, 2 * ns), F32)],
        compiler_params=pltpu.CompilerParams(dimension_semantics=("arbitrary",), vmem_limit_bytes=VMEM_LIMIT),
        name="s5",
    )(u, z, bmat, a_rows, cmat, d_row, glu_w, glu_b)


def _merge_kernel(x_ref, ng_ref, o0_ref, o1_ref, o2_ref, o3_ref, wm_ref, wb_ref, wo_ref, out_ref):
    x = x_ref[...]
    ms = jnp.mean(x * x, axis=-1, keepdims=True)
    h = (x * lax.rsqrt(ms + NORM_EPS) * ng_ref[...]).astype(BF16)
    d = x.shape[1]
    mixed = None
    for m, o_ref in enumerate((o0_ref, o1_ref, o2_ref, o3_ref)):
        gate = _sigmoid(_dot(h, wm_ref[:, m * d:(m + 1) * d]))
        term = gate * _dot(o_ref[...], wb_ref[m])
        mixed = term if mixed is None else mixed + term
    out_ref[...] = x + _dot(mixed.astype(BF16), wo_ref[...])


def _merge(x2, ng, outs, wm, wb, wo):
    t, d = x2.shape
    tm = min(TM_PROJ, t)
    tok = lambda w: pl.BlockSpec((tm, w), lambda i: (i, 0))
    full = lambda a: pl.BlockSpec(a.shape, lambda i: (0,) * a.ndim)
    return pl.pallas_call(
        _merge_kernel,
        grid=(t // tm,),
        in_specs=[tok(d), full(ng)] + [tok(WIDTH)] * 4 + [full(wm), full(wb), full(wo)],
        out_specs=tok(d),
        out_shape=jax.ShapeDtypeStruct((t, d), F32),
        compiler_params=pltpu.CompilerParams(dimension_semantics=("arbitrary",), vmem_limit_bytes=VMEM_LIMIT),
        name="merge",
    )(x2, ng, *outs, wm, wb, wo)


def _rep_heads(w):
    h0, h1 = w[..., :HEAD_DIM], w[..., HEAD_DIM:]
    return jnp.concatenate([h0, h0, h1, h1], axis=-1)


def _block_diag(blocks):
    g, r, c = blocks.shape
    eye = jnp.eye(g, dtype=blocks.dtype)
    return (eye[:, None, :, None] * blocks[:, :, None, :]).reshape(g * r, g * c)


def _layer(x, cos, sin, cos_c, sin_c, n_cmp, norm_g, w_in, qk_g, cmp_pe, cmp_w1, cmp_w2, conv_w,
           a_re, a_im, log_dt, b_re, b_im, c_re, c_im, d_skip, glu_w, glu_b, w_branch, w_out):
    b, s, d = x.shape
    w = WIDTH
    o_q, o_kv, o_gate = 0, w, 4 * w
    o_nz = o_gate + 3 * N_HEADS
    o_sc, o_scz, o_sb, o_sbz, o_s5, o_s5z = o_nz + w, o_nz + 4 * w, o_nz + 5 * w, o_nz + 8 * w, o_nz + 9 * w, o_nz + 10 * w
    o_merge = o_nz + 11 * w
    cols = lambda a, n: w_in[:, a:a + n]
    kv = lambda i: cols(o_kv + i * (w // 2), w // 2)
    wgate = jnp.repeat(cols(o_gate, 3 * N_HEADS), HEAD_DIM, axis=1)
    weights = [
        jnp.concatenate([cols(o_q, w), _rep_heads(kv(2)), _rep_heads(kv(4))], axis=1),
        jnp.concatenate([_rep_heads(kv(3)), _rep_heads(kv(5))], axis=1),
        jnp.concatenate([kv(0), kv(1)], axis=1),
        wgate,
        jnp.concatenate([cols(o_nz, w), cols(o_scz, w), cols(o_sbz, w), cols(o_s5z, w)], axis=1),
        cols(o_sc, 3 * w), cols(o_sb, 3 * w), cols(o_s5, w),
    ]
    weights = [a.astype(BF16) for a in weights]
    qkg = jnp.concatenate([jnp.tile(qk_g[0], N_HEADS), jnp.tile(qk_g[2], N_HEADS), jnp.tile(qk_g[3], N_HEADS)])[None, :]
    ng = norm_g[None, :]

    (q, ks, kw, vs, vw, kvc, gate, zn, zsb, zs5, sc_o, sbq, sbk, sbv, s5u) = _inproj(
        x, ng, weights, qkg, cos, sin, conv_w)

    half = NSA_CMP_STRIDE
    nk = s // half
    xh = kvc.reshape(b, nk, half, 4, HEAD_DIM).transpose(0, 3, 1, 2, 4).reshape(b, 4, nk, half * HEAD_DIM)
    pe = cmp_pe.reshape(2, NSA_CMP_LEN // half, half * HEAD_DIM)
    w1 = cmp_w1.reshape(2, NSA_CMP_LEN // half, half * HEAD_DIM, HEAD_DIM).astype(BF16)
    zeros = jnp.zeros_like(cmp_w2)
    w2 = jnp.stack([jnp.concatenate([cmp_w2, cmp_w2, zeros, zeros], axis=-1),
                    jnp.concatenate([zeros, zeros, cmp_w2, cmp_w2], axis=-1)], axis=1).astype(BF16)
    kc, vc = _compress(xh, pe, w1, w2, jnp.tile(qk_g[1], N_HEADS)[None, :], cos_c, sin_c)

    ocmp, bias = _cmp_topk(q, kc, vc, n_cmp)
    nsa_o = _selwin(q, bias, ocmp, gate, zn, ks, vs, kw, vw)
    sb_o = _stickbrk(sbq, sbk, sbv, zsb)

    ab_re, ab_im, bb_re, bb_im = _s5_discretise(a_re, a_im, log_dt, b_re, b_im)
    bmat = jnp.concatenate([_block_diag(bb_re), _block_diag(bb_im)], axis=1).astype(BF16)
    cmat = jnp.concatenate([_block_diag(jnp.swapaxes(c_re, 1, 2)),
                            -_block_diag(jnp.swapaxes(c_im, 1, 2))], axis=0).astype(BF16)
    a_rows = jnp.stack([ab_re.reshape(-1), ab_im.reshape(-1)])
    s5_o = _s5(s5u, zs5, bmat, a_rows, cmat, d_skip.reshape(1, -1), glu_w.astype(BF16), glu_b[None, :])

    out = _merge(x.reshape(b * s, d), ng,
                 [o.reshape(b * s, w) for o in (nsa_o, sc_o, sb_o, s5_o)],
                 cols(o_merge, N_HEADS * d).astype(BF16), w_branch.astype(BF16), w_out.astype(BF16))
    return out.reshape(b, s, d)


def kernel(x, positions, norm_g, w_in, nsa_qk_g, nsa_cmp_pe, nsa_cmp_w1, nsa_cmp_w2, sc_conv_w, s5_a_re, s5_a_im,
           s5_log_dt, s5_b_re, s5_b_im, s5_c_re, s5_c_im, s5_d, s5_glu_w, s5_glu_b, w_branch, w_out):
    b, s, _ = x.shape
    assert s % TK_ATT == 0 or s < TK_ATT
    cos, sin = _rope_tables(positions.reshape(-1))
    cos, sin = cos.reshape(b, s, LANES), sin.reshape(b, s, LANES)
    nk = s // NSA_CMP_STRIDE
    n_cmp = (s - NSA_CMP_LEN) // NSA_CMP_STRIDE + 1
    pos_c = jnp.concatenate([positions[:, NSA_CMP_LEN - 1::NSA_CMP_STRIDE],
                             jnp.zeros((b, nk - n_cmp), positions.dtype)], axis=1)
    cos_c, sin_c = _rope_tables(pos_c.reshape(-1))
    cos_c, sin_c = cos_c.reshape(b, nk, LANES), sin_c.reshape(b, nk, LANES)
    for l in range(norm_g.shape[0]):
        x = _layer(x, cos, sin, cos_c, sin_c, n_cmp, norm_g[l], w_in[l], nsa_qk_g[l], nsa_cmp_pe[l], nsa_cmp_w1[l],
                   nsa_cmp_w2[l], sc_conv_w[l], s5_a_re[l], s5_a_im[l], s5_log_dt[l], s5_b_re[l], s5_b_im[l],
                   s5_c_re[l], s5_c_im[l], s5_d[l], s5_glu_w[l], s5_glu_b[l], w_branch[l], w_out[l])
    return x
```

```python
import functools
import math

import jax
import jax.numpy as jnp
from jax import lax
from jax.experimental import pallas as pl
from jax.experimental.pallas import tpu as pltpu

F32 = jnp.float32
BF16 = jnp.bfloat16

HEAD_DIM = 64
N_HEADS = 4
WIDTH = N_HEADS * HEAD_DIM
NSA_KV_HEADS = 2
NSA_CMP_LEN = 32
NSA_CMP_STRIDE = 16
NSA_SEL_LEN = 64
NSA_SEL_TOPN = 16
NSA_WINDOW = 512
NSA_FORCE_SCORE = 1.0e4
S5_GROUPS = 16
S5_GROUP_CH = 16
S5_STATE = 64
ROPE_THETA = 10000.0
NORM_EPS = 1e-6
QK_SCALE = HEAD_DIM ** -0.5

LANES = 128
SUBLANES = 8
NEG = -1.0e30
SEL_BIAS = -30000.0
SB_DEAD = -120.0
VMEM_LIMIT = 56 * 1024 * 1024

TM_PROJ = 256
TT_CMP = 128
TQ_ATT = 128
TK_ATT = 256
TK_SEL = 512
BIAS_SLOT = (2, 0)
TS_S5 = 128


def _dot(a, b):
    return jnp.dot(a, b, preferred_element_type=F32)


def _dot_nt(a, b):
    return lax.dot_general(a, b, (((1,), (1,)), ((), ())), preferred_element_type=F32)


def _silu(x):
    return x * (1.0 / (1.0 + jnp.exp(-x)))


def _sigmoid(x):
    return 1.0 / (1.0 + jnp.exp(-x))


def _iota(shape, dim):
    return lax.broadcasted_iota(jnp.int32, shape, dim)


def _group_mean_sq(x):
    outs = []
    lane = _iota((x.shape[0], LANES), 1)
    low = lane < HEAD_DIM
    for c in range(x.shape[1] // LANES):
        xc = x[:, c * LANES:(c + 1) * LANES]
        sq = xc * xc
        s_lo = jnp.sum(jnp.where(low, sq, 0.0), axis=-1, keepdims=True)
        s_hi = jnp.sum(jnp.where(low, 0.0, sq), axis=-1, keepdims=True)
        outs.append(jnp.where(low, s_lo, s_hi) * (1.0 / HEAD_DIM))
    return outs[0] if len(outs) == 1 else jnp.concatenate(outs, axis=1)


def _rot_half(y):
    w = y.shape[1]
    first = (_iota(y.shape, 1) & (HEAD_DIM // 2)) == 0
    return jnp.where(first, pltpu.roll(y, w - HEAD_DIM // 2, 1), pltpu.roll(y, HEAD_DIM // 2, 1))


def _norm_rope(x, gain, cos, sin_signed):
    reps = x.shape[1] // LANES
    y = x * lax.rsqrt(_group_mean_sq(x) + NORM_EPS) * gain
    c = cos if reps == 1 else jnp.concatenate([cos] * reps, axis=1)
    s = sin_signed if reps == 1 else jnp.concatenate([sin_signed] * reps, axis=1)
    return y * c + _rot_half(y) * s


def _stack_heads(q):
    head = _iota(q.shape, 1) // HEAD_DIM
    return jnp.concatenate([jnp.where(head == h, q, jnp.zeros_like(q)) for h in range(N_HEADS)], axis=0)


def _unstack_heads(o4, m):
    head = _iota((m, WIDTH), 1) // HEAD_DIM
    out = jnp.zeros((m, WIDTH), F32)
    for h in range(N_HEADS):
        out = jnp.where(head == h, o4[h * m:(h + 1) * m], out)
    return out


def _rope_table_kernel(pos_ref, freq_ref, sign_ref, cos_ref, sin_ref):
    ang = pos_ref[...].astype(F32) * freq_ref[...]
    cos_ref[...] = jnp.cos(ang)
    sin_ref[...] = jnp.sin(ang) * sign_ref[...]


def _rope_tables(pos_flat):
    n = pos_flat.shape[0]
    half = HEAD_DIM // 2
    inv_freq = jnp.power(ROPE_THETA, -jnp.arange(half, dtype=F32) / half)
    freq = jnp.tile(inv_freq, LANES // half)[None, :]
    sign = jnp.tile(jnp.concatenate([-jnp.ones((half,), F32), jnp.ones((half,), F32)]), LANES // HEAD_DIM)[None, :]
    tm = 512 if n % 512 == 0 else n
    return pl.pallas_call(
        _rope_table_kernel,
        grid=(n // tm,),
        in_specs=[pl.BlockSpec((tm, 1), lambda i: (i, 0)),
                  pl.BlockSpec((1, LANES), lambda i: (0, 0)),
                  pl.BlockSpec((1, LANES), lambda i: (0, 0))],
        out_specs=[pl.BlockSpec((tm, LANES), lambda i: (i, 0))] * 2,
        out_shape=[jax.ShapeDtypeStruct((n, LANES), F32)] * 2,
        name="rope_tables",
    )(pos_flat[:, None], freq, sign)


def _inproj_kernel(x_ref, ng_ref, wqk_ref, wv_ref, wc_ref, wg_ref, wz_ref, wsc_ref, wsb_ref, ws5_ref,
                   qkg_ref, cos_ref, sin_ref, convw_ref,
                   q_ref, ks_ref, kw_ref, vs_ref, vw_ref, kvc_ref, gate_ref, zn_ref, zsb_ref, zs5_ref,
                   sc_ref, sbq_ref, sbk_ref, sbv_ref, s5u_ref, ubuf_ref, *, tm):
    x = x_ref[0]
    ms = jnp.mean(x * x, axis=-1, keepdims=True)
    h = (x * lax.rsqrt(ms + NORM_EPS) * ng_ref[...]).astype(BF16)

    qk = _norm_rope(_dot(h, wqk_ref[...]), qkg_ref[...], cos_ref[0], sin_ref[0])
    q_ref[0] = (qk[:, :WIDTH] * QK_SCALE).astype(BF16)
    ks_ref[0] = qk[:, WIDTH:2 * WIDTH].astype(BF16)
    kw_ref[0] = qk[:, 2 * WIDTH:].astype(BF16)

    v = _dot(h, wv_ref[...])
    v = jnp.where((_iota(v.shape, 1) // HEAD_DIM) % 2 == 1, 1.0, v)
    vs_ref[0] = v[:, :WIDTH].astype(BF16)
    vw_ref[0] = v[:, WIDTH:].astype(BF16)
    kvc_ref[0] = _dot(h, wc_ref[...])
    gate_ref[0] = _sigmoid(_dot(h, wg_ref[...]))

    z = _silu(_dot(h, wz_ref[...]))
    zn_ref[0] = z[:, :WIDTH]
    zsb_ref[0] = z[:, 2 * WIDTH:3 * WIDTH]
    zs5_ref[0] = z[:, 3 * WIDTH:]

    bcx = _dot(h, wsc_ref[...])
    u = bcx[:, WIDTH:2 * WIDTH] * bcx[:, 2 * WIDTH:]

    @pl.when(pl.program_id(1) == 0)
    def _():
        ubuf_ref[0:SUBLANES, :] = jnp.zeros((SUBLANES, WIDTH), F32)

    ubuf_ref[SUBLANES:SUBLANES + tm, :] = u
    u1 = ubuf_ref[SUBLANES - 1:SUBLANES - 1 + tm, :]
    u2 = ubuf_ref[SUBLANES - 2:SUBLANES - 2 + tm, :]
    cw = convw_ref[...]
    y = cw[2:3, :] * u + cw[1:2, :] * u1 + cw[0:1, :] * u2
    ubuf_ref[0:SUBLANES, :] = ubuf_ref[tm:tm + SUBLANES, :]
    sc_ref[0] = (bcx[:, :WIDTH] * y * z[:, WIDTH:2 * WIDTH]).astype(BF16)

    sb = _dot(h, wsb_ref[...])
    sbq_ref[0] = (sb[:, :WIDTH] * QK_SCALE).astype(BF16)
    sbk_ref[0] = sb[:, WIDTH:2 * WIDTH].astype(BF16)
    sbv_ref[0] = sb[:, 2 * WIDTH:].astype(BF16)
    s5u_ref[0] = _dot(h, ws5_ref[...])


def _inproj(x, ng, weights, qkg, cos, sin, convw):
    b, s, d = x.shape
    tm = min(TM_PROJ, s)
    full2 = lambda a: pl.BlockSpec(a.shape, lambda bi, si: (0, 0))
    tok = lambda w: pl.BlockSpec((1, tm, w), lambda bi, si: (bi, si, 0))
    out_widths = [(WIDTH, BF16)] * 5 + [(WIDTH, F32), (3 * WIDTH, F32)] + [(WIDTH, F32)] * 3 \
        + [(WIDTH, BF16)] * 4 + [(WIDTH, F32)]
    return pl.pallas_call(
        functools.partial(_inproj_kernel, tm=tm),
        grid=(b, s // tm),
        in_specs=[tok(d), full2(ng)] + [full2(w) for w in weights] + [full2(qkg), tok(LANES), tok(LANES), full2(convw)],
        out_specs=[tok(w) for w, _ in out_widths],
        out_shape=[jax.ShapeDtypeStruct((b, s, w), dt) for w, dt in out_widths],
        scratch_shapes=[pltpu.VMEM((tm + 2 * SUBLANES, WIDTH), F32)],
        compiler_params=pltpu.CompilerParams(dimension_semantics=("arbitrary", "arbitrary"),
                                             vmem_limit_bytes=VMEM_LIMIT),
        name="inproj",
    )(x, ng, *weights, qkg, cos, sin, convw)


def _compress_kernel(x_ref, pe_ref, w1_ref, w2_ref, g_ref, cos_ref, sin_ref, kc_ref, vc_ref, *, nk):
    for j, out_ref in ((0, kc_ref), (1, vc_ref)):
        acc = jnp.zeros((nk, WIDTH), F32)
        for kh in range(NSA_KV_HEADS):
            xa = x_ref[0, 2 * j + kh]
            lo = _dot((xa + pe_ref[j, 0:1, :]).astype(BF16), w1_ref[j, 0])
            hi = _dot((xa + pe_ref[j, 1:2, :]).astype(BF16), w1_ref[j, 1])
            hid = lo + pltpu.roll(hi, nk - 1, 0)
            acc = acc + _dot(_silu(hid).astype(BF16), w2_ref[j, kh])
        if j == 0:
            acc = _norm_rope(acc, g_ref[...], cos_ref[0], sin_ref[0])
        out_ref[0] = acc.astype(BF16)


def _compress(xh, pe, w1, w2, g, cos, sin):
    b, _, nk, _ = xh.shape
    full = lambda a: pl.BlockSpec(a.shape, lambda bi: (0,) * a.ndim)
    return pl.pallas_call(
        functools.partial(_compress_kernel, nk=nk),
        grid=(b,),
        in_specs=[pl.BlockSpec((1,) + xh.shape[1:], lambda bi: (bi, 0, 0, 0)), full(pe), full(w1), full(w2), full(g),
                  pl.BlockSpec((1, nk, LANES), lambda bi: (bi, 0, 0)),
                  pl.BlockSpec((1, nk, LANES), lambda bi: (bi, 0, 0))],
        out_specs=[pl.BlockSpec((1, nk, WIDTH), lambda bi: (bi, 0, 0))] * 2,
        out_shape=[jax.ShapeDtypeStruct((b, nk, WIDTH), BF16)] * 2,
        compiler_params=pltpu.CompilerParams(dimension_semantics=("arbitrary",), vmem_limit_bytes=VMEM_LIMIT),
        name="compress",
    )(xh, pe, w1, w2, g, cos, sin)


def _cmp_topk_kernel(q_ref, kc_ref, vc_ref, ocmp_ref, bias_ref, p_scr, imp_scr, *, tt, nk, n_cmp, n_sel, top_n):
    t0 = pl.program_id(1) * tt
    q = q_ref[0]
    kc = kc_ref[0]
    head = _iota(q.shape, 1) // HEAD_DIM

    s = _dot_nt(_stack_heads(q), kc)
    trow = t0 + (_iota(s.shape, 0) & (tt - 1))
    ncol = _iota(s.shape, 1)
    valid = (ncol * NSA_CMP_STRIDE + (NSA_CMP_LEN - 1) <= trow) & (ncol < n_cmp)
    s = jnp.where(valid, s, NEG)
    e = jnp.where(valid, jnp.exp(s - jnp.max(s, axis=-1, keepdims=True)), 0.0)
    p = e * (1.0 / jnp.maximum(jnp.sum(e, axis=-1, keepdims=True), 1e-30))
    ocmp_ref[0] = _unstack_heads(_dot(p.astype(BF16), vc_ref[0]), tt)

    nrow = _iota((nk, tt), 0)
    tcol = t0 + _iota((nk, tt), 1)
    valid_t = (nrow * NSA_CMP_STRIDE + (NSA_CMP_LEN - 1) <= tcol) & (nrow < n_cmp)
    brow = _iota((n_sel, tt), 0)
    tsel = t0 + _iota((n_sel, tt), 1)
    cur = tsel // NSA_SEL_LEN
    forced = (brow == 0) | (brow == cur) | (brow == cur - 1)
    in_past = brow * NSA_SEL_LEN <= tsel
    p_scr[0:SUBLANES, :] = jnp.zeros((SUBLANES, tt), F32)
    biases = []
    for kh in range(NSA_KV_HEADS):
        psum = jnp.zeros((nk, tt), F32)
        for g in range(N_HEADS // NSA_KV_HEADS):
            qh = jnp.where(head == kh * (N_HEADS // NSA_KV_HEADS) + g, q, jnp.zeros_like(q))
            st = jnp.where(valid_t, _dot_nt(kc, qh), NEG)
            et = jnp.where(valid_t, jnp.exp(st - jnp.max(st, axis=0, keepdims=True)), 0.0)
            psum = psum + et * (1.0 / jnp.maximum(jnp.sum(et, axis=0, keepdims=True), 1e-30))
        p_scr[SUBLANES:SUBLANES + nk, :] = psum
        ratio = NSA_SEL_LEN // NSA_CMP_STRIDE
        imp = p_scr[pl.ds(SUBLANES - 1, n_sel, stride=ratio), :]
        for k in range(1, NSA_CMP_LEN // NSA_CMP_STRIDE + ratio - 1):
            imp = imp + p_scr[pl.ds(SUBLANES - 1 + k, n_sel, stride=ratio), :]
        imp = jnp.where(forced, NSA_FORCE_SCORE, jnp.where(in_past, imp, -NSA_FORCE_SCORE))
        imp_scr[...] = imp

        def body(bp, cnt, imp=imp):
            row = imp_scr[pl.ds(bp, 1), :]
            beats = (row > imp) | ((row == imp) & (brow > bp))
            return cnt + jnp.where(beats, 1.0, 0.0)

        rank = lax.fori_loop(0, n_sel, body, jnp.zeros((n_sel, tt), F32))
        biases.append(jnp.where(rank < top_n, 0.0, SEL_BIAS))
    pad = jnp.zeros((2 * HEAD_DIM - n_sel, tt), F32)
    groups = [biases[1], pad, biases[0], pad]
    assert BIAS_SLOT == (2, 0)
    bias_ref[0] = jnp.concatenate(groups, axis=0).T.astype(BF16)


def _cmp_topk(q, kc, vc, n_cmp):
    b, s, _ = q.shape
    nk = kc.shape[1]
    n_sel = s // NSA_SEL_LEN
    tt = min(TT_CMP, s)
    kern = functools.partial(_cmp_topk_kernel, tt=tt, nk=nk, n_cmp=n_cmp, n_sel=n_sel, top_n=min(NSA_SEL_TOPN, n_sel))
    return pl.pallas_call(
        kern,
        grid=(b, s // tt),
        in_specs=[pl.BlockSpec((1, tt, WIDTH), lambda bi, ti: (bi, ti, 0)),
                  pl.BlockSpec((1, nk, WIDTH), lambda bi, ti: (bi, 0, 0)),
                  pl.BlockSpec((1, nk, WIDTH), lambda bi, ti: (bi, 0, 0))],
        out_specs=[pl.BlockSpec((1, tt, WIDTH), lambda bi, ti: (bi, ti, 0))] * 2,
        out_shape=[jax.ShapeDtypeStruct((b, s, WIDTH), F32), jax.ShapeDtypeStruct((b, s, WIDTH), BF16)],
        scratch_shapes=[pltpu.VMEM((nk + SUBLANES, tt), F32), pltpu.VMEM((n_sel, tt), F32)],
        compiler_params=pltpu.CompilerParams(dimension_semantics=("arbitrary", "arbitrary"),
                                             vmem_limit_bytes=VMEM_LIMIT),
        name="cmp_topk",
    )(q, kc, vc)


def _selwin_kernel(q_ref, bias_ref, ocmp_ref, gate_ref, zn_ref, ks_ref, vs_ref, kw_ref, vw_ref, e_ref, out_ref,
                   acc_scr, m_scr, *, tq, tk):
    t0 = pl.program_id(1) * tq
    jd = t0 // tk
    q = q_ref[0]
    bias = bias_ref[0]
    slot = _iota(q.shape, 1) // HEAD_DIM
    zero = jnp.zeros_like(q)
    q_sel = jnp.concatenate(
        [jnp.where(slot == h, q, jnp.where(slot == BIAS_SLOT[h // (N_HEADS // NSA_KV_HEADS)], bias, zero))
         for h in range(N_HEADS)], axis=0)
    q_win = _stack_heads(q)
    rows = N_HEADS * tq
    trow = t0 + (_iota((rows, tk), 0) & (tq - 1))
    kslot = _iota((tk, WIDTH), 1) // HEAD_DIM
    lanes = lambda x, n: jnp.concatenate([x] * (n // LANES), axis=1)

    def tile(i, k_ref, v_ref, selected, masking, first):
        k0 = pl.multiple_of((jd - i) * tk, tk)
        kt = k_ref[0, pl.ds(k0, tk), :]
        if selected:
            et = e_ref[pl.ds(k0, tk), :]
            s = jnp.concatenate([_dot_nt(q_sel[kh * (rows // 2):(kh + 1) * (rows // 2)],
                                         jnp.where(kslot == BIAS_SLOT[kh], et, kt))
                                 for kh in range(NSA_KV_HEADS)], axis=0)
        else:
            s = _dot_nt(q_win, kt)
        if masking == "causal":
            s = jnp.where((k0 + _iota(s.shape, 1)) <= trow, s, NEG)
        elif masking == "window":
            s = jnp.where((k0 + _iota(s.shape, 1)) > trow - NSA_WINDOW, s, NEG)
        s_max = jnp.max(s, axis=-1, keepdims=True)
        if first:
            m_new = jnp.broadcast_to(s_max, (rows, LANES))
            p = jnp.exp(s - lanes(m_new, tk))
            acc_scr[...] = _dot(p.astype(BF16), v_ref[0, pl.ds(k0, tk), :])
        else:
            m_old = m_scr[...]
            m_new = jnp.maximum(m_old, s_max)
            alpha = jnp.exp(m_old - m_new)
            p = jnp.exp(s - lanes(m_new, tk))
            acc_scr[...] = lanes(alpha, WIDTH) * acc_scr[...] + _dot(p.astype(BF16), v_ref[0, pl.ds(k0, tk), :])
        m_scr[...] = m_new

    def result():
        out = jnp.zeros((tq, WIDTH), F32)
        for h in range(N_HEADS):
            a = acc_scr[h * tq:(h + 1) * tq, :]
            r = pltpu.roll(a, HEAD_DIM, 1)
            out = jnp.where(slot == h, a * (1.0 / r) if h % 2 == 0 else r * (1.0 / a), out)
        return out

    tile(0, ks_ref, vs_ref, True, "causal", True)

    def sel_body(i, carry):
        tile(i, ks_ref, vs_ref, True, "none", False)
        return carry

    lax.fori_loop(1, jd + 1, sel_body, 0)
    o_slc = result()

    tile(0, kw_ref, vw_ref, False, "causal", True)

    @pl.when(jd >= 1)
    def _():
        tile(1, kw_ref, vw_ref, False, "window", False)

    o_win = result()
    gate = gate_ref[0]
    o = gate[:, :WIDTH] * ocmp_ref[0] + gate[:, WIDTH:2 * WIDTH] * o_slc + gate[:, 2 * WIDTH:] * o_win
    out_ref[0] = (o * zn_ref[0]).astype(BF16)


def _selwin(q, bias, ocmp, gate, zn, ks, vs, kw, vw):
    b, s, _ = q.shape
    tq, tk = min(TQ_ATT, s), min(TK_SEL, s)
    assert NSA_WINDOW == tk and tk % tq == 0 and s // NSA_SEL_LEN <= HEAD_DIM
    member = (jnp.arange(s)[:, None] // NSA_SEL_LEN == jnp.arange(WIDTH)[None, :] % HEAD_DIM).astype(BF16)
    tok = lambda w: pl.BlockSpec((1, tq, w), lambda bi, qi: (bi, qi, 0))
    seq = pl.BlockSpec((1, s, WIDTH), lambda bi, qi: (bi, 0, 0))
    return pl.pallas_call(
        functools.partial(_selwin_kernel, tq=tq, tk=tk),
        grid=(b, s // tq),
        in_specs=[tok(WIDTH), tok(WIDTH), tok(WIDTH), tok(3 * WIDTH), tok(WIDTH), seq, seq, seq, seq,
                  pl.BlockSpec(member.shape, lambda bi, qi: (0, 0))],
        out_specs=tok(WIDTH),
        out_shape=jax.ShapeDtypeStruct((b, s, WIDTH), BF16),
        scratch_shapes=[pltpu.VMEM((N_HEADS * tq, WIDTH), F32), pltpu.VMEM((N_HEADS * tq, LANES), F32)],
        compiler_params=pltpu.CompilerParams(dimension_semantics=("arbitrary", "arbitrary"),
                                             vmem_limit_bytes=VMEM_LIMIT),
        name="selwin",
    )(q, bias, ocmp, gate, zn, ks, vs, kw, vw, member)


def _stickbrk_kernel(q_ref, k_ref, v_ref, z_ref, out_ref, acc_scr, carry_scr, *, tq, tk):
    t0 = pl.program_id(1) * tq
    jd = t0 // tk
    qs = _stack_heads(q_ref[0])
    rows = N_HEADS * tq
    trow = t0 + (_iota((rows, tk), 0) & (tq - 1))
    tri = jnp.where(_iota((tk, tk), 0) >= _iota((tk, tk), 1), 1.0, 0.0).astype(BF16)
    acc_scr[...] = jnp.zeros((rows, WIDTH), F32)
    carry_scr[...] = jnp.zeros((rows, 1), F32)

    def cond(c):
        i, dead = c
        return (i <= jd) & (dead == 0)

    def body(c):
        i, _ = c
        k0 = pl.multiple_of((jd - i) * tk, tk)
        z = _dot_nt(qs, k_ref[0, pl.ds(k0, tk), :])
        mask = (k0 + _iota(z.shape, 1)) < trow
        softplus = jnp.maximum(z, 0.0) + jnp.log1p(jnp.exp(-jnp.abs(z)))
        log1mb = jnp.where(mask, -softplus, 0.0)
        hi = log1mb.astype(BF16)
        lo = (log1mb - hi.astype(F32)).astype(BF16)
        suffix = _dot(hi, tri) + _dot(lo, tri)
        carry = carry_scr[...]
        w = jnp.where(mask, jnp.exp(z + suffix + carry), 0.0)
        acc_scr[...] += _dot(w.astype(BF16), v_ref[0, pl.ds(k0, tk), :])
        carry = carry + jnp.sum(log1mb, axis=-1, keepdims=True)
        carry_scr[...] = carry
        dead = (jnp.max(carry) < SB_DEAD).astype(jnp.int32)
        return i + 1, dead

    lax.while_loop(cond, body, (jnp.int32(0), jnp.int32(0)))
    out_ref[0] = (_unstack_heads(acc_scr[...], tq) * z_ref[0]).astype(BF16)


def _stickbrk(q, k, v, z):
    b, s, _ = q.shape
    tq, tk = min(TQ_ATT, s), min(TK_ATT, s)
    tok = pl.BlockSpec((1, tq, WIDTH), lambda bi, qi: (bi, qi, 0))
    seq = pl.BlockSpec((1, s, WIDTH), lambda bi, qi: (bi, 0, 0))
    return pl.pallas_call(
        functools.partial(_stickbrk_kernel, tq=tq, tk=tk),
        grid=(b, s // tq),
        in_specs=[tok, seq, seq, tok],
        out_specs=tok,
        out_shape=jax.ShapeDtypeStruct((b, s, WIDTH), BF16),
        scratch_shapes=[pltpu.VMEM((N_HEADS * tq, WIDTH), F32), pltpu.VMEM((N_HEADS * tq, 1), F32)],
        compiler_params=pltpu.CompilerParams(dimension_semantics=("arbitrary", "arbitrary"),
                                             vmem_limit_bytes=VMEM_LIMIT),
        name="stickbrk",
    )(q, k, v, z)


def _s5_disc_kernel(are_ref, aim_ref, ldt_ref, bre_ref, bim_ref, abre_ref, abim_ref, bbre_ref, bbim_ref):
    dt = jnp.exp(ldt_ref[...])
    lr, li = are_ref[...], aim_ref[...]
    mag = jnp.exp(lr * dt)
    ab_re, ab_im = mag * jnp.cos(li * dt), mag * jnp.sin(li * dt)
    den = lr * lr + li * li
    coef_re = ((ab_re - 1.0) * lr + ab_im * li) / den
    coef_im = (ab_im * lr - (ab_re - 1.0) * li) / den
    abre_ref[...] = ab_re
    abim_ref[...] = ab_im
    br, bi = bre_ref[...], bim_ref[...]
    bbre_ref[...] = coef_re[:, None, :] * br - coef_im[:, None, :] * bi
    bbim_ref[...] = coef_re[:, None, :] * bi + coef_im[:, None, :] * br


def _s5_discretise(a_re, a_im, log_dt, b_re, b_im):
    g, p = a_re.shape
    brt, bit = jnp.swapaxes(b_re, 1, 2), jnp.swapaxes(b_im, 1, 2)
    return pl.pallas_call(
        _s5_disc_kernel,
        out_shape=[jax.ShapeDtypeStruct((g, p), F32)] * 2 + [jax.ShapeDtypeStruct(brt.shape, F32)] * 2,
        name="s5_disc",
    )(a_re, a_im, log_dt[:, None], brt, bit)


def _s5_kernel(u_ref, z_ref, bmat_ref, a_ref, cmat_ref, d_ref, gw_ref, gb_ref, out_ref, xs_scr, state_scr,
               *, nb, ts, ns):
    @pl.when(pl.program_id(0) == 0)
    def _():
        state_scr[...] = jnp.zeros((nb, 2 * ns), F32)

    nc = 2 * ns // LANES
    for b in range(nb):
        bu = _dot(u_ref[b].astype(BF16), bmat_ref[...])
        for c in range(nc):
            xs_scr[c, pl.ds(b, ts, stride=nb), :] = bu[:, c * LANES:(c + 1) * LANES]

    a_re = jnp.broadcast_to(a_ref[0:1, :], (nb, ns))
    a_im = jnp.broadcast_to(a_ref[1:2, :], (nb, ns))

    def step(t, state):
        x_re, x_im = state
        r0 = pl.multiple_of(t * nb, nb)
        bu = jnp.concatenate([xs_scr[c, pl.ds(r0, nb), :] for c in range(nc)], axis=1)
        n_re = a_re * x_re - a_im * x_im + bu[:, :ns]
        n_im = a_re * x_im + a_im * x_re + bu[:, ns:]
        for c in range(nc // 2):
            xs_scr[c, pl.ds(r0, nb), :] = n_re[:, c * LANES:(c + 1) * LANES]
            xs_scr[nc // 2 + c, pl.ds(r0, nb), :] = n_im[:, c * LANES:(c + 1) * LANES]
        return n_re, n_im

    st = state_scr[...]
    x_re, x_im = lax.fori_loop(0, ts, step, (st[:, :ns], st[:, ns:]))
    state_scr[...] = jnp.concatenate([x_re, x_im], axis=1)

    width = u_ref.shape[-1]
    for b in range(nb):
        xs = jnp.concatenate([xs_scr[c, pl.ds(b, ts, stride=nb), :] for c in range(nc)], axis=1)
        y = _dot(xs.astype(BF16), cmat_ref[...]) + d_ref[...] * u_ref[b]
        glu = _dot(y.astype(BF16), gw_ref[...]) + gb_ref[...]
        out_ref[b] = (glu[:, :width] * _sigmoid(glu[:, width:]) * z_ref[b]).astype(BF16)


def _s5(u, z, bmat, a_rows, cmat, d_row, glu_w, glu_b):
    nb, s, w = u.shape
    ts = min(TS_S5, s)
    ns = a_rows.shape[1]
    full = lambda a: pl.BlockSpec(a.shape, lambda i: (0, 0))
    tok = pl.BlockSpec((nb, ts, w), lambda i: (0, i, 0))
    return pl.pallas_call(
        functools.partial(_s5_kernel, nb=nb, ts=ts, ns=ns),
        grid=(s // ts,),
        in_specs=[tok, tok, full(bmat), full(a_rows), full(cmat), full(d_row), full(glu_w), full(glu_b)],
        out_specs=tok,
        out_shape=jax.ShapeDtypeStruct((nb, s, w), BF16),
        scratch_shapes=[pltpu.VMEM((2 * ns // LANES, ts * nb, LANES), F32), pltpu.VMEM((nb, 2 * ns), F32)],
        compiler_params=pltpu.CompilerParams(dimension_semantics=("arbitrary",), vmem_limit_bytes=VMEM_LIMIT),
        name="s5",
    )(u, z, bmat, a_rows, cmat, d_row, glu_w, glu_b)


def _merge_kernel(x_ref, ng_ref, o0_ref, o1_ref, o2_ref, o3_ref, wm_ref, wb_ref, wo_ref, out_ref):
    x = x_ref[...]
    ms = jnp.mean(x * x, axis=-1, keepdims=True)
    h = (x * lax.rsqrt(ms + NORM_EPS) * ng_ref[...]).astype(BF16)
    d = x.shape[1]
    mixed = None
    for m, o_ref in enumerate((o0_ref, o1_ref, o2_ref, o3_ref)):
        gate = _sigmoid(_dot(h, wm_ref[:, m * d:(m + 1) * d]))
        term = gate * _dot(o_ref[...], wb_ref[m])
        mixed = term if mixed is None else mixed + term
    out_ref[...] = x + _dot(mixed.astype(BF16), wo_ref[...])


def _merge(x2, ng, outs, wm, wb, wo):
    t, d = x2.shape
    tm = min(TM_PROJ, t)
    tok = lambda w: pl.BlockSpec((tm, w), lambda i: (i, 0))
    full = lambda a: pl.BlockSpec(a.shape, lambda i: (0,) * a.ndim)
    return pl.pallas_call(
        _merge_kernel,
        grid=(t // tm,),
        in_specs=[tok(d), full(ng)] + [tok(WIDTH)] * 4 + [full(wm), full(wb), full(wo)],
        out_specs=tok(d),
        out_shape=jax.ShapeDtypeStruct((t, d), F32),
        compiler_params=pltpu.CompilerParams(dimension_semantics=("arbitrary",), vmem_limit_bytes=VMEM_LIMIT),
        name="merge",
    )(x2, ng, *outs, wm, wb, wo)


def _rep_heads(w):
    h0, h1 = w[..., :HEAD_DIM], w[..., HEAD_DIM:]
    return jnp.concatenate([h0, h0, h1, h1], axis=-1)


def _spread_heads(w):
    h0, h1 = w[..., :HEAD_DIM], w[..., HEAD_DIM:]
    z = jnp.zeros_like(h0)
    return jnp.concatenate([h0, z, h1, z], axis=-1)


def _block_diag(blocks):
    g, r, c = blocks.shape
    eye = jnp.eye(g, dtype=blocks.dtype)
    return (eye[:, None, :, None] * blocks[:, :, None, :]).reshape(g * r, g * c)


def _layer(x, cos, sin, cos_c, sin_c, n_cmp, norm_g, w_in, qk_g, cmp_pe, cmp_w1, cmp_w2, conv_w,
           a_re, a_im, log_dt, b_re, b_im, c_re, c_im, d_skip, glu_w, glu_b, w_branch, w_out):
    b, s, d = x.shape
    w = WIDTH
    o_q, o_kv, o_gate = 0, w, 4 * w
    o_nz = o_gate + 3 * N_HEADS
    o_sc, o_scz, o_sb, o_sbz, o_s5, o_s5z = o_nz + w, o_nz + 4 * w, o_nz + 5 * w, o_nz + 8 * w, o_nz + 9 * w, o_nz + 10 * w
    o_merge = o_nz + 11 * w
    cols = lambda a, n: w_in[:, a:a + n]
    kv = lambda i: cols(o_kv + i * (w // 2), w // 2)
    wgate = jnp.repeat(cols(o_gate, 3 * N_HEADS), HEAD_DIM, axis=1)
    weights = [
        jnp.concatenate([cols(o_q, w), _rep_heads(kv(2)), _rep_heads(kv(4))], axis=1),
        jnp.concatenate([_spread_heads(kv(3)), _spread_heads(kv(5))], axis=1),
        jnp.concatenate([kv(0), kv(1)], axis=1),
        wgate,
        jnp.concatenate([cols(o_nz, w), cols(o_scz, w), cols(o_sbz, w), cols(o_s5z, w)], axis=1),
        cols(o_sc, 3 * w), cols(o_sb, 3 * w), cols(o_s5, w),
    ]
    weights = [a.astype(BF16) for a in weights]
    qkg = jnp.concatenate([jnp.tile(qk_g[0], N_HEADS), jnp.tile(qk_g[2], N_HEADS), jnp.tile(qk_g[3], N_HEADS)])[None, :]
    ng = norm_g[None, :]

    (q, ks, kw, vs, vw, kvc, gate, zn, zsb, zs5, sc_o, sbq, sbk, sbv, s5u) = _inproj(
        x, ng, weights, qkg, cos, sin, conv_w)

    half = NSA_CMP_STRIDE
    nk = s // half
    xh = kvc.reshape(b, nk, half, 4, HEAD_DIM).transpose(0, 3, 1, 2, 4).reshape(b, 4, nk, half * HEAD_DIM)
    pe = cmp_pe.reshape(2, NSA_CMP_LEN // half, half * HEAD_DIM)
    w1 = cmp_w1.reshape(2, NSA_CMP_LEN // half, half * HEAD_DIM, HEAD_DIM).astype(BF16)
    zeros = jnp.zeros_like(cmp_w2)
    w2 = jnp.stack([jnp.concatenate([cmp_w2, cmp_w2, zeros, zeros], axis=-1),
                    jnp.concatenate([zeros, zeros, cmp_w2, cmp_w2], axis=-1)], axis=1).astype(BF16)
    kc, vc = _compress(xh, pe, w1, w2, jnp.tile(qk_g[1], N_HEADS)[None, :], cos_c, sin_c)

    ocmp, bias = _cmp_topk(q, kc, vc, n_cmp)
    nsa_o = _selwin(q, bias, ocmp, gate, zn, ks, vs, kw, vw)
    sb_o = _stickbrk(sbq, sbk, sbv, zsb)

    ab_re, ab_im, bb_re, bb_im = _s5_discretise(a_re, a_im, log_dt, b_re, b_im)
    bmat = jnp.concatenate([_block_diag(bb_re), _block_diag(bb_im)], axis=1).astype(BF16)
    cmat = jnp.concatenate([_block_diag(jnp.swapaxes(c_re, 1, 2)),
                            -_block_diag(jnp.swapaxes(c_im, 1, 2))], axis=0).astype(BF16)
    a_rows = jnp.stack([ab_re.reshape(-1), ab_im.reshape(-1)])
    s5_o = _s5(s5u, zs5, bmat, a_rows, cmat, d_skip.reshape(1, -1), glu_w.astype(BF16), glu_b[None, :])

    out = _merge(x.reshape(b * s, d), ng,
                 [o.reshape(b * s, w) for o in (nsa_o, sc_o, sb_o, s5_o)],
                 cols(o_merge, N_HEADS * d).astype(BF16), w_branch.astype(BF16), w_out.astype(BF16))
    return out.reshape(b, s, d)


def kernel(x, positions, norm_g, w_in, nsa_qk_g, nsa_cmp_pe, nsa_cmp_w1, nsa_cmp_w2, sc_conv_w, s5_a_re, s5_a_im,
           s5_log_dt, s5_b_re, s5_b_im, s5_c_re, s5_c_im, s5_d, s5_glu_w, s5_glu_b, w_branch, w_out):
    b, s, _ = x.shape
    assert s % TK_ATT == 0 or s < TK_ATT
    cos, sin = _rope_tables(positions.reshape(-1))
    cos, sin = cos.reshape(b, s, LANES), sin.reshape(b, s, LANES)
    nk = s // NSA_CMP_STRIDE
    n_cmp = (s - NSA_CMP_LEN) // NSA_CMP_STRIDE + 1
    pos_c = jnp.concatenate([positions[:, NSA_CMP_LEN - 1::NSA_CMP_STRIDE],
                             jnp.zeros((b, nk - n_cmp), positions.dtype)], axis=1)
    cos_c, sin_c = _rope_tables(pos_c.reshape(-1))
    cos_c, sin_c = cos_c.reshape(b, nk, LANES), sin_c.reshape(b, nk, LANES)
    for l in range(norm_g.shape[0]):
        x = _layer(x, cos, sin, cos_c, sin_c, n_cmp, norm_g[l], w_in[l], nsa_qk_g[l], nsa_cmp_pe[l], nsa_cmp_w1[l],
                   nsa_cmp_w2[l], sc_conv_w[l], s5_a_re[l], s5_a_im[l], s5_log_dt[l], s5_b_re[l], s5_b_im[l],
                   s5_c_re[l], s5_c_im[l], s5_d[l], s5_glu_w[l], s5_glu_b[l], w_branch[l], w_out[l])
    return x
```

```python
import functools
import math

import jax
import jax.numpy as jnp
from jax import lax
from jax.experimental import pallas as pl
from jax.experimental.pallas import tpu as pltpu

F32 = jnp.float32
BF16 = jnp.bfloat16

HEAD_DIM = 64
N_HEADS = 4
WIDTH = N_HEADS * HEAD_DIM
NSA_KV_HEADS = 2
NSA_CMP_LEN = 32
NSA_CMP_STRIDE = 16
NSA_SEL_LEN = 64
NSA_SEL_TOPN = 16
NSA_WINDOW = 512
NSA_FORCE_SCORE = 1.0e4
S5_GROUPS = 16
S5_GROUP_CH = 16
S5_STATE = 64
ROPE_THETA = 10000.0
NORM_EPS = 1e-6
QK_SCALE = HEAD_DIM ** -0.5

LANES = 128
SUBLANES = 8
NEG = -1.0e30
SEL_BIAS = -30000.0
SB_DEAD = -120.0
VMEM_LIMIT = 56 * 1024 * 1024

TM_PROJ = 256
TT_CMP = 128
TQ_ATT = 128
TK_ATT = 256
TK_SEL = 512
BIAS_SLOT = (2, 0)
TS_S5 = 128


def _dot(a, b):
    return jnp.dot(a, b, preferred_element_type=F32)


def _dot_nt(a, b):
    return lax.dot_general(a, b, (((1,), (1,)), ((), ())), preferred_element_type=F32)


def _silu(x):
    return x * (1.0 / (1.0 + jnp.exp(-x)))


def _sigmoid(x):
    return 1.0 / (1.0 + jnp.exp(-x))


def _iota(shape, dim):
    return lax.broadcasted_iota(jnp.int32, shape, dim)


def _group_mean_sq(x):
    outs = []
    lane = _iota((x.shape[0], LANES), 1)
    low = lane < HEAD_DIM
    for c in range(x.shape[1] // LANES):
        xc = x[:, c * LANES:(c + 1) * LANES]
        sq = xc * xc
        s_lo = jnp.sum(jnp.where(low, sq, 0.0), axis=-1, keepdims=True)
        s_hi = jnp.sum(jnp.where(low, 0.0, sq), axis=-1, keepdims=True)
        outs.append(jnp.where(low, s_lo, s_hi) * (1.0 / HEAD_DIM))
    return outs[0] if len(outs) == 1 else jnp.concatenate(outs, axis=1)


def _rot_half(y):
    w = y.shape[1]
    first = (_iota(y.shape, 1) & (HEAD_DIM // 2)) == 0
    return jnp.where(first, pltpu.roll(y, w - HEAD_DIM // 2, 1), pltpu.roll(y, HEAD_DIM // 2, 1))


def _norm_rope(x, gain, cos, sin_signed):
    reps = x.shape[1] // LANES
    y = x * lax.rsqrt(_group_mean_sq(x) + NORM_EPS) * gain
    c = cos if reps == 1 else jnp.concatenate([cos] * reps, axis=1)
    s = sin_signed if reps == 1 else jnp.concatenate([sin_signed] * reps, axis=1)
    return y * c + _rot_half(y) * s


def _stack_heads(q):
    head = _iota(q.shape, 1) // HEAD_DIM
    return jnp.concatenate([jnp.where(head == h, q, jnp.zeros_like(q)) for h in range(N_HEADS)], axis=0)


def _unstack_heads(o4, m):
    head = _iota((m, WIDTH), 1) // HEAD_DIM
    out = jnp.zeros((m, WIDTH), F32)
    for h in range(N_HEADS):
        out = jnp.where(head == h, o4[h * m:(h + 1) * m], out)
    return out


def _rope_table_kernel(pos_ref, freq_ref, sign_ref, cos_ref, sin_ref):
    ang = pos_ref[...].astype(F32) * freq_ref[...]
    cos_ref[...] = jnp.cos(ang)
    sin_ref[...] = jnp.sin(ang) * sign_ref[...]


def _rope_tables(pos_flat):
    n = pos_flat.shape[0]
    half = HEAD_DIM // 2
    inv_freq = jnp.power(ROPE_THETA, -jnp.arange(half, dtype=F32) / half)
    freq = jnp.tile(inv_freq, LANES // half)[None, :]
    sign = jnp.tile(jnp.concatenate([-jnp.ones((half,), F32), jnp.ones((half,), F32)]), LANES // HEAD_DIM)[None, :]
    tm = 512 if n % 512 == 0 else n
    return pl.pallas_call(
        _rope_table_kernel,
        grid=(n // tm,),
        in_specs=[pl.BlockSpec((tm, 1), lambda i: (i, 0)),
                  pl.BlockSpec((1, LANES), lambda i: (0, 0)),
                  pl.BlockSpec((1, LANES), lambda i: (0, 0))],
        out_specs=[pl.BlockSpec((tm, LANES), lambda i: (i, 0))] * 2,
        out_shape=[jax.ShapeDtypeStruct((n, LANES), F32)] * 2,
        name="rope_tables",
    )(pos_flat[:, None], freq, sign)


def _inproj_kernel(x_ref, ng_ref, wqk_ref, wv_ref, wc_ref, wg_ref, wz_ref, wsc_ref, wsb_ref, ws5_ref,
                   qkg_ref, cos_ref, sin_ref, convw_ref,
                   q_ref, ks_ref, kw_ref, vs_ref, vw_ref, gate_ref, zn_ref, zsb_ref, zs5_ref,
                   sc_ref, sbq_ref, sbk_ref, sbv_ref, s5u_ref, kc_ref, vc_ref, ubuf_ref, *, tm):
    x = x_ref[0]
    ms = jnp.mean(x * x, axis=-1, keepdims=True)
    h = (x * lax.rsqrt(ms + NORM_EPS) * ng_ref[...]).astype(BF16)

    qk = _norm_rope(_dot(h, wqk_ref[...]), qkg_ref[...], cos_ref[0], sin_ref[0])
    q_ref[0] = (qk[:, :WIDTH] * QK_SCALE).astype(BF16)
    ks_ref[0] = qk[:, WIDTH:2 * WIDTH].astype(BF16)
    kw_ref[0] = qk[:, 2 * WIDTH:].astype(BF16)

    v = _dot(h, wv_ref[...])
    v = jnp.where((_iota(v.shape, 1) // HEAD_DIM) % 2 == 1, 1.0, v)
    vs_ref[0] = v[:, :WIDTH].astype(BF16)
    vw_ref[0] = v[:, WIDTH:].astype(BF16)
    kvc = _dot(h, wc_ref[...])
    kc_ref[0] = kvc[:, :LANES]
    vc_ref[0] = kvc[:, LANES:]
    gate_ref[0] = _sigmoid(_dot(h, wg_ref[...]))

    z = _silu(_dot(h, wz_ref[...]))
    zn_ref[0] = z[:, :WIDTH]
    zsb_ref[0] = z[:, 2 * WIDTH:3 * WIDTH]
    zs5_ref[0] = z[:, 3 * WIDTH:]

    bcx = _dot(h, wsc_ref[...])
    u = bcx[:, WIDTH:2 * WIDTH] * bcx[:, 2 * WIDTH:]

    @pl.when(pl.program_id(1) == 0)
    def _():
        ubuf_ref[0:SUBLANES, :] = jnp.zeros((SUBLANES, WIDTH), F32)

    ubuf_ref[SUBLANES:SUBLANES + tm, :] = u
    u1 = ubuf_ref[SUBLANES - 1:SUBLANES - 1 + tm, :]
    u2 = ubuf_ref[SUBLANES - 2:SUBLANES - 2 + tm, :]
    cw = convw_ref[...]
    y = cw[2:3, :] * u + cw[1:2, :] * u1 + cw[0:1, :] * u2
    ubuf_ref[0:SUBLANES, :] = ubuf_ref[tm:tm + SUBLANES, :]
    sc_ref[0] = (bcx[:, :WIDTH] * y * z[:, WIDTH:2 * WIDTH]).astype(BF16)

    sb = _dot(h, wsb_ref[...])
    sbq_ref[0] = (sb[:, :WIDTH] * QK_SCALE).astype(BF16)
    sbk_ref[0] = sb[:, WIDTH:2 * WIDTH].astype(BF16)
    sbv_ref[0] = sb[:, 2 * WIDTH:].astype(BF16)
    s5u_ref[0] = _dot(h, ws5_ref[...])


def _inproj(x, ng, weights, qkg, cos, sin, convw):
    b, s, d = x.shape
    tm = min(TM_PROJ, s)
    full2 = lambda a: pl.BlockSpec(a.shape, lambda bi, si: (0, 0))
    tok = lambda w: pl.BlockSpec((1, tm, w), lambda bi, si: (bi, si, 0))
    out_widths = [(WIDTH, BF16)] * 5 + [(3 * WIDTH, F32)] + [(WIDTH, F32)] * 3 \
        + [(WIDTH, BF16)] * 4 + [(WIDTH, F32)] + [(LANES, F32)] * 2
    return pl.pallas_call(
        functools.partial(_inproj_kernel, tm=tm),
        grid=(b, s // tm),
        in_specs=[tok(d), full2(ng)] + [full2(w) for w in weights] + [full2(qkg), tok(LANES), tok(LANES), full2(convw)],
        out_specs=[tok(w) for w, _ in out_widths],
        out_shape=[jax.ShapeDtypeStruct((b, s, w), dt) for w, dt in out_widths],
        scratch_shapes=[pltpu.VMEM((tm + 2 * SUBLANES, WIDTH), F32)],
        compiler_params=pltpu.CompilerParams(dimension_semantics=("arbitrary", "arbitrary"),
                                             vmem_limit_bytes=VMEM_LIMIT),
        name="inproj",
    )(x, ng, *weights, qkg, cos, sin, convw)


def _compress_kernel(kc_ref, vc_ref, pe_ref, w1_ref, w2k_ref, w2v_ref, g_ref, cos_ref, sin_ref, kco_ref, vct_ref, *, nk):
    half = NSA_CMP_STRIDE
    for j, src in ((0, kc_ref), (1, vc_ref)):
        tok = [src[0, pl.ds(i, nk, stride=half), :] for i in range(half)]
        lo = jnp.concatenate([(tok[i] + pe_ref[j, 0, i:i + 1, :]).astype(BF16) for i in range(half)], axis=1)
        hi = jnp.concatenate([(tok[i] + pe_ref[j, 1, i:i + 1, :]).astype(BF16) for i in range(half)], axis=1)
        hid = _dot(lo, w1_ref[j, 0]) + pltpu.roll(_dot(hi, w1_ref[j, 1]), nk - 1, 0)
        act = _silu(hid).astype(BF16)
        if j == 0:
            kco_ref[0] = _norm_rope(_dot(act, w2k_ref[...]), g_ref[...], cos_ref[0], sin_ref[0]).astype(BF16)
        else:
            vct_ref[0] = _dot_nt(w2v_ref[...], act).astype(BF16)


def _compress(kc, vc, pe, w1, w2k, w2v, g, cos, sin):
    b, s, _ = kc.shape
    nk = s // NSA_CMP_STRIDE
    full = lambda a: pl.BlockSpec(a.shape, lambda bi: (0,) * a.ndim)
    seq = pl.BlockSpec((1, s, LANES), lambda bi: (bi, 0, 0))
    tab = pl.BlockSpec((1, nk, LANES), lambda bi: (bi, 0, 0))
    return pl.pallas_call(
        functools.partial(_compress_kernel, nk=nk),
        grid=(b,),
        in_specs=[seq, seq, full(pe), full(w1), full(w2k), full(w2v), full(g), tab, tab],
        out_specs=[pl.BlockSpec((1, nk, WIDTH), lambda bi: (bi, 0, 0)),
                   pl.BlockSpec((1, LANES, nk), lambda bi: (bi, 0, 0))],
        out_shape=[jax.ShapeDtypeStruct((b, nk, WIDTH), BF16), jax.ShapeDtypeStruct((b, LANES, nk), BF16)],
        compiler_params=pltpu.CompilerParams(dimension_semantics=("arbitrary",), vmem_limit_bytes=VMEM_LIMIT),
        name="compress",
    )(kc, vc, pe, w1, w2k, w2v, g, cos, sin)


def _cmp_topk_kernel(q_ref, kc_ref, vct_ref, ocmp_ref, bias_ref, p_scr, imp_scr, rank_scr,
                     *, tt, nk, n_cmp, n_sel, top_n):
    t0 = pl.program_id(1) * tt
    q = q_ref[0]
    kc = kc_ref[0]
    head = _iota(q.shape, 1) // HEAD_DIM
    nrow = _iota((nk, tt), 0)
    tcol = t0 + _iota((nk, tt), 1)
    valid_t = (nrow * NSA_CMP_STRIDE + (NSA_CMP_LEN - 1) <= tcol) & (nrow < n_cmp)
    brow = _iota((n_sel, tt), 0)
    tsel = t0 + _iota((n_sel, tt), 1)
    cur = tsel // NSA_SEL_LEN
    forced = (brow == 0) | (brow == cur) | (brow == cur - 1)
    in_past = brow * NSA_SEL_LEN <= tsel
    last_block = (t0 + tt - 1) // NSA_SEL_LEN
    sub = _iota((SUBLANES, tt), 0)
    heads_per_kv = N_HEADS // NSA_KV_HEADS
    scores = [_dot_nt(kc, jnp.where(head == h, q, jnp.zeros_like(q))) for h in range(N_HEADS)]
    probs = []
    for st in scores:
        st = jnp.where(valid_t, st, NEG)
        et = jnp.where(valid_t, jnp.exp(st - jnp.max(st, axis=0, keepdims=True)), 0.0)
        probs.append(et * (1.0 / jnp.maximum(jnp.sum(et, axis=0, keepdims=True), 1e-30)))
    outs = [_dot(vct_ref[0, (h // heads_per_kv) * HEAD_DIM:(h // heads_per_kv + 1) * HEAD_DIM, :],
                 probs[h].astype(BF16)) for h in range(N_HEADS)]
    groups = n_sel // SUBLANES
    imp_g = []
    for kh in range(NSA_KV_HEADS):
        psum = probs[kh * heads_per_kv]
        for g in range(1, heads_per_kv):
            psum = psum + probs[kh * heads_per_kv + g]
        p_scr[kh, SUBLANES:SUBLANES + nk, :] = psum
        p_scr[kh, 0:SUBLANES, :] = jnp.zeros((SUBLANES, tt), F32)
        ratio = NSA_SEL_LEN // NSA_CMP_STRIDE
        imp = p_scr[kh, pl.ds(SUBLANES - 1, n_sel, stride=ratio), :]
        for k in range(1, NSA_CMP_LEN // NSA_CMP_STRIDE + ratio - 1):
            imp = imp + p_scr[kh, pl.ds(SUBLANES - 1 + k, n_sel, stride=ratio), :]
        imp = jnp.where(forced, NSA_FORCE_SCORE, jnp.where(in_past, imp, -NSA_FORCE_SCORE))
        imp_scr[kh] = imp
        imp_g.append([imp[k * SUBLANES:(k + 1) * SUBLANES] for k in range(groups)])

    rank_scr[...] = jnp.zeros((NSA_KV_HEADS, n_sel, tt), F32)
    for c in range(groups):
        @pl.when(c * SUBLANES <= last_block)
        def _(c=c):
            for kh in range(NSA_KV_HEADS):
                for k in range(groups):
                    mine = imp_g[kh][k]
                    cnt = jnp.zeros((SUBLANES, tt), F32)
                    for r in range(SUBLANES):
                        row = imp_scr[kh, c * SUBLANES + r:c * SUBLANES + r + 1, :]
                        if k > c:
                            beats = row >= mine
                        elif k < c:
                            beats = row > mine
                        else:
                            beats = (row > mine) | ((row == mine) & (sub > r))
                        cnt = cnt + jnp.where(beats, 1.0, 0.0)
                    rank_scr[kh, k * SUBLANES:(k + 1) * SUBLANES, :] += cnt

    biases = [jnp.where(rank_scr[kh] < top_n, 0.0, SEL_BIAS) for kh in range(NSA_KV_HEADS)]
    ocmp_ref[0] = jnp.concatenate(outs, axis=0).T
    pad = jnp.zeros((2 * HEAD_DIM - n_sel, tt), F32)
    assert BIAS_SLOT == (2, 0)
    bias_ref[0] = jnp.concatenate([biases[1], pad, biases[0], pad], axis=0).T.astype(BF16)


def _cmp_topk(q, kc, vct, n_cmp):
    b, s, _ = q.shape
    nk = kc.shape[1]
    n_sel = s // NSA_SEL_LEN
    tt = min(TT_CMP, s)
    kern = functools.partial(_cmp_topk_kernel, tt=tt, nk=nk, n_cmp=n_cmp, n_sel=n_sel, top_n=min(NSA_SEL_TOPN, n_sel))
    return pl.pallas_call(
        kern,
        grid=(b, s // tt),
        in_specs=[pl.BlockSpec((1, tt, WIDTH), lambda bi, ti: (bi, ti, 0)),
                  pl.BlockSpec((1, nk, WIDTH), lambda bi, ti: (bi, 0, 0)),
                  pl.BlockSpec((1, LANES, nk), lambda bi, ti: (bi, 0, 0))],
        out_specs=[pl.BlockSpec((1, tt, WIDTH), lambda bi, ti: (bi, ti, 0))] * 2,
        out_shape=[jax.ShapeDtypeStruct((b, s, WIDTH), F32), jax.ShapeDtypeStruct((b, s, WIDTH), BF16)],
        scratch_shapes=[pltpu.VMEM((NSA_KV_HEADS, nk + SUBLANES, tt), F32), pltpu.VMEM((NSA_KV_HEADS, n_sel, tt), F32),
                        pltpu.VMEM((NSA_KV_HEADS, n_sel, tt), F32)],
        compiler_params=pltpu.CompilerParams(dimension_semantics=("arbitrary", "arbitrary"),
                                             vmem_limit_bytes=VMEM_LIMIT),
        name="cmp_topk",
    )(q, kc, vct)


def _selwin_kernel(q_ref, bias_ref, ocmp_ref, gate_ref, zn_ref, ks_ref, vs_ref, kw_ref, vw_ref, e_ref, out_ref,
                   acc_scr, m_scr, *, tq, tk):
    t0 = pl.program_id(1) * tq
    jd = t0 // tk
    q = q_ref[0]
    bias = bias_ref[0]
    slot = _iota(q.shape, 1) // HEAD_DIM
    zero = jnp.zeros_like(q)
    q_sel = jnp.concatenate(
        [jnp.where(slot == h, q, jnp.where(slot == BIAS_SLOT[h // (N_HEADS // NSA_KV_HEADS)], bias, zero))
         for h in range(N_HEADS)], axis=0)
    q_win = _stack_heads(q)
    rows = N_HEADS * tq
    trow = t0 + (_iota((rows, tk), 0) & (tq - 1))
    kslot = _iota((tk, WIDTH), 1) // HEAD_DIM
    lanes = lambda x, n: jnp.concatenate([x] * (n // LANES), axis=1)

    def tile(i, k_ref, v_ref, selected, masking, first):
        k0 = pl.multiple_of((jd - i) * tk, tk)
        kt = k_ref[0, pl.ds(k0, tk), :]
        if selected:
            et = e_ref[pl.ds(k0, tk), :]
            s = jnp.concatenate([_dot_nt(q_sel[kh * (rows // 2):(kh + 1) * (rows // 2)],
                                         jnp.where(kslot == BIAS_SLOT[kh], et, kt))
                                 for kh in range(NSA_KV_HEADS)], axis=0)
        else:
            s = _dot_nt(q_win, kt)
        if masking == "causal":
            s = jnp.where((k0 + _iota(s.shape, 1)) <= trow, s, NEG)
        elif masking == "window":
            s = jnp.where((k0 + _iota(s.shape, 1)) > trow - NSA_WINDOW, s, NEG)
        s_max = jnp.max(s, axis=-1, keepdims=True)
        if first:
            m_new = jnp.broadcast_to(s_max, (rows, LANES))
            p = jnp.exp(s - lanes(m_new, tk))
            acc_scr[...] = _dot(p.astype(BF16), v_ref[0, pl.ds(k0, tk), :])
        else:
            m_old = m_scr[...]
            m_new = jnp.maximum(m_old, s_max)
            alpha = jnp.exp(m_old - m_new)
            p = jnp.exp(s - lanes(m_new, tk))
            acc_scr[...] = lanes(alpha, WIDTH) * acc_scr[...] + _dot(p.astype(BF16), v_ref[0, pl.ds(k0, tk), :])
        m_scr[...] = m_new

    def result():
        out = jnp.zeros((tq, WIDTH), F32)
        for h in range(N_HEADS):
            a = acc_scr[h * tq:(h + 1) * tq, :]
            r = pltpu.roll(a, HEAD_DIM, 1)
            out = jnp.where(slot == h, a * (1.0 / r) if h % 2 == 0 else r * (1.0 / a), out)
        return out

    tile(0, ks_ref, vs_ref, True, "causal", True)

    def sel_body(i, carry):
        tile(i, ks_ref, vs_ref, True, "none", False)
        return carry

    lax.fori_loop(1, jd + 1, sel_body, 0)
    o_slc = result()

    tile(0, kw_ref, vw_ref, False, "causal", True)

    @pl.when(jd >= 1)
    def _():
        tile(1, kw_ref, vw_ref, False, "window", False)

    o_win = result()
    gate = gate_ref[0]
    o = gate[:, :WIDTH] * ocmp_ref[0] + gate[:, WIDTH:2 * WIDTH] * o_slc + gate[:, 2 * WIDTH:] * o_win
    out_ref[0] = (o * zn_ref[0]).astype(BF16)


def _selwin(q, bias, ocmp, gate, zn, ks, vs, kw, vw):
    b, s, _ = q.shape
    tq, tk = min(TQ_ATT, s), min(TK_SEL, s)
    assert NSA_WINDOW == tk and tk % tq == 0 and s // NSA_SEL_LEN <= HEAD_DIM
    member = (jnp.arange(s)[:, None] // NSA_SEL_LEN == jnp.arange(WIDTH)[None, :] % HEAD_DIM).astype(BF16)
    tok = lambda w: pl.BlockSpec((1, tq, w), lambda bi, qi: (bi, qi, 0))
    seq = pl.BlockSpec((1, s, WIDTH), lambda bi, qi: (bi, 0, 0))
    return pl.pallas_call(
        functools.partial(_selwin_kernel, tq=tq, tk=tk),
        grid=(b, s // tq),
        in_specs=[tok(WIDTH), tok(WIDTH), tok(WIDTH), tok(3 * WIDTH), tok(WIDTH), seq, seq, seq, seq,
                  pl.BlockSpec(member.shape, lambda bi, qi: (0, 0))],
        out_specs=tok(WIDTH),
        out_shape=jax.ShapeDtypeStruct((b, s, WIDTH), BF16),
        scratch_shapes=[pltpu.VMEM((N_HEADS * tq, WIDTH), F32), pltpu.VMEM((N_HEADS * tq, LANES), F32)],
        compiler_params=pltpu.CompilerParams(dimension_semantics=("arbitrary", "arbitrary"),
                                             vmem_limit_bytes=VMEM_LIMIT),
        name="selwin",
    )(q, bias, ocmp, gate, zn, ks, vs, kw, vw, member)


def _stickbrk_kernel(q_ref, k_ref, v_ref, z_ref, out_ref, acc_scr, carry_scr, *, tq, tk):
    t0 = pl.program_id(1) * tq
    jd = t0 // tk
    qs = _stack_heads(q_ref[0])
    rows = N_HEADS * tq
    trow = t0 + (_iota((rows, tk), 0) & (tq - 1))
    tri = jnp.where(_iota((tk, tk), 0) >= _iota((tk, tk), 1), 1.0, 0.0).astype(BF16)
    lanes = lambda x, n: jnp.concatenate([x] * (n // LANES), axis=1)

    def tile(i, diagonal):
        k0 = pl.multiple_of((jd - i) * tk, tk)
        z = _dot_nt(qs, k_ref[0, pl.ds(k0, tk), :])
        log1mb = -(jnp.maximum(z, 0.0) + jnp.log(1.0 + jnp.exp(-jnp.abs(z))))
        if diagonal:
            mask = (k0 + _iota(z.shape, 1)) < trow
            log1mb = jnp.where(mask, log1mb, 0.0)
        hi = log1mb.astype(BF16)
        lo = (log1mb - hi.astype(F32)).astype(BF16)
        suffix = _dot(hi, tri) + _dot(lo, tri)
        tile_sum = jnp.broadcast_to(jnp.sum(log1mb, axis=-1, keepdims=True), (rows, LANES))
        if diagonal:
            w = jnp.where(mask, jnp.exp(z + suffix), 0.0)
            acc_scr[...] = _dot(w.astype(BF16), v_ref[0, pl.ds(k0, tk), :])
            carry = tile_sum
        else:
            carry = carry_scr[...]
            w = jnp.exp(z + suffix + lanes(carry, tk))
            acc_scr[...] += _dot(w.astype(BF16), v_ref[0, pl.ds(k0, tk), :])
            carry = carry + tile_sum
        carry_scr[...] = carry
        return (jnp.max(carry) < SB_DEAD).astype(jnp.int32)

    def cond(c):
        i, dead = c
        return (i <= jd) & (dead == 0)

    def body(c):
        return c[0] + 1, tile(c[0], False)

    lax.while_loop(cond, body, (jnp.int32(1), tile(0, True)))
    out_ref[0] = (_unstack_heads(acc_scr[...], tq) * z_ref[0]).astype(BF16)


def _stickbrk(q, k, v, z):
    b, s, _ = q.shape
    tq, tk = min(TQ_ATT, s), min(TK_ATT, s)
    tok = pl.BlockSpec((1, tq, WIDTH), lambda bi, qi: (bi, qi, 0))
    seq = pl.BlockSpec((1, s, WIDTH), lambda bi, qi: (bi, 0, 0))
    return pl.pallas_call(
        functools.partial(_stickbrk_kernel, tq=tq, tk=tk),
        grid=(b, s // tq),
        in_specs=[tok, seq, seq, tok],
        out_specs=tok,
        out_shape=jax.ShapeDtypeStruct((b, s, WIDTH), BF16),
        scratch_shapes=[pltpu.VMEM((N_HEADS * tq, WIDTH), F32), pltpu.VMEM((N_HEADS * tq, LANES), F32)],
        compiler_params=pltpu.CompilerParams(dimension_semantics=("arbitrary", "arbitrary"),
                                             vmem_limit_bytes=VMEM_LIMIT),
        name="stickbrk",
    )(q, k, v, z)


def _s5_disc_kernel(are_ref, aim_ref, ldt_ref, bre_ref, bim_ref, abre_ref, abim_ref, bbre_ref, bbim_ref):
    dt = jnp.exp(ldt_ref[...])
    lr, li = are_ref[...], aim_ref[...]
    mag = jnp.exp(lr * dt)
    ab_re, ab_im = mag * jnp.cos(li * dt), mag * jnp.sin(li * dt)
    den = lr * lr + li * li
    coef_re = ((ab_re - 1.0) * lr + ab_im * li) / den
    coef_im = (ab_im * lr - (ab_re - 1.0) * li) / den
    abre_ref[...] = ab_re
    abim_ref[...] = ab_im
    br, bi = bre_ref[...], bim_ref[...]
    bbre_ref[...] = coef_re[:, None, :] * br - coef_im[:, None, :] * bi
    bbim_ref[...] = coef_re[:, None, :] * bi + coef_im[:, None, :] * br


def _s5_discretise(a_re, a_im, log_dt, b_re, b_im):
    g, p = a_re.shape
    brt, bit = jnp.swapaxes(b_re, 1, 2), jnp.swapaxes(b_im, 1, 2)
    return pl.pallas_call(
        _s5_disc_kernel,
        out_shape=[jax.ShapeDtypeStruct((g, p), F32)] * 2 + [jax.ShapeDtypeStruct(brt.shape, F32)] * 2,
        name="s5_disc",
    )(a_re, a_im, log_dt[:, None], brt, bit)


def _s5_kernel(u_ref, z_ref, bmat_ref, a_ref, cmat_ref, d_ref, gw_ref, gb_ref, out_ref, xs_scr, state_scr,
               *, nb, ts, ns):
    @pl.when(pl.program_id(0) == 0)
    def _():
        state_scr[...] = jnp.zeros((nb, 2 * ns), F32)

    nc = 2 * ns // LANES
    for b in range(nb):
        bu = _dot(u_ref[b].astype(BF16), bmat_ref[...])
        for c in range(nc):
            xs_scr[c, pl.ds(b, ts, stride=nb), :] = bu[:, c * LANES:(c + 1) * LANES]

    a_re = jnp.broadcast_to(a_ref[0:1, :], (nb, ns))
    a_im = jnp.broadcast_to(a_ref[1:2, :], (nb, ns))

    def step(t, state):
        x_re, x_im = state
        r0 = pl.multiple_of(t * nb, nb)
        bu = jnp.concatenate([xs_scr[c, pl.ds(r0, nb), :] for c in range(nc)], axis=1)
        n_re = a_re * x_re - a_im * x_im + bu[:, :ns]
        n_im = a_re * x_im + a_im * x_re + bu[:, ns:]
        for c in range(nc // 2):
            xs_scr[c, pl.ds(r0, nb), :] = n_re[:, c * LANES:(c + 1) * LANES]
            xs_scr[nc // 2 + c, pl.ds(r0, nb), :] = n_im[:, c * LANES:(c + 1) * LANES]
        return n_re, n_im

    st = state_scr[...]
    x_re, x_im = lax.fori_loop(0, ts, step, (st[:, :ns], st[:, ns:]))
    state_scr[...] = jnp.concatenate([x_re, x_im], axis=1)

    width = u_ref.shape[-1]
    for b in range(nb):
        xs = jnp.concatenate([xs_scr[c, pl.ds(b, ts, stride=nb), :] for c in range(nc)], axis=1)
        y = _dot(xs.astype(BF16), cmat_ref[...]) + d_ref[...] * u_ref[b]
        glu = _dot(y.astype(BF16), gw_ref[...]) + gb_ref[...]
        out_ref[b] = (glu[:, :width] * _sigmoid(glu[:, width:]) * z_ref[b]).astype(BF16)


def _s5(u, z, bmat, a_rows, cmat, d_row, glu_w, glu_b):
    nb, s, w = u.shape
    ts = min(TS_S5, s)
    ns = a_rows.shape[1]
    full = lambda a: pl.BlockSpec(a.shape, lambda i: (0, 0))
    tok = pl.BlockSpec((nb, ts, w), lambda i: (0, i, 0))
    return pl.pallas_call(
        functools.partial(_s5_kernel, nb=nb, ts=ts, ns=ns),
        grid=(s // ts,),
        in_specs=[tok, tok, full(bmat), full(a_rows), full(cmat), full(d_row), full(glu_w), full(glu_b)],
        out_specs=tok,
        out_shape=jax.ShapeDtypeStruct((nb, s, w), BF16),
        scratch_shapes=[pltpu.VMEM((2 * ns // LANES, ts * nb, LANES), F32), pltpu.VMEM((nb, 2 * ns), F32)],
        compiler_params=pltpu.CompilerParams(dimension_semantics=("arbitrary",), vmem_limit_bytes=VMEM_LIMIT),
        name="s5",
    )(u, z, bmat, a_rows, cmat, d_row, glu_w, glu_b)


def _merge_kernel(x_ref, ng_ref, o0_ref, o1_ref, o2_ref, o3_ref, wm_ref, wb_ref, wo_ref, out_ref):
    x = x_ref[...]
    ms = jnp.mean(x * x, axis=-1, keepdims=True)
    h = (x * lax.rsqrt(ms + NORM_EPS) * ng_ref[...]).astype(BF16)
    d = x.shape[1]
    mixed = None
    for m, o_ref in enumerate((o0_ref, o1_ref, o2_ref, o3_ref)):
        gate = _sigmoid(_dot(h, wm_ref[:, m * d:(m + 1) * d]))
        term = gate * _dot(o_ref[...], wb_ref[m])
        mixed = term if mixed is None else mixed + term
    out_ref[...] = x + _dot(mixed.astype(BF16), wo_ref[...])


def _merge(x2, ng, outs, wm, wb, wo):
    t, d = x2.shape
    tm = min(TM_PROJ, t)
    tok = lambda w: pl.BlockSpec((tm, w), lambda i: (i, 0))
    full = lambda a: pl.BlockSpec(a.shape, lambda i: (0,) * a.ndim)
    return pl.pallas_call(
        _merge_kernel,
        grid=(t // tm,),
        in_specs=[tok(d), full(ng)] + [tok(WIDTH)] * 4 + [full(wm), full(wb), full(wo)],
        out_specs=tok(d),
        out_shape=jax.ShapeDtypeStruct((t, d), F32),
        compiler_params=pltpu.CompilerParams(dimension_semantics=("arbitrary",), vmem_limit_bytes=VMEM_LIMIT),
        name="merge",
    )(x2, ng, *outs, wm, wb, wo)


def _wprep_kernel(w_ref, wqk_ref, wv_ref, wc_ref, wg_ref, wz_ref, wsc_ref, wsb_ref, ws5_ref, wm_ref, *, d):
    w = WIDTH
    o_q, o_kv, o_gate = 0, w, 4 * w
    o_nz = o_gate + 3 * N_HEADS
    o_sc, o_scz, o_sb, o_sbz, o_s5, o_s5z = o_nz + w, o_nz + 4 * w, o_nz + 5 * w, o_nz + 8 * w, o_nz + 9 * w, o_nz + 10 * w
    o_merge = o_nz + 11 * w
    col = lambda a, n: w_ref[0, :, a:a + n]
    kv = lambda i: o_kv + i * (w // 2)
    cat = lambda parts: jnp.concatenate(parts, axis=1).astype(BF16)

    def rep_heads(a):
        h0, h1 = col(a, HEAD_DIM), col(a + HEAD_DIM, HEAD_DIM)
        return [h0, h0, h1, h1]

    def spread_heads(a):
        h0, h1 = col(a, HEAD_DIM), col(a + HEAD_DIM, HEAD_DIM)
        return [h0, jnp.zeros_like(h0), h1, jnp.zeros_like(h0)]

    wqk_ref[...] = cat([col(o_q, w)] + rep_heads(kv(2)) + rep_heads(kv(4)))
    wv_ref[...] = cat(spread_heads(kv(3)) + spread_heads(kv(5)))
    wc_ref[...] = col(kv(0), w).astype(BF16)
    gates = col(o_gate, 3 * N_HEADS)
    wg_ref[...] = cat([jnp.broadcast_to(gates[:, c:c + 1], (gates.shape[0], HEAD_DIM)) for c in range(3 * N_HEADS)])
    wz_ref[...] = cat([col(o_nz, w), col(o_scz, w), col(o_sbz, w), col(o_s5z, w)])
    wsc_ref[...] = col(o_sc, 3 * w).astype(BF16)
    wsb_ref[...] = col(o_sb, 3 * w).astype(BF16)
    ws5_ref[...] = col(o_s5, w).astype(BF16)
    wm_ref[...] = col(o_merge, N_HEADS * d).astype(BF16)


def _wprep(w_in, layer):
    _, d, n_in = w_in.shape
    tr = min(TM_PROJ, d)
    widths = [3 * WIDTH, 2 * WIDTH, WIDTH, 3 * WIDTH, 4 * WIDTH, 3 * WIDTH, 3 * WIDTH, WIDTH, N_HEADS * d]
    return pl.pallas_call(
        functools.partial(_wprep_kernel, d=d),
        grid=(d // tr,),
        in_specs=[pl.BlockSpec((1, tr, n_in), lambda i: (layer, i, 0))],
        out_specs=[pl.BlockSpec((tr, n), lambda i: (i, 0)) for n in widths],
        out_shape=[jax.ShapeDtypeStruct((d, n), BF16) for n in widths],
        compiler_params=pltpu.CompilerParams(dimension_semantics=("arbitrary",), vmem_limit_bytes=VMEM_LIMIT),
        name="wprep",
    )(w_in)


def _block_diag(blocks):
    g, r, c = blocks.shape
    eye = jnp.eye(g, dtype=blocks.dtype)
    return (eye[:, None, :, None] * blocks[:, :, None, :]).reshape(g * r, g * c)


def _block_diag_pair(w1):
    eye = jnp.eye(NSA_KV_HEADS, dtype=w1.dtype)
    bd = w1[..., :, None, :, None, :] * eye[:, None, :, None]
    return bd.reshape(w1.shape[:-3] + (w1.shape[-3] * NSA_KV_HEADS * HEAD_DIM, NSA_KV_HEADS * HEAD_DIM))


def _layer(x, cos, sin, cos_c, sin_c, n_cmp, norm_g, w_in_all, layer, qk_g, cmp_pe, cmp_w1, cmp_w2, conv_w,
           a_re, a_im, log_dt, b_re, b_im, c_re, c_im, d_skip, glu_w, glu_b, w_branch, w_out):
    b, s, d = x.shape
    w = WIDTH
    *weights, w_merge = _wprep(w_in_all, layer)
    qkg = jnp.concatenate([jnp.tile(qk_g[0], N_HEADS), jnp.tile(qk_g[2], N_HEADS), jnp.tile(qk_g[3], N_HEADS)])[None, :]
    ng = norm_g[None, :]

    (q, ks, kw, vs, vw, gate, zn, zsb, zs5, sc_o, sbq, sbk, sbv, s5u, kc_in, vc_in) = _inproj(
        x, ng, weights, qkg, cos, sin, conv_w)

    half = NSA_CMP_STRIDE
    two = NSA_CMP_LEN // half
    pe = jnp.tile(cmp_pe.reshape(2, two, half, HEAD_DIM), (1, 1, 1, NSA_KV_HEADS))
    w1 = _block_diag_pair(cmp_w1.reshape(2, two, half, HEAD_DIM, HEAD_DIM))
    zeros = jnp.zeros_like(cmp_w2[0])
    w2k = jnp.concatenate([jnp.concatenate([cmp_w2[0], cmp_w2[0], zeros, zeros], axis=1),
                           jnp.concatenate([zeros, zeros, cmp_w2[0], cmp_w2[0]], axis=1)], axis=0)
    w2v = jnp.concatenate([jnp.concatenate([cmp_w2[1], zeros], axis=1),
                           jnp.concatenate([zeros, cmp_w2[1]], axis=1)], axis=0)
    kc, vct = _compress(kc_in, vc_in, pe, w1.astype(BF16), w2k.astype(BF16), w2v.T.astype(BF16),
                        jnp.tile(qk_g[1], N_HEADS)[None, :], cos_c, sin_c)

    ocmp, bias = _cmp_topk(q, kc, vct, n_cmp)
    nsa_o = _selwin(q, bias, ocmp, gate, zn, ks, vs, kw, vw)
    sb_o = _stickbrk(sbq, sbk, sbv, zsb)

    ab_re, ab_im, bb_re, bb_im = _s5_discretise(a_re, a_im, log_dt, b_re, b_im)
    bmat = jnp.concatenate([_block_diag(bb_re), _block_diag(bb_im)], axis=1).astype(BF16)
    cmat = jnp.concatenate([_block_diag(jnp.swapaxes(c_re, 1, 2)),
                            -_block_diag(jnp.swapaxes(c_im, 1, 2))], axis=0).astype(BF16)
    a_rows = jnp.stack([ab_re.reshape(-1), ab_im.reshape(-1)])
    s5_o = _s5(s5u, zs5, bmat, a_rows, cmat, d_skip.reshape(1, -1), glu_w.astype(BF16), glu_b[None, :])

    out = _merge(x.reshape(b * s, d), ng,
                 [o.reshape(b * s, w) for o in (nsa_o, sc_o, sb_o, s5_o)],
                 w_merge, w_branch.astype(BF16), w_out.astype(BF16))
    return out.reshape(b, s, d)


def kernel(x, positions, norm_g, w_in, nsa_qk_g, nsa_cmp_pe, nsa_cmp_w1, nsa_cmp_w2, sc_conv_w, s5_a_re, s5_a_im,
           s5_log_dt, s5_b_re, s5_b_im, s5_c_re, s5_c_im, s5_d, s5_glu_w, s5_glu_b, w_branch, w_out):
    b, s, _ = x.shape
    assert s % TK_ATT == 0 or s < TK_ATT
    cos, sin = _rope_tables(positions.reshape(-1))
    cos, sin = cos.reshape(b, s, LANES), sin.reshape(b, s, LANES)
    nk = s // NSA_CMP_STRIDE
    n_cmp = (s - NSA_CMP_LEN) // NSA_CMP_STRIDE + 1
    pos_c = jnp.concatenate([positions[:, NSA_CMP_LEN - 1::NSA_CMP_STRIDE],
                             jnp.zeros((b, nk - n_cmp), positions.dtype)], axis=1)
    cos_c, sin_c = _rope_tables(pos_c.reshape(-1))
    cos_c, sin_c = cos_c.reshape(b, nk, LANES), sin_c.reshape(b, nk, LANES)
    for l in range(norm_g.shape[0]):
        x = _layer(x, cos, sin, cos_c, sin_c, n_cmp, norm_g[l], w_in, l, nsa_qk_g[l], nsa_cmp_pe[l], nsa_cmp_w1[l],
                   nsa_cmp_w2[l], sc_conv_w[l], s5_a_re[l], s5_a_im[l], s5_log_dt[l], s5_b_re[l], s5_b_im[l],
                   s5_c_re[l], s5_c_im[l], s5_d[l], s5_glu_w[l], s5_glu_b[l], w_branch[l], w_out[l])
    return x
```

```python
import functools
import math

import jax
import jax.numpy as jnp
from jax import lax
from jax.experimental import pallas as pl
from jax.experimental.pallas import tpu as pltpu

F32 = jnp.float32
BF16 = jnp.bfloat16

HEAD_DIM = 64
N_HEADS = 4
WIDTH = N_HEADS * HEAD_DIM
NSA_KV_HEADS = 2
NSA_CMP_LEN = 32
NSA_CMP_STRIDE = 16
NSA_SEL_LEN = 64
NSA_SEL_TOPN = 16
NSA_WINDOW = 512
NSA_FORCE_SCORE = 1.0e4
S5_GROUPS = 16
S5_GROUP_CH = 16
S5_STATE = 64
ROPE_THETA = 10000.0
NORM_EPS = 1e-6
QK_SCALE = HEAD_DIM ** -0.5

LANES = 128
SUBLANES = 8
NEG = -1.0e30
SEL_BIAS = -30000.0
SB_DEAD = -120.0
VMEM_LIMIT = 56 * 1024 * 1024

TM_PROJ = 512
TR_WPREP = 256
TT_CMP = 128
TQ_SB = 256
TK_SB = 256
TQ_SEL = 512
TK_SEL = 512
BIAS_SLOT = (2, 0)
TS_S5 = 128


def _dot(a, b):
    return jnp.dot(a, b, preferred_element_type=F32)


def _dot_nt(a, b):
    return lax.dot_general(a, b, (((1,), (1,)), ((), ())), preferred_element_type=F32)


def _silu(x):
    return x * (1.0 / (1.0 + jnp.exp(-x)))


def _sigmoid(x):
    return 1.0 / (1.0 + jnp.exp(-x))


def _iota(shape, dim):
    return lax.broadcasted_iota(jnp.int32, shape, dim)


def _group_mean_sq(x):
    outs = []
    lane = _iota((x.shape[0], LANES), 1)
    low = lane < HEAD_DIM
    for c in range(x.shape[1] // LANES):
        xc = x[:, c * LANES:(c + 1) * LANES]
        sq = xc * xc
        s_lo = jnp.sum(jnp.where(low, sq, 0.0), axis=-1, keepdims=True)
        s_hi = jnp.sum(jnp.where(low, 0.0, sq), axis=-1, keepdims=True)
        outs.append(jnp.where(low, s_lo, s_hi) * (1.0 / HEAD_DIM))
    return outs[0] if len(outs) == 1 else jnp.concatenate(outs, axis=1)


def _rot_half(y):
    w = y.shape[1]
    first = (_iota(y.shape, 1) & (HEAD_DIM // 2)) == 0
    return jnp.where(first, pltpu.roll(y, w - HEAD_DIM // 2, 1), pltpu.roll(y, HEAD_DIM // 2, 1))


def _norm_rope(x, gain, cos, sin_signed):
    reps = x.shape[1] // LANES
    y = x * lax.rsqrt(_group_mean_sq(x) + NORM_EPS) * gain
    c = cos if reps == 1 else jnp.concatenate([cos] * reps, axis=1)
    s = sin_signed if reps == 1 else jnp.concatenate([sin_signed] * reps, axis=1)
    return y * c + _rot_half(y) * s


def _stack_heads(q):
    head = _iota(q.shape, 1) // HEAD_DIM
    return jnp.concatenate([jnp.where(head == h, q, jnp.zeros_like(q)) for h in range(N_HEADS)], axis=0)


def _unstack_heads(o4, m):
    head = _iota((m, WIDTH), 1) // HEAD_DIM
    out = jnp.zeros((m, WIDTH), F32)
    for h in range(N_HEADS):
        out = jnp.where(head == h, o4[h * m:(h + 1) * m], out)
    return out


def _rope_table_kernel(pos_ref, freq_ref, sign_ref, cos_ref, sin_ref):
    ang = pos_ref[...].astype(F32) * freq_ref[...]
    cos_ref[...] = jnp.cos(ang)
    sin_ref[...] = jnp.sin(ang) * sign_ref[...]


def _rope_tables(pos_flat):
    n = pos_flat.shape[0]
    half = HEAD_DIM // 2
    inv_freq = jnp.power(ROPE_THETA, -jnp.arange(half, dtype=F32) / half)
    freq = jnp.tile(inv_freq, LANES // half)[None, :]
    sign = jnp.tile(jnp.concatenate([-jnp.ones((half,), F32), jnp.ones((half,), F32)]), LANES // HEAD_DIM)[None, :]
    tm = 512 if n % 512 == 0 else n
    return pl.pallas_call(
        _rope_table_kernel,
        grid=(n // tm,),
        in_specs=[pl.BlockSpec((tm, 1), lambda i: (i, 0)),
                  pl.BlockSpec((1, LANES), lambda i: (0, 0)),
                  pl.BlockSpec((1, LANES), lambda i: (0, 0))],
        out_specs=[pl.BlockSpec((tm, LANES), lambda i: (i, 0))] * 2,
        out_shape=[jax.ShapeDtypeStruct((n, LANES), F32)] * 2,
        name="rope_tables",
    )(pos_flat[:, None], freq, sign)


def _inproj_kernel(x_ref, ng_ref, wqk_ref, wv_ref, wc_ref, wg_ref, wz_ref, wsc_ref, wsb_ref, ws5_ref,
                   qkg_ref, cos_ref, sin_ref, convw_ref,
                   q_ref, ks_ref, kw_ref, vs_ref, vw_ref, gate_ref, zn_ref, zsb_ref, zs5_ref,
                   sc_ref, sbq_ref, sbk_ref, sbv_ref, s5u_ref, kc_ref, vc_ref, ubuf_ref, *, tm):
    x = x_ref[0]
    ms = jnp.mean(x * x, axis=-1, keepdims=True)
    h = (x * lax.rsqrt(ms + NORM_EPS) * ng_ref[...]).astype(BF16)

    qk = _norm_rope(_dot(h, wqk_ref[...]), qkg_ref[...], cos_ref[0], sin_ref[0])
    q_ref[0] = (qk[:, :WIDTH] * QK_SCALE).astype(BF16)
    ks_ref[0] = qk[:, WIDTH:2 * WIDTH].astype(BF16)
    kw_ref[0] = qk[:, 2 * WIDTH:].astype(BF16)

    v = _dot(h, wv_ref[...])
    v = jnp.where((_iota(v.shape, 1) // HEAD_DIM) % 2 == 1, 1.0, v)
    vs_ref[0] = v[:, :WIDTH].astype(BF16)
    vw_ref[0] = v[:, WIDTH:].astype(BF16)
    kvc = _dot(h, wc_ref[...])
    kc_ref[0] = kvc[:, :LANES]
    vc_ref[0] = kvc[:, LANES:]
    gate_ref[0] = _sigmoid(_dot(h, wg_ref[...]))

    z = _silu(_dot(h, wz_ref[...]))
    zn_ref[0] = z[:, :WIDTH]
    zsb_ref[0] = z[:, 2 * WIDTH:3 * WIDTH]
    zs5_ref[0] = z[:, 3 * WIDTH:]

    bcx = _dot(h, wsc_ref[...])
    u = bcx[:, WIDTH:2 * WIDTH] * bcx[:, 2 * WIDTH:]

    @pl.when(pl.program_id(1) == 0)
    def _():
        ubuf_ref[0:SUBLANES, :] = jnp.zeros((SUBLANES, WIDTH), F32)

    ubuf_ref[SUBLANES:SUBLANES + tm, :] = u
    u1 = ubuf_ref[SUBLANES - 1:SUBLANES - 1 + tm, :]
    u2 = ubuf_ref[SUBLANES - 2:SUBLANES - 2 + tm, :]
    cw = convw_ref[...]
    y = cw[2:3, :] * u + cw[1:2, :] * u1 + cw[0:1, :] * u2
    ubuf_ref[0:SUBLANES, :] = ubuf_ref[tm:tm + SUBLANES, :]
    sc_ref[0] = (bcx[:, :WIDTH] * y * z[:, WIDTH:2 * WIDTH]).astype(BF16)

    sb = _dot(h, wsb_ref[...])
    sbq_ref[0] = (sb[:, :WIDTH] * QK_SCALE).astype(BF16)
    sbk_ref[0] = sb[:, WIDTH:2 * WIDTH].astype(BF16)
    sbv_ref[0] = sb[:, 2 * WIDTH:].astype(BF16)
    s5u_ref[0] = _dot(h, ws5_ref[...])


def _inproj(x, ng, weights, qkg, cos, sin, convw):
    b, s, d = x.shape
    tm = min(TM_PROJ, s)
    full2 = lambda a: pl.BlockSpec(a.shape, lambda bi, si: (0, 0))
    tok = lambda w: pl.BlockSpec((1, tm, w), lambda bi, si: (bi, si, 0))
    out_widths = [(WIDTH, BF16)] * 5 + [(3 * WIDTH, F32)] + [(WIDTH, F32)] * 3 \
        + [(WIDTH, BF16)] * 4 + [(WIDTH, F32)] + [(LANES, F32)] * 2
    return pl.pallas_call(
        functools.partial(_inproj_kernel, tm=tm),
        grid=(b, s // tm),
        in_specs=[tok(d), full2(ng)] + [full2(w) for w in weights] + [full2(qkg), tok(LANES), tok(LANES), full2(convw)],
        out_specs=[tok(w) for w, _ in out_widths],
        out_shape=[jax.ShapeDtypeStruct((b, s, w), dt) for w, dt in out_widths],
        scratch_shapes=[pltpu.VMEM((tm + 2 * SUBLANES, WIDTH), F32)],
        compiler_params=pltpu.CompilerParams(dimension_semantics=("arbitrary", "arbitrary"),
                                             vmem_limit_bytes=VMEM_LIMIT),
        name="inproj",
    )(x, ng, *weights, qkg, cos, sin, convw)


def _compress_kernel(kc_ref, vc_ref, pe_ref, w1_ref, w2k_ref, w2v_ref, g_ref, cos_ref, sin_ref, kco_ref, vct_ref, *, nk):
    half = NSA_CMP_STRIDE
    for j, src in ((0, kc_ref), (1, vc_ref)):
        tok = [src[0, pl.ds(i, nk, stride=half), :] for i in range(half)]
        lo = jnp.concatenate([(tok[i] + pe_ref[j, 0, i:i + 1, :]).astype(BF16) for i in range(half)], axis=1)
        hi = jnp.concatenate([(tok[i] + pe_ref[j, 1, i:i + 1, :]).astype(BF16) for i in range(half)], axis=1)
        hid = _dot(lo, w1_ref[j, 0]) + pltpu.roll(_dot(hi, w1_ref[j, 1]), nk - 1, 0)
        act = _silu(hid).astype(BF16)
        if j == 0:
            kco_ref[0] = _norm_rope(_dot(act, w2k_ref[...]), g_ref[...], cos_ref[0], sin_ref[0]).astype(BF16)
        else:
            vct_ref[0] = _dot_nt(w2v_ref[...], act).astype(BF16)


def _compress(kc, vc, pe, w1, w2k, w2v, g, cos, sin):
    b, s, _ = kc.shape
    nk = s // NSA_CMP_STRIDE
    full = lambda a: pl.BlockSpec(a.shape, lambda bi: (0,) * a.ndim)
    seq = pl.BlockSpec((1, s, LANES), lambda bi: (bi, 0, 0))
    tab = pl.BlockSpec((1, nk, LANES), lambda bi: (bi, 0, 0))
    return pl.pallas_call(
        functools.partial(_compress_kernel, nk=nk),
        grid=(b,),
        in_specs=[seq, seq, full(pe), full(w1), full(w2k), full(w2v), full(g), tab, tab],
        out_specs=[pl.BlockSpec((1, nk, WIDTH), lambda bi: (bi, 0, 0)),
                   pl.BlockSpec((1, LANES, nk), lambda bi: (bi, 0, 0))],
        out_shape=[jax.ShapeDtypeStruct((b, nk, WIDTH), BF16), jax.ShapeDtypeStruct((b, LANES, nk), BF16)],
        compiler_params=pltpu.CompilerParams(dimension_semantics=("arbitrary",), vmem_limit_bytes=VMEM_LIMIT),
        name="compress",
    )(kc, vc, pe, w1, w2k, w2v, g, cos, sin)


def _cmp_topk_kernel(q_ref, kc_ref, vct_ref, ocmp_ref, bias_ref, p_scr, imp_scr, rank_scr,
                     *, tt, nk, n_cmp, n_sel, top_n):
    t0 = pl.program_id(1) * tt
    q = q_ref[0]
    kc = kc_ref[0]
    head = _iota(q.shape, 1) // HEAD_DIM
    nrow = _iota((nk, tt), 0)
    tcol = t0 + _iota((nk, tt), 1)
    valid_t = (nrow * NSA_CMP_STRIDE + (NSA_CMP_LEN - 1) <= tcol) & (nrow < n_cmp)
    brow = _iota((n_sel, tt), 0)
    tsel = t0 + _iota((n_sel, tt), 1)
    cur = tsel // NSA_SEL_LEN
    forced = (brow == 0) | (brow == cur) | (brow == cur - 1)
    in_past = brow * NSA_SEL_LEN <= tsel
    last_block = (t0 + tt - 1) // NSA_SEL_LEN
    sub = _iota((SUBLANES, tt), 0)
    heads_per_kv = N_HEADS // NSA_KV_HEADS
    scores = [_dot_nt(kc, jnp.where(head == h, q, jnp.zeros_like(q))) for h in range(N_HEADS)]
    probs = []
    for st in scores:
        st = jnp.where(valid_t, st, NEG)
        et = jnp.where(valid_t, jnp.exp(st - jnp.max(st, axis=0, keepdims=True)), 0.0)
        probs.append(et * (1.0 / jnp.maximum(jnp.sum(et, axis=0, keepdims=True), 1e-30)))
    outs = [_dot(vct_ref[0, (h // heads_per_kv) * HEAD_DIM:(h // heads_per_kv + 1) * HEAD_DIM, :],
                 probs[h].astype(BF16)) for h in range(N_HEADS)]
    groups = n_sel // SUBLANES
    imp_g = []
    for kh in range(NSA_KV_HEADS):
        psum = probs[kh * heads_per_kv]
        for g in range(1, heads_per_kv):
            psum = psum + probs[kh * heads_per_kv + g]
        p_scr[kh, SUBLANES:SUBLANES + nk, :] = psum
        p_scr[kh, 0:SUBLANES, :] = jnp.zeros((SUBLANES, tt), F32)
        ratio = NSA_SEL_LEN // NSA_CMP_STRIDE
        imp = p_scr[kh, pl.ds(SUBLANES - 1, n_sel, stride=ratio), :]
        for k in range(1, NSA_CMP_LEN // NSA_CMP_STRIDE + ratio - 1):
            imp = imp + p_scr[kh, pl.ds(SUBLANES - 1 + k, n_sel, stride=ratio), :]
        imp = jnp.where(forced, NSA_FORCE_SCORE, jnp.where(in_past, imp, -NSA_FORCE_SCORE))
        imp_scr[kh] = imp
        imp_g.append([imp[k * SUBLANES:(k + 1) * SUBLANES] for k in range(groups)])

    rank_scr[...] = jnp.zeros((NSA_KV_HEADS, n_sel, tt), F32)
    for c in range(groups):
        @pl.when(c * SUBLANES <= last_block)
        def _(c=c):
            for kh in range(NSA_KV_HEADS):
                for k in range(groups):
                    mine = imp_g[kh][k]
                    cnt = jnp.zeros((SUBLANES, tt), F32)
                    for r in range(SUBLANES):
                        row = imp_scr[kh, c * SUBLANES + r:c * SUBLANES + r + 1, :]
                        if k > c:
                            beats = row >= mine
                        elif k < c:
                            beats = row > mine
                        else:
                            beats = (row > mine) | ((row == mine) & (sub > r))
                        cnt = cnt + jnp.where(beats, 1.0, 0.0)
                    rank_scr[kh, k * SUBLANES:(k + 1) * SUBLANES, :] += cnt

    biases = [jnp.where(rank_scr[kh] < top_n, 0.0, SEL_BIAS) for kh in range(NSA_KV_HEADS)]
    ocmp_ref[0] = jnp.concatenate(outs, axis=0).T
    pad = jnp.zeros((2 * HEAD_DIM - n_sel, tt), F32)
    assert BIAS_SLOT == (2, 0)
    bias_ref[0] = jnp.concatenate([biases[1], pad, biases[0], pad], axis=0).T.astype(BF16)


def _cmp_topk(q, kc, vct, n_cmp):
    b, s, _ = q.shape
    nk = kc.shape[1]
    n_sel = s // NSA_SEL_LEN
    tt = min(TT_CMP, s)
    kern = functools.partial(_cmp_topk_kernel, tt=tt, nk=nk, n_cmp=n_cmp, n_sel=n_sel, top_n=min(NSA_SEL_TOPN, n_sel))
    return pl.pallas_call(
        kern,
        grid=(b, s // tt),
        in_specs=[pl.BlockSpec((1, tt, WIDTH), lambda bi, ti: (bi, ti, 0)),
                  pl.BlockSpec((1, nk, WIDTH), lambda bi, ti: (bi, 0, 0)),
                  pl.BlockSpec((1, LANES, nk), lambda bi, ti: (bi, 0, 0))],
        out_specs=[pl.BlockSpec((1, tt, WIDTH), lambda bi, ti: (bi, ti, 0))] * 2,
        out_shape=[jax.ShapeDtypeStruct((b, s, WIDTH), F32), jax.ShapeDtypeStruct((b, s, WIDTH), BF16)],
        scratch_shapes=[pltpu.VMEM((NSA_KV_HEADS, nk + SUBLANES, tt), F32), pltpu.VMEM((NSA_KV_HEADS, n_sel, tt), F32),
                        pltpu.VMEM((NSA_KV_HEADS, n_sel, tt), F32)],
        compiler_params=pltpu.CompilerParams(dimension_semantics=("arbitrary", "arbitrary"),
                                             vmem_limit_bytes=VMEM_LIMIT),
        name="cmp_topk",
    )(q, kc, vct)


def _selwin_kernel(q_ref, bias_ref, ocmp_ref, gate_ref, zn_ref, ks_ref, vs_ref, kw_ref, vw_ref, e_ref, out_ref,
                   acc_scr, m_scr, *, tq, tk):
    t0 = pl.program_id(1) * tq
    jd = t0 // tk
    q = q_ref[0]
    bias = bias_ref[0]
    slot = _iota(q.shape, 1) // HEAD_DIM
    zero = jnp.zeros_like(q)
    q_sel = jnp.concatenate(
        [jnp.where(slot == h, q, jnp.where(slot == BIAS_SLOT[h // (N_HEADS // NSA_KV_HEADS)], bias, zero))
         for h in range(N_HEADS)], axis=0)
    q_win = _stack_heads(q)
    rows = N_HEADS * tq
    trow = t0 + (_iota((rows, tk), 0) & (tq - 1))
    kslot = _iota((tk, WIDTH), 1) // HEAD_DIM
    lanes = lambda x, n: jnp.concatenate([x] * (n // LANES), axis=1)

    def tile(i, k_ref, v_ref, selected, masking, first):
        k0 = pl.multiple_of((jd - i) * tk, tk)
        kt = k_ref[0, pl.ds(k0, tk), :]
        if selected:
            et = e_ref[pl.ds(k0, tk), :]
            s = jnp.concatenate([_dot_nt(q_sel[kh * (rows // 2):(kh + 1) * (rows // 2)],
                                         jnp.where(kslot == BIAS_SLOT[kh], et, kt))
                                 for kh in range(NSA_KV_HEADS)], axis=0)
        else:
            s = _dot_nt(q_win, kt)
        if masking == "causal":
            s = jnp.where((k0 + _iota(s.shape, 1)) <= trow, s, NEG)
        elif masking == "window":
            s = jnp.where((k0 + _iota(s.shape, 1)) > trow - NSA_WINDOW, s, NEG)
        s_max = jnp.max(s, axis=-1, keepdims=True)
        if first:
            m_new = jnp.broadcast_to(s_max, (rows, LANES))
            p = jnp.exp(s - lanes(m_new, tk))
            acc_scr[...] = _dot(p.astype(BF16), v_ref[0, pl.ds(k0, tk), :])
        else:
            m_old = m_scr[...]
            m_new = jnp.maximum(m_old, s_max)
            alpha = jnp.exp(m_old - m_new)
            p = jnp.exp(s - lanes(m_new, tk))
            acc_scr[...] = lanes(alpha, WIDTH) * acc_scr[...] + _dot(p.astype(BF16), v_ref[0, pl.ds(k0, tk), :])
        m_scr[...] = m_new

    def result():
        outs = []
        for kh in range(NSA_KV_HEADS):
            pair = []
            for g in range(N_HEADS // NSA_KV_HEADS):
                h = kh * (N_HEADS // NSA_KV_HEADS) + g
                a = acc_scr[h * tq:(h + 1) * tq, kh * LANES:(kh + 1) * LANES]
                r = pltpu.roll(a, HEAD_DIM, 1)
                pair.append(a * (1.0 / r) if g == 0 else r * (1.0 / a))
            outs.append(jnp.where(_iota((tq, LANES), 1) < HEAD_DIM, pair[0], pair[1]))
        return jnp.concatenate(outs, axis=1)

    tile(0, ks_ref, vs_ref, True, "causal", True)

    def sel_body(i, carry):
        tile(i, ks_ref, vs_ref, True, "none", False)
        return carry

    lax.fori_loop(1, jd + 1, sel_body, 0)
    o_slc = result()

    tile(0, kw_ref, vw_ref, False, "causal", True)

    @pl.when(jd >= 1)
    def _():
        tile(1, kw_ref, vw_ref, False, "window", False)

    o_win = result()
    gate = gate_ref[0]
    o = gate[:, :WIDTH] * ocmp_ref[0] + gate[:, WIDTH:2 * WIDTH] * o_slc + gate[:, 2 * WIDTH:] * o_win
    out_ref[0] = (o * zn_ref[0]).astype(BF16)


def _selwin(q, bias, ocmp, gate, zn, ks, vs, kw, vw):
    b, s, _ = q.shape
    tq, tk = min(TQ_SEL, s), min(TK_SEL, s)
    assert NSA_WINDOW == tk and tk % tq == 0 and s // NSA_SEL_LEN <= HEAD_DIM
    member = (jnp.arange(s)[:, None] // NSA_SEL_LEN == jnp.arange(WIDTH)[None, :] % HEAD_DIM).astype(BF16)
    tok = lambda w: pl.BlockSpec((1, tq, w), lambda bi, qi: (bi, qi, 0))
    seq = pl.BlockSpec((1, s, WIDTH), lambda bi, qi: (bi, 0, 0))
    return pl.pallas_call(
        functools.partial(_selwin_kernel, tq=tq, tk=tk),
        grid=(b, s // tq),
        in_specs=[tok(WIDTH), tok(WIDTH), tok(WIDTH), tok(3 * WIDTH), tok(WIDTH), seq, seq, seq, seq,
                  pl.BlockSpec(member.shape, lambda bi, qi: (0, 0))],
        out_specs=tok(WIDTH),
        out_shape=jax.ShapeDtypeStruct((b, s, WIDTH), BF16),
        scratch_shapes=[pltpu.VMEM((N_HEADS * tq, WIDTH), F32), pltpu.VMEM((N_HEADS * tq, LANES), F32)],
        compiler_params=pltpu.CompilerParams(dimension_semantics=("arbitrary", "arbitrary"),
                                             vmem_limit_bytes=VMEM_LIMIT),
        name="selwin",
    )(q, bias, ocmp, gate, zn, ks, vs, kw, vw, member)


def _stickbrk_kernel(q_ref, k_ref, v_ref, z_ref, out_ref, acc_scr, carry_scr, *, tq, tk):
    t0 = pl.program_id(1) * tq
    jd = t0 // tk
    qs = _stack_heads(q_ref[0])
    rows = N_HEADS * tq
    trow = t0 + (_iota((rows, tk), 0) & (tq - 1))
    tri = jnp.where(_iota((tk, tk), 0) >= _iota((tk, tk), 1), 1.0, 0.0).astype(BF16)
    lanes = lambda x, n: jnp.concatenate([x] * (n // LANES), axis=1)

    def tile(i, diagonal):
        k0 = pl.multiple_of((jd - i) * tk, tk)
        z = _dot_nt(qs, k_ref[0, pl.ds(k0, tk), :])
        log1mb = -(jnp.maximum(z, 0.0) + jnp.log(1.0 + jnp.exp(-jnp.abs(z))))
        if diagonal:
            mask = (k0 + _iota(z.shape, 1)) < trow
            log1mb = jnp.where(mask, log1mb, 0.0)
        hi = log1mb.astype(BF16)
        lo = (log1mb - hi.astype(F32)).astype(BF16)
        suffix = _dot(hi, tri) + _dot(lo, tri)
        tile_sum = jnp.broadcast_to(jnp.sum(log1mb, axis=-1, keepdims=True), (rows, LANES))
        if diagonal:
            w = jnp.where(mask, jnp.exp(z + suffix), 0.0)
            acc_scr[...] = _dot(w.astype(BF16), v_ref[0, pl.ds(k0, tk), :])
            carry = tile_sum
        else:
            carry = carry_scr[...]
            w = jnp.exp(z + suffix + lanes(carry, tk))
            acc_scr[...] += _dot(w.astype(BF16), v_ref[0, pl.ds(k0, tk), :])
            carry = carry + tile_sum
        carry_scr[...] = carry
        return (jnp.max(carry) < SB_DEAD).astype(jnp.int32)

    def cond(c):
        i, dead = c
        return (i <= jd) & (dead == 0)

    def body(c):
        return c[0] + 1, tile(c[0], False)

    lax.while_loop(cond, body, (jnp.int32(1), tile(0, True)))
    out_ref[0] = (_unstack_heads(acc_scr[...], tq) * z_ref[0]).astype(BF16)


def _stickbrk(q, k, v, z):
    b, s, _ = q.shape
    tq, tk = min(TQ_SB, s), min(TK_SB, s)
    tok = pl.BlockSpec((1, tq, WIDTH), lambda bi, qi: (bi, qi, 0))
    seq = pl.BlockSpec((1, s, WIDTH), lambda bi, qi: (bi, 0, 0))
    return pl.pallas_call(
        functools.partial(_stickbrk_kernel, tq=tq, tk=tk),
        grid=(b, s // tq),
        in_specs=[tok, seq, seq, tok],
        out_specs=tok,
        out_shape=jax.ShapeDtypeStruct((b, s, WIDTH), BF16),
        scratch_shapes=[pltpu.VMEM((N_HEADS * tq, WIDTH), F32), pltpu.VMEM((N_HEADS * tq, LANES), F32)],
        compiler_params=pltpu.CompilerParams(dimension_semantics=("arbitrary", "arbitrary"),
                                             vmem_limit_bytes=VMEM_LIMIT),
        name="stickbrk",
    )(q, k, v, z)


def _s5_disc_kernel(are_ref, aim_ref, ldt_ref, bre_ref, bim_ref, abre_ref, abim_ref, bbre_ref, bbim_ref):
    dt = jnp.exp(ldt_ref[...])
    lr, li = are_ref[...], aim_ref[...]
    mag = jnp.exp(lr * dt)
    ab_re, ab_im = mag * jnp.cos(li * dt), mag * jnp.sin(li * dt)
    den = lr * lr + li * li
    coef_re = ((ab_re - 1.0) * lr + ab_im * li) / den
    coef_im = (ab_im * lr - (ab_re - 1.0) * li) / den
    abre_ref[...] = ab_re
    abim_ref[...] = ab_im
    br, bi = bre_ref[...], bim_ref[...]
    bbre_ref[...] = coef_re[:, None, :] * br - coef_im[:, None, :] * bi
    bbim_ref[...] = coef_re[:, None, :] * bi + coef_im[:, None, :] * br


def _s5_discretise(a_re, a_im, log_dt, b_re, b_im):
    g, p = a_re.shape
    brt, bit = jnp.swapaxes(b_re, 1, 2), jnp.swapaxes(b_im, 1, 2)
    return pl.pallas_call(
        _s5_disc_kernel,
        out_shape=[jax.ShapeDtypeStruct((g, p), F32)] * 2 + [jax.ShapeDtypeStruct(brt.shape, F32)] * 2,
        name="s5_disc",
    )(a_re, a_im, log_dt[:, None], brt, bit)


def _s5_kernel(u_ref, z_ref, bmat_ref, a_ref, cmat_ref, d_ref, gw_ref, gb_ref, out_ref, xs_scr, state_scr,
               *, nb, ts, ns):
    @pl.when(pl.program_id(0) == 0)
    def _():
        state_scr[...] = jnp.zeros((nb, 2 * ns), F32)

    nc = 2 * ns // LANES
    for b in range(nb):
        bu = _dot(u_ref[b].astype(BF16), bmat_ref[...])
        for c in range(nc):
            xs_scr[c, pl.ds(b, ts, stride=nb), :] = bu[:, c * LANES:(c + 1) * LANES]

    a_re = jnp.broadcast_to(a_ref[0:1, :], (nb, ns))
    a_im = jnp.broadcast_to(a_ref[1:2, :], (nb, ns))

    def step(t, state):
        x_re, x_im = state
        r0 = pl.multiple_of(t * nb, nb)
        bu = jnp.concatenate([xs_scr[c, pl.ds(r0, nb), :] for c in range(nc)], axis=1)
        n_re = a_re * x_re - a_im * x_im + bu[:, :ns]
        n_im = a_re * x_im + a_im * x_re + bu[:, ns:]
        for c in range(nc // 2):
            xs_scr[c, pl.ds(r0, nb), :] = n_re[:, c * LANES:(c + 1) * LANES]
            xs_scr[nc // 2 + c, pl.ds(r0, nb), :] = n_im[:, c * LANES:(c + 1) * LANES]
        return n_re, n_im

    st = state_scr[...]
    x_re, x_im = lax.fori_loop(0, ts, step, (st[:, :ns], st[:, ns:]))
    state_scr[...] = jnp.concatenate([x_re, x_im], axis=1)

    width = u_ref.shape[-1]
    for b in range(nb):
        xs = jnp.concatenate([xs_scr[c, pl.ds(b, ts, stride=nb), :] for c in range(nc)], axis=1)
        y = _dot(xs.astype(BF16), cmat_ref[...]) + d_ref[...] * u_ref[b]
        glu = _dot(y.astype(BF16), gw_ref[...]) + gb_ref[...]
        out_ref[b] = (glu[:, :width] * _sigmoid(glu[:, width:]) * z_ref[b]).astype(BF16)


def _s5(u, z, bmat, a_rows, cmat, d_row, glu_w, glu_b):
    nb, s, w = u.shape
    ts = min(TS_S5, s)
    ns = a_rows.shape[1]
    full = lambda a: pl.BlockSpec(a.shape, lambda i: (0, 0))
    tok = pl.BlockSpec((nb, ts, w), lambda i: (0, i, 0))
    return pl.pallas_call(
        functools.partial(_s5_kernel, nb=nb, ts=ts, ns=ns),
        grid=(s // ts,),
        in_specs=[tok, tok, full(bmat), full(a_rows), full(cmat), full(d_row), full(glu_w), full(glu_b)],
        out_specs=tok,
        out_shape=jax.ShapeDtypeStruct((nb, s, w), BF16),
        scratch_shapes=[pltpu.VMEM((2 * ns // LANES, ts * nb, LANES), F32), pltpu.VMEM((nb, 2 * ns), F32)],
        compiler_params=pltpu.CompilerParams(dimension_semantics=("arbitrary",), vmem_limit_bytes=VMEM_LIMIT),
        name="s5",
    )(u, z, bmat, a_rows, cmat, d_row, glu_w, glu_b)


def _merge_kernel(x_ref, ng_ref, o0_ref, o1_ref, o2_ref, o3_ref, wm_ref, wb_ref, wo_ref, out_ref):
    x = x_ref[...]
    ms = jnp.mean(x * x, axis=-1, keepdims=True)
    h = (x * lax.rsqrt(ms + NORM_EPS) * ng_ref[...]).astype(BF16)
    d = x.shape[1]
    mixed = None
    for m, o_ref in enumerate((o0_ref, o1_ref, o2_ref, o3_ref)):
        gate = _sigmoid(_dot(h, wm_ref[:, m * d:(m + 1) * d]))
        term = gate * _dot(o_ref[...], wb_ref[m])
        mixed = term if mixed is None else mixed + term
    out_ref[...] = x + _dot(mixed.astype(BF16), wo_ref[...])


def _merge(x2, ng, outs, wm, wb, wo):
    t, d = x2.shape
    tm = min(TM_PROJ, t)
    tok = lambda w: pl.BlockSpec((tm, w), lambda i: (i, 0))
    full = lambda a: pl.BlockSpec(a.shape, lambda i: (0,) * a.ndim)
    return pl.pallas_call(
        _merge_kernel,
        grid=(t // tm,),
        in_specs=[tok(d), full(ng)] + [tok(WIDTH)] * 4 + [full(wm), full(wb), full(wo)],
        out_specs=tok(d),
        out_shape=jax.ShapeDtypeStruct((t, d), F32),
        compiler_params=pltpu.CompilerParams(dimension_semantics=("arbitrary",), vmem_limit_bytes=VMEM_LIMIT),
        name="merge",
    )(x2, ng, *outs, wm, wb, wo)


def _wprep_kernel(w_ref, wqk_ref, wv_ref, wc_ref, wg_ref, wz_ref, wsc_ref, wsb_ref, ws5_ref, wm_ref, *, d):
    w = WIDTH
    o_q, o_kv, o_gate = 0, w, 4 * w
    o_nz = o_gate + 3 * N_HEADS
    o_sc, o_scz, o_sb, o_sbz, o_s5, o_s5z = o_nz + w, o_nz + 4 * w, o_nz + 5 * w, o_nz + 8 * w, o_nz + 9 * w, o_nz + 10 * w
    o_merge = o_nz + 11 * w
    col = lambda a, n: w_ref[0, :, a:a + n]
    kv = lambda i: o_kv + i * (w // 2)
    cat = lambda parts: jnp.concatenate(parts, axis=1).astype(BF16)

    def rep_heads(a):
        h0, h1 = col(a, HEAD_DIM), col(a + HEAD_DIM, HEAD_DIM)
        return [h0, h0, h1, h1]

    def spread_heads(a):
        h0, h1 = col(a, HEAD_DIM), col(a + HEAD_DIM, HEAD_DIM)
        return [h0, jnp.zeros_like(h0), h1, jnp.zeros_like(h0)]

    wqk_ref[...] = cat([col(o_q, w)] + rep_heads(kv(2)) + rep_heads(kv(4)))
    wv_ref[...] = cat(spread_heads(kv(3)) + spread_heads(kv(5)))
    wc_ref[...] = col(kv(0), w).astype(BF16)
    gates = col(o_gate, 3 * N_HEADS)
    wg_ref[...] = cat([jnp.broadcast_to(gates[:, c:c + 1], (gates.shape[0], HEAD_DIM)) for c in range(3 * N_HEADS)])
    wz_ref[...] = cat([col(o_nz, w), col(o_scz, w), col(o_sbz, w), col(o_s5z, w)])
    wsc_ref[...] = col(o_sc, 3 * w).astype(BF16)
    wsb_ref[...] = col(o_sb, 3 * w).astype(BF16)
    ws5_ref[...] = col(o_s5, w).astype(BF16)
    wm_ref[...] = col(o_merge, N_HEADS * d).astype(BF16)


def _wprep(w_in, layer):
    _, d, n_in = w_in.shape
    tr = min(TR_WPREP, d)
    widths = [3 * WIDTH, 2 * WIDTH, WIDTH, 3 * WIDTH, 4 * WIDTH, 3 * WIDTH, 3 * WIDTH, WIDTH, N_HEADS * d]
    return pl.pallas_call(
        functools.partial(_wprep_kernel, d=d),
        grid=(d // tr,),
        in_specs=[pl.BlockSpec((1, tr, n_in), lambda i: (layer, i, 0))],
        out_specs=[pl.BlockSpec((tr, n), lambda i: (i, 0)) for n in widths],
        out_shape=[jax.ShapeDtypeStruct((d, n), BF16) for n in widths],
        compiler_params=pltpu.CompilerParams(dimension_semantics=("arbitrary",), vmem_limit_bytes=VMEM_LIMIT),
        name="wprep",
    )(w_in)


def _block_diag(blocks):
    g, r, c = blocks.shape
    eye = jnp.eye(g, dtype=blocks.dtype)
    return (eye[:, None, :, None] * blocks[:, :, None, :]).reshape(g * r, g * c)


def _block_diag_pair(w1):
    eye = jnp.eye(NSA_KV_HEADS, dtype=w1.dtype)
    bd = w1[..., :, None, :, None, :] * eye[:, None, :, None]
    return bd.reshape(w1.shape[:-3] + (w1.shape[-3] * NSA_KV_HEADS * HEAD_DIM, NSA_KV_HEADS * HEAD_DIM))


def _layer(x, cos, sin, cos_c, sin_c, n_cmp, norm_g, w_in_all, layer, qk_g, cmp_pe, cmp_w1, cmp_w2, conv_w,
           a_re, a_im, log_dt, b_re, b_im, c_re, c_im, d_skip, glu_w, glu_b, w_branch, w_out):
    b, s, d = x.shape
    w = WIDTH
    *weights, w_merge = _wprep(w_in_all, layer)
    qkg = jnp.concatenate([jnp.tile(qk_g[0], N_HEADS), jnp.tile(qk_g[2], N_HEADS), jnp.tile(qk_g[3], N_HEADS)])[None, :]
    ng = norm_g[None, :]

    (q, ks, kw, vs, vw, gate, zn, zsb, zs5, sc_o, sbq, sbk, sbv, s5u, kc_in, vc_in) = _inproj(
        x, ng, weights, qkg, cos, sin, conv_w)

    half = NSA_CMP_STRIDE
    two = NSA_CMP_LEN // half
    pe = jnp.tile(cmp_pe.reshape(2, two, half, HEAD_DIM), (1, 1, 1, NSA_KV_HEADS))
    w1 = _block_diag_pair(cmp_w1.reshape(2, two, half, HEAD_DIM, HEAD_DIM))
    zeros = jnp.zeros_like(cmp_w2[0])
    w2k = jnp.concatenate([jnp.concatenate([cmp_w2[0], cmp_w2[0], zeros, zeros], axis=1),
                           jnp.concatenate([zeros, zeros, cmp_w2[0], cmp_w2[0]], axis=1)], axis=0)
    w2v = jnp.concatenate([jnp.concatenate([cmp_w2[1], zeros], axis=1),
                           jnp.concatenate([zeros, cmp_w2[1]], axis=1)], axis=0)
    kc, vct = _compress(kc_in, vc_in, pe, w1.astype(BF16), w2k.astype(BF16), w2v.T.astype(BF16),
                        jnp.tile(qk_g[1], N_HEADS)[None, :], cos_c, sin_c)

    ocmp, bias = _cmp_topk(q, kc, vct, n_cmp)
    nsa_o = _selwin(q, bias, ocmp, gate, zn, ks, vs, kw, vw)
    sb_o = _stickbrk(sbq, sbk, sbv, zsb)

    ab_re, ab_im, bb_re, bb_im = _s5_discretise(a_re, a_im, log_dt, b_re, b_im)
    bmat = jnp.concatenate([_block_diag(bb_re), _block_diag(bb_im)], axis=1).astype(BF16)
    cmat = jnp.concatenate([_block_diag(jnp.swapaxes(c_re, 1, 2)),
                            -_block_diag(jnp.swapaxes(c_im, 1, 2))], axis=0).astype(BF16)
    a_rows = jnp.stack([ab_re.reshape(-1), ab_im.reshape(-1)])
    s5_o = _s5(s5u, zs5, bmat, a_rows, cmat, d_skip.reshape(1, -1), glu_w.astype(BF16), glu_b[None, :])

    out = _merge(x.reshape(b * s, d), ng,
                 [o.reshape(b * s, w) for o in (nsa_o, sc_o, sb_o, s5_o)],
                 w_merge, w_branch.astype(BF16), w_out.astype(BF16))
    return out.reshape(b, s, d)


def kernel(x, positions, norm_g, w_in, nsa_qk_g, nsa_cmp_pe, nsa_cmp_w1, nsa_cmp_w2, sc_conv_w, s5_a_re, s5_a_im,
           s5_log_dt, s5_b_re, s5_b_im, s5_c_re, s5_c_im, s5_d, s5_glu_w, s5_glu_b, w_branch, w_out):
    b, s, _ = x.shape
    assert s % TK_SEL == 0 or s < TK_SB
    cos, sin = _rope_tables(positions.reshape(-1))
    cos, sin = cos.reshape(b, s, LANES), sin.reshape(b, s, LANES)
    nk = s // NSA_CMP_STRIDE
    n_cmp = (s - NSA_CMP_LEN) // NSA_CMP_STRIDE + 1
    pos_c = jnp.concatenate([positions[:, NSA_CMP_LEN - 1::NSA_CMP_STRIDE],
                             jnp.zeros((b, nk - n_cmp), positions.dtype)], axis=1)
    cos_c, sin_c = _rope_tables(pos_c.reshape(-1))
    cos_c, sin_c = cos_c.reshape(b, nk, LANES), sin_c.reshape(b, nk, LANES)
    for l in range(norm_g.shape[0]):
        x = _layer(x, cos, sin, cos_c, sin_c, n_cmp, norm_g[l], w_in, l, nsa_qk_g[l], nsa_cmp_pe[l], nsa_cmp_w1[l],
                   nsa_cmp_w2[l], sc_conv_w[l], s5_a_re[l], s5_a_im[l], s5_log_dt[l], s5_b_re[l], s5_b_im[l],
                   s5_c_re[l], s5_c_im[l], s5_d[l], s5_glu_w[l], s5_glu_b[l], w_branch[l], w_out[l])
    return x
```

```python
import functools
import math

import jax
import jax.numpy as jnp
from jax import lax
from jax.experimental import pallas as pl
from jax.experimental.pallas import tpu as pltpu

F32 = jnp.float32
BF16 = jnp.bfloat16

HEAD_DIM = 64
N_HEADS = 4
WIDTH = N_HEADS * HEAD_DIM
NSA_KV_HEADS = 2
NSA_CMP_LEN = 32
NSA_CMP_STRIDE = 16
NSA_SEL_LEN = 64
NSA_SEL_TOPN = 16
NSA_WINDOW = 512
NSA_FORCE_SCORE = 1.0e4
S5_GROUPS = 16
S5_GROUP_CH = 16
S5_STATE = 64
ROPE_THETA = 10000.0
NORM_EPS = 1e-6
QK_SCALE = HEAD_DIM ** -0.5

LANES = 128
SUBLANES = 8
NEG = -1.0e30
SEL_BIAS = -30000.0
SB_DEAD = -120.0
VMEM_LIMIT = 56 * 1024 * 1024

TM_PROJ = 512
TR_WPREP = 256
TT_CMP = 128
TQ_SB = 256
TK_SB = 256
TQ_SEL = 512
TK_SEL = 512
BIAS_SLOT = (2, 0)
TS_S5 = 128


def _dot(a, b):
    return jnp.dot(a, b, preferred_element_type=F32)


def _dot_nt(a, b):
    return lax.dot_general(a, b, (((1,), (1,)), ((), ())), preferred_element_type=F32)


def _silu(x):
    return x * (1.0 / (1.0 + jnp.exp(-x)))


def _sigmoid(x):
    return 1.0 / (1.0 + jnp.exp(-x))


def _iota(shape, dim):
    return lax.broadcasted_iota(jnp.int32, shape, dim)


def _group_mean_sq(x):
    outs = []
    lane = _iota((x.shape[0], LANES), 1)
    low = lane < HEAD_DIM
    for c in range(x.shape[1] // LANES):
        xc = x[:, c * LANES:(c + 1) * LANES]
        sq = xc * xc
        s_lo = jnp.sum(jnp.where(low, sq, 0.0), axis=-1, keepdims=True)
        s_hi = jnp.sum(jnp.where(low, 0.0, sq), axis=-1, keepdims=True)
        outs.append(jnp.where(low, s_lo, s_hi) * (1.0 / HEAD_DIM))
    return outs[0] if len(outs) == 1 else jnp.concatenate(outs, axis=1)


def _rot_half(y):
    w = y.shape[1]
    first = (_iota(y.shape, 1) & (HEAD_DIM // 2)) == 0
    return jnp.where(first, pltpu.roll(y, w - HEAD_DIM // 2, 1), pltpu.roll(y, HEAD_DIM // 2, 1))


def _norm_rope(x, gain, cos, sin_signed):
    reps = x.shape[1] // LANES
    y = x * lax.rsqrt(_group_mean_sq(x) + NORM_EPS) * gain
    c = cos if reps == 1 else jnp.concatenate([cos] * reps, axis=1)
    s = sin_signed if reps == 1 else jnp.concatenate([sin_signed] * reps, axis=1)
    return y * c + _rot_half(y) * s


def _stack_heads(q):
    head = _iota(q.shape, 1) // HEAD_DIM
    return jnp.concatenate([jnp.where(head == h, q, jnp.zeros_like(q)) for h in range(N_HEADS)], axis=0)


def _unstack_heads(o4, m):
    head = _iota((m, WIDTH), 1) // HEAD_DIM
    out = jnp.zeros((m, WIDTH), F32)
    for h in range(N_HEADS):
        out = jnp.where(head == h, o4[h * m:(h + 1) * m], out)
    return out


def _rope_table_kernel(pos_ref, freq_ref, sign_ref, cos_ref, sin_ref):
    ang = pos_ref[...].astype(F32) * freq_ref[...]
    cos_ref[...] = jnp.cos(ang)
    sin_ref[...] = jnp.sin(ang) * sign_ref[...]


def _rope_tables(pos_flat):
    n = pos_flat.shape[0]
    half = HEAD_DIM // 2
    inv_freq = jnp.power(ROPE_THETA, -jnp.arange(half, dtype=F32) / half)
    freq = jnp.tile(inv_freq, LANES // half)[None, :]
    sign = jnp.tile(jnp.concatenate([-jnp.ones((half,), F32), jnp.ones((half,), F32)]), LANES // HEAD_DIM)[None, :]
    tm = 512 if n % 512 == 0 else n
    return pl.pallas_call(
        _rope_table_kernel,
        grid=(n // tm,),
        in_specs=[pl.BlockSpec((tm, 1), lambda i: (i, 0)),
                  pl.BlockSpec((1, LANES), lambda i: (0, 0)),
                  pl.BlockSpec((1, LANES), lambda i: (0, 0))],
        out_specs=[pl.BlockSpec((tm, LANES), lambda i: (i, 0))] * 2,
        out_shape=[jax.ShapeDtypeStruct((n, LANES), F32)] * 2,
        name="rope_tables",
    )(pos_flat[:, None], freq, sign)


def _inproj_kernel(x_ref, ng_ref, wqk_ref, wv_ref, wc_ref, wg_ref, wz_ref, wsc_ref, wsb_ref, ws5_ref,
                   qkg_ref, cos_ref, sin_ref, convw_ref,
                   q_ref, ks_ref, kw_ref, vs_ref, vw_ref, gate_ref, zn_ref, zsb_ref, zs5_ref,
                   sc_ref, sbq_ref, sbk_ref, sbv_ref, s5u_ref, kc_ref, vc_ref, ubuf_ref, *, tm):
    x = x_ref[0]
    ms = jnp.mean(x * x, axis=-1, keepdims=True)
    h = (x * lax.rsqrt(ms + NORM_EPS) * ng_ref[...]).astype(BF16)

    qk = _norm_rope(_dot(h, wqk_ref[...]), qkg_ref[...], cos_ref[0], sin_ref[0])
    q_ref[0] = (qk[:, :WIDTH] * QK_SCALE).astype(BF16)
    ks_ref[0] = qk[:, WIDTH:2 * WIDTH].astype(BF16)
    kw_ref[0] = qk[:, 2 * WIDTH:].astype(BF16)

    v = _dot(h, wv_ref[...])
    v = jnp.where((_iota(v.shape, 1) // HEAD_DIM) % 2 == 1, 1.0, v)
    vs_ref[0] = v[:, :WIDTH].astype(BF16)
    vw_ref[0] = v[:, WIDTH:].astype(BF16)
    kvc = _dot(h, wc_ref[...])
    kc_ref[0] = kvc[:, :LANES]
    vc_ref[0] = kvc[:, LANES:]
    gate_ref[0] = _sigmoid(_dot(h, wg_ref[...]))

    z = _silu(_dot(h, wz_ref[...]))
    zn_ref[0] = z[:, :WIDTH]
    zsb_ref[0] = z[:, 2 * WIDTH:3 * WIDTH]
    zs5_ref[0] = z[:, 3 * WIDTH:]

    bcx = _dot(h, wsc_ref[...])
    u = bcx[:, WIDTH:2 * WIDTH] * bcx[:, 2 * WIDTH:]

    @pl.when(pl.program_id(1) == 0)
    def _():
        ubuf_ref[0:SUBLANES, :] = jnp.zeros((SUBLANES, WIDTH), F32)

    ubuf_ref[SUBLANES:SUBLANES + tm, :] = u
    u1 = ubuf_ref[SUBLANES - 1:SUBLANES - 1 + tm, :]
    u2 = ubuf_ref[SUBLANES - 2:SUBLANES - 2 + tm, :]
    cw = convw_ref[...]
    y = cw[2:3, :] * u + cw[1:2, :] * u1 + cw[0:1, :] * u2
    ubuf_ref[0:SUBLANES, :] = ubuf_ref[tm:tm + SUBLANES, :]
    sc_ref[0] = (bcx[:, :WIDTH] * y * z[:, WIDTH:2 * WIDTH]).astype(BF16)

    sb = _dot(h, wsb_ref[...])
    sbq_ref[0] = (sb[:, :WIDTH] * QK_SCALE).astype(BF16)
    sbk_ref[0] = sb[:, WIDTH:2 * WIDTH].astype(BF16)
    sbv_ref[0] = sb[:, 2 * WIDTH:].astype(BF16)
    s5u_ref[0] = _dot(h, ws5_ref[...])


def _inproj(x, ng, weights, qkg, cos, sin, convw):
    b, s, d = x.shape
    tm = min(TM_PROJ, s)
    full2 = lambda a: pl.BlockSpec(a.shape, lambda bi, si: (0, 0))
    tok = lambda w: pl.BlockSpec((1, tm, w), lambda bi, si: (bi, si, 0))
    out_widths = [(WIDTH, BF16)] * 5 + [(3 * WIDTH, F32)] + [(WIDTH, F32)] * 3 \
        + [(WIDTH, BF16)] * 4 + [(WIDTH, F32)] + [(LANES, F32)] * 2
    return pl.pallas_call(
        functools.partial(_inproj_kernel, tm=tm),
        grid=(b, s // tm),
        in_specs=[tok(d), full2(ng)] + [full2(w) for w in weights] + [full2(qkg), tok(LANES), tok(LANES), full2(convw)],
        out_specs=[tok(w) for w, _ in out_widths],
        out_shape=[jax.ShapeDtypeStruct((b, s, w), dt) for w, dt in out_widths],
        scratch_shapes=[pltpu.VMEM((tm + 2 * SUBLANES, WIDTH), F32)],
        compiler_params=pltpu.CompilerParams(dimension_semantics=("arbitrary", "arbitrary"),
                                             vmem_limit_bytes=VMEM_LIMIT),
        name="inproj",
    )(x, ng, *weights, qkg, cos, sin, convw)


def _compress_kernel(kc_ref, vc_ref, pe_ref, w1_ref, w2k_ref, w2v_ref, g_ref, cos_ref, sin_ref, kco_ref, vct_ref, *, nk):
    half = NSA_CMP_STRIDE
    for j, src in ((0, kc_ref), (1, vc_ref)):
        tok = [src[0, pl.ds(i, nk, stride=half), :] for i in range(half)]
        lo = jnp.concatenate([(tok[i] + pe_ref[j, 0, i:i + 1, :]).astype(BF16) for i in range(half)], axis=1)
        hi = jnp.concatenate([(tok[i] + pe_ref[j, 1, i:i + 1, :]).astype(BF16) for i in range(half)], axis=1)
        hid = _dot(lo, w1_ref[j, 0]) + pltpu.roll(_dot(hi, w1_ref[j, 1]), nk - 1, 0)
        act = _silu(hid).astype(BF16)
        if j == 0:
            kco_ref[0] = _norm_rope(_dot(act, w2k_ref[...]), g_ref[...], cos_ref[0], sin_ref[0]).astype(BF16)
        else:
            vct_ref[0] = _dot_nt(w2v_ref[...], act).astype(BF16)


def _compress(kc, vc, pe, w1, w2k, w2v, g, cos, sin):
    b, s, _ = kc.shape
    nk = s // NSA_CMP_STRIDE
    full = lambda a: pl.BlockSpec(a.shape, lambda bi: (0,) * a.ndim)
    seq = pl.BlockSpec((1, s, LANES), lambda bi: (bi, 0, 0))
    tab = pl.BlockSpec((1, nk, LANES), lambda bi: (bi, 0, 0))
    return pl.pallas_call(
        functools.partial(_compress_kernel, nk=nk),
        grid=(b,),
        in_specs=[seq, seq, full(pe), full(w1), full(w2k), full(w2v), full(g), tab, tab],
        out_specs=[pl.BlockSpec((1, nk, WIDTH), lambda bi: (bi, 0, 0)),
                   pl.BlockSpec((1, LANES, nk), lambda bi: (bi, 0, 0))],
        out_shape=[jax.ShapeDtypeStruct((b, nk, WIDTH), BF16), jax.ShapeDtypeStruct((b, LANES, nk), BF16)],
        compiler_params=pltpu.CompilerParams(dimension_semantics=("arbitrary",), vmem_limit_bytes=VMEM_LIMIT),
        name="compress",
    )(kc, vc, pe, w1, w2k, w2v, g, cos, sin)


def _cmp_topk_kernel(q_ref, kc_ref, vct_ref, ocmp_ref, bias_ref, p_scr, imp_scr, rank_scr,
                     *, tt, nk, n_cmp, n_sel, top_n):
    t0 = pl.program_id(1) * tt
    q = q_ref[0]
    kc = kc_ref[0]
    head = _iota(q.shape, 1) // HEAD_DIM
    nrow = _iota((nk, tt), 0)
    tcol = t0 + _iota((nk, tt), 1)
    valid_t = (nrow * NSA_CMP_STRIDE + (NSA_CMP_LEN - 1) <= tcol) & (nrow < n_cmp)
    brow = _iota((n_sel, tt), 0)
    tsel = t0 + _iota((n_sel, tt), 1)
    cur = tsel // NSA_SEL_LEN
    forced = (brow == 0) | (brow == cur) | (brow == cur - 1)
    in_past = brow * NSA_SEL_LEN <= tsel
    last_block = (t0 + tt - 1) // NSA_SEL_LEN
    sub = _iota((SUBLANES, tt), 0)
    heads_per_kv = N_HEADS // NSA_KV_HEADS
    scores = [_dot_nt(kc, jnp.where(head == h, q, jnp.zeros_like(q))) for h in range(N_HEADS)]
    probs = []
    for st in scores:
        st = jnp.where(valid_t, st, NEG)
        et = jnp.where(valid_t, jnp.exp(st - jnp.max(st, axis=0, keepdims=True)), 0.0)
        probs.append(et * (1.0 / jnp.maximum(jnp.sum(et, axis=0, keepdims=True), 1e-30)))
    outs = [_dot(vct_ref[0, (h // heads_per_kv) * HEAD_DIM:(h // heads_per_kv + 1) * HEAD_DIM, :],
                 probs[h].astype(BF16)) for h in range(N_HEADS)]
    groups = n_sel // SUBLANES
    imp_g = []
    for kh in range(NSA_KV_HEADS):
        psum = probs[kh * heads_per_kv]
        for g in range(1, heads_per_kv):
            psum = psum + probs[kh * heads_per_kv + g]
        p_scr[kh, SUBLANES:SUBLANES + nk, :] = psum
        p_scr[kh, 0:SUBLANES, :] = jnp.zeros((SUBLANES, tt), F32)
        ratio = NSA_SEL_LEN // NSA_CMP_STRIDE
        imp = p_scr[kh, pl.ds(SUBLANES - 1, n_sel, stride=ratio), :]
        for k in range(1, NSA_CMP_LEN // NSA_CMP_STRIDE + ratio - 1):
            imp = imp + p_scr[kh, pl.ds(SUBLANES - 1 + k, n_sel, stride=ratio), :]
        imp = jnp.where(forced, NSA_FORCE_SCORE, jnp.where(in_past, imp, -NSA_FORCE_SCORE))
        imp_scr[kh] = imp
        imp_g.append([imp[k * SUBLANES:(k + 1) * SUBLANES] for k in range(groups)])

    rank_scr[...] = jnp.zeros((NSA_KV_HEADS, n_sel, tt), F32)
    for m in range(groups):
        @pl.when(m * SUBLANES <= last_block)
        def _(m=m):
            for kh in range(NSA_KV_HEADS):
                for k in range(m + 1):
                    mine = imp_g[kh][k]
                    cnt = jnp.zeros((SUBLANES, tt), F32)
                    for c in ([m] if k < m else range(m + 1)):
                        for r in range(SUBLANES):
                            row = imp_scr[kh, c * SUBLANES + r:c * SUBLANES + r + 1, :]
                            if k > c:
                                beats = row >= mine
                            elif k < c:
                                beats = row > mine
                            else:
                                beats = (row > mine) | ((row == mine) & (sub > r))
                            cnt = cnt + jnp.where(beats, 1.0, 0.0)
                    rank_scr[kh, k * SUBLANES:(k + 1) * SUBLANES, :] += cnt

    biases = [jnp.where(rank_scr[kh] < top_n, 0.0, SEL_BIAS) for kh in range(NSA_KV_HEADS)]
    ocmp_ref[0] = jnp.concatenate(outs, axis=0).T
    pad = jnp.zeros((2 * HEAD_DIM - n_sel, tt), F32)
    assert BIAS_SLOT == (2, 0)
    bias_ref[0] = jnp.concatenate([biases[1], pad, biases[0], pad], axis=0).T.astype(BF16)


def _cmp_topk(q, kc, vct, n_cmp):
    b, s, _ = q.shape
    nk = kc.shape[1]
    n_sel = s // NSA_SEL_LEN
    tt = min(TT_CMP, s)
    kern = functools.partial(_cmp_topk_kernel, tt=tt, nk=nk, n_cmp=n_cmp, n_sel=n_sel, top_n=min(NSA_SEL_TOPN, n_sel))
    return pl.pallas_call(
        kern,
        grid=(b, s // tt),
        in_specs=[pl.BlockSpec((1, tt, WIDTH), lambda bi, ti: (bi, ti, 0)),
                  pl.BlockSpec((1, nk, WIDTH), lambda bi, ti: (bi, 0, 0)),
                  pl.BlockSpec((1, LANES, nk), lambda bi, ti: (bi, 0, 0))],
        out_specs=[pl.BlockSpec((1, tt, WIDTH), lambda bi, ti: (bi, ti, 0))] * 2,
        out_shape=[jax.ShapeDtypeStruct((b, s, WIDTH), F32), jax.ShapeDtypeStruct((b, s, WIDTH), BF16)],
        scratch_shapes=[pltpu.VMEM((NSA_KV_HEADS, nk + SUBLANES, tt), F32), pltpu.VMEM((NSA_KV_HEADS, n_sel, tt), F32),
                        pltpu.VMEM((NSA_KV_HEADS, n_sel, tt), F32)],
        compiler_params=pltpu.CompilerParams(dimension_semantics=("arbitrary", "arbitrary"),
                                             vmem_limit_bytes=VMEM_LIMIT),
        name="cmp_topk",
    )(q, kc, vct)


CAUSAL, WINDOW = 0, 1


def _selwin_kernel(q_ref, bias_ref, ocmp_ref, gate_ref, zn_ref, ks_ref, vs_ref, kw_ref, vw_ref, e_ref, cm_ref, out_ref,
                   acc_scr, m_scr, *, tq, tk):
    t0 = pl.program_id(1) * tq
    jd = t0 // tk
    q = q_ref[0]
    bias = bias_ref[0]
    slot = _iota(q.shape, 1) // HEAD_DIM
    zero = jnp.zeros_like(q)
    q_sel = jnp.concatenate(
        [jnp.where(slot == h, q, jnp.where(slot == BIAS_SLOT[h // (N_HEADS // NSA_KV_HEADS)], bias, zero))
         for h in range(N_HEADS)], axis=0)
    q_win = _stack_heads(q)
    rows = N_HEADS * tq
    kslot = _iota((tk, WIDTH), 1) // HEAD_DIM
    lanes = lambda x, n: jnp.concatenate([x] * (n // LANES), axis=1)

    def tile(i, k_ref, v_ref, selected, masking, first):
        k0 = pl.multiple_of((jd - i) * tk, tk)
        kt = k_ref[0, pl.ds(k0, tk), :]
        if selected:
            et = e_ref[pl.ds(k0, tk), :]
            s = jnp.concatenate([_dot_nt(q_sel[kh * (rows // 2):(kh + 1) * (rows // 2)],
                                         jnp.where(kslot == BIAS_SLOT[kh], et, kt))
                                 for kh in range(NSA_KV_HEADS)], axis=0)
        else:
            s = _dot_nt(q_win, kt)
        if masking is not None:
            s = s + jnp.concatenate([cm_ref[masking]] * N_HEADS, axis=0)
        s_max = jnp.max(s, axis=-1, keepdims=True)
        if first:
            m_new = jnp.broadcast_to(s_max, (rows, LANES))
            p = jnp.exp(s - lanes(m_new, tk))
            acc_scr[...] = _dot(p.astype(BF16), v_ref[0, pl.ds(k0, tk), :])
        else:
            m_old = m_scr[...]
            m_new = jnp.maximum(m_old, s_max)
            alpha = jnp.exp(m_old - m_new)
            p = jnp.exp(s - lanes(m_new, tk))
            acc_scr[...] = lanes(alpha, WIDTH) * acc_scr[...] + _dot(p.astype(BF16), v_ref[0, pl.ds(k0, tk), :])
        m_scr[...] = m_new

    def result():
        outs = []
        for kh in range(NSA_KV_HEADS):
            pair = []
            for g in range(N_HEADS // NSA_KV_HEADS):
                h = kh * (N_HEADS // NSA_KV_HEADS) + g
                a = acc_scr[h * tq:(h + 1) * tq, kh * LANES:(kh + 1) * LANES]
                r = pltpu.roll(a, HEAD_DIM, 1)
                pair.append(a * (1.0 / r) if g == 0 else r * (1.0 / a))
            outs.append(jnp.where(_iota((tq, LANES), 1) < HEAD_DIM, pair[0], pair[1]))
        return jnp.concatenate(outs, axis=1)

    tile(0, ks_ref, vs_ref, True, CAUSAL, True)

    def sel_body(i, carry):
        tile(i, ks_ref, vs_ref, True, None, False)
        return carry

    lax.fori_loop(1, jd + 1, sel_body, 0)
    o_slc = result()

    tile(0, kw_ref, vw_ref, False, CAUSAL, True)

    @pl.when(jd >= 1)
    def _():
        tile(1, kw_ref, vw_ref, False, WINDOW, False)

    o_win = result()
    gate = gate_ref[0]
    o = gate[:, :WIDTH] * ocmp_ref[0] + gate[:, WIDTH:2 * WIDTH] * o_slc + gate[:, 2 * WIDTH:] * o_win
    out_ref[0] = (o * zn_ref[0]).astype(BF16)


def _selwin(q, bias, ocmp, gate, zn, ks, vs, kw, vw):
    b, s, _ = q.shape
    tq, tk = min(TQ_SEL, s), min(TK_SEL, s)
    assert NSA_WINDOW == tk and tk == tq and s // NSA_SEL_LEN <= HEAD_DIM
    member = (jnp.arange(s)[:, None] // NSA_SEL_LEN == jnp.arange(WIDTH)[None, :] % HEAD_DIM).astype(BF16)
    row, col = jnp.arange(tq)[:, None], jnp.arange(tk)[None, :]
    cmask = jnp.where(jnp.stack([col <= row, col > row]), 0.0, NEG).astype(F32)
    tok = lambda w: pl.BlockSpec((1, tq, w), lambda bi, qi: (bi, qi, 0))
    seq = pl.BlockSpec((1, s, WIDTH), lambda bi, qi: (bi, 0, 0))
    return pl.pallas_call(
        functools.partial(_selwin_kernel, tq=tq, tk=tk),
        grid=(b, s // tq),
        in_specs=[tok(WIDTH), tok(WIDTH), tok(WIDTH), tok(3 * WIDTH), tok(WIDTH), seq, seq, seq, seq,
                  pl.BlockSpec(member.shape, lambda bi, qi: (0, 0)),
                  pl.BlockSpec(cmask.shape, lambda bi, qi: (0, 0, 0))],
        out_specs=tok(WIDTH),
        out_shape=jax.ShapeDtypeStruct((b, s, WIDTH), BF16),
        scratch_shapes=[pltpu.VMEM((N_HEADS * tq, WIDTH), F32), pltpu.VMEM((N_HEADS * tq, LANES), F32)],
        compiler_params=pltpu.CompilerParams(dimension_semantics=("arbitrary", "arbitrary"),
                                             vmem_limit_bytes=VMEM_LIMIT),
        name="selwin",
    )(q, bias, ocmp, gate, zn, ks, vs, kw, vw, member, cmask)


def _stickbrk_kernel(q_ref, k_ref, v_ref, z_ref, mk_ref, out_ref, acc_scr, carry_scr, *, tq, tk):
    t0 = pl.program_id(1) * tq
    jd = t0 // tk
    qs = _stack_heads(q_ref[0])
    rows = N_HEADS * tq
    tri = jnp.where(_iota((tk, tk), 0) >= _iota((tk, tk), 1), 1.0, 0.0).astype(BF16)
    lanes = lambda x, n: jnp.concatenate([x] * (n // LANES), axis=1)

    def tile(i, diagonal):
        k0 = pl.multiple_of((jd - i) * tk, tk)
        z = _dot_nt(qs, k_ref[0, pl.ds(k0, tk), :])
        log1mb = -(jnp.maximum(z, 0.0) + jnp.log(1.0 + jnp.exp(-jnp.abs(z))))
        if diagonal:
            log1mb = log1mb * jnp.concatenate([mk_ref[0]] * N_HEADS, axis=0)
        hi = log1mb.astype(BF16)
        lo = (log1mb - hi.astype(F32)).astype(BF16)
        suffix = _dot(hi, tri) + _dot(lo, tri)
        tile_sum = jnp.broadcast_to(jnp.sum(log1mb, axis=-1, keepdims=True), (rows, LANES))
        if diagonal:
            w = jnp.exp(z + suffix + jnp.concatenate([mk_ref[1]] * N_HEADS, axis=0))
            acc_scr[...] = _dot(w.astype(BF16), v_ref[0, pl.ds(k0, tk), :])
            carry = tile_sum
        else:
            carry = carry_scr[...]
            w = jnp.exp(z + suffix + lanes(carry, tk))
            acc_scr[...] += _dot(w.astype(BF16), v_ref[0, pl.ds(k0, tk), :])
            carry = carry + tile_sum
        carry_scr[...] = carry
        return (jnp.max(carry) < SB_DEAD).astype(jnp.int32)

    def cond(c):
        i, dead = c
        return (i <= jd) & (dead == 0)

    def body(c):
        return c[0] + 1, tile(c[0], False)

    lax.while_loop(cond, body, (jnp.int32(1), tile(0, True)))
    out_ref[0] = (_unstack_heads(acc_scr[...], tq) * z_ref[0]).astype(BF16)


def _stickbrk(q, k, v, z):
    b, s, _ = q.shape
    tq, tk = min(TQ_SB, s), min(TK_SB, s)
    assert tq == tk
    keep = jnp.arange(tk)[None, :] < jnp.arange(tq)[:, None]
    masks = jnp.stack([jnp.where(keep, 1.0, 0.0), jnp.where(keep, 0.0, NEG)]).astype(F32)
    tok = pl.BlockSpec((1, tq, WIDTH), lambda bi, qi: (bi, qi, 0))
    seq = pl.BlockSpec((1, s, WIDTH), lambda bi, qi: (bi, 0, 0))
    return pl.pallas_call(
        functools.partial(_stickbrk_kernel, tq=tq, tk=tk),
        grid=(b, s // tq),
        in_specs=[tok, seq, seq, tok, pl.BlockSpec(masks.shape, lambda bi, qi: (0, 0, 0))],
        out_specs=tok,
        out_shape=jax.ShapeDtypeStruct((b, s, WIDTH), BF16),
        scratch_shapes=[pltpu.VMEM((N_HEADS * tq, WIDTH), F32), pltpu.VMEM((N_HEADS * tq, LANES), F32)],
        compiler_params=pltpu.CompilerParams(dimension_semantics=("arbitrary", "arbitrary"),
                                             vmem_limit_bytes=VMEM_LIMIT),
        name="stickbrk",
    )(q, k, v, z, masks)


def _s5_disc_kernel(are_ref, aim_ref, ldt_ref, bre_ref, bim_ref, abre_ref, abim_ref, bbre_ref, bbim_ref):
    dt = jnp.exp(ldt_ref[...])
    lr, li = are_ref[...], aim_ref[...]
    mag = jnp.exp(lr * dt)
    ab_re, ab_im = mag * jnp.cos(li * dt), mag * jnp.sin(li * dt)
    den = lr * lr + li * li
    coef_re = ((ab_re - 1.0) * lr + ab_im * li) / den
    coef_im = (ab_im * lr - (ab_re - 1.0) * li) / den
    abre_ref[...] = ab_re
    abim_ref[...] = ab_im
    br, bi = bre_ref[...], bim_ref[...]
    bbre_ref[...] = coef_re[:, None, :] * br - coef_im[:, None, :] * bi
    bbim_ref[...] = coef_re[:, None, :] * bi + coef_im[:, None, :] * br


def _s5_discretise(a_re, a_im, log_dt, b_re, b_im):
    g, p = a_re.shape
    brt, bit = jnp.swapaxes(b_re, 1, 2), jnp.swapaxes(b_im, 1, 2)
    return pl.pallas_call(
        _s5_disc_kernel,
        out_shape=[jax.ShapeDtypeStruct((g, p), F32)] * 2 + [jax.ShapeDtypeStruct(brt.shape, F32)] * 2,
        name="s5_disc",
    )(a_re, a_im, log_dt[:, None], brt, bit)


def _s5_kernel(u_ref, z_ref, bmat_ref, a_ref, cmat_ref, d_ref, gw_ref, gb_ref, out_ref, xs_scr, state_scr,
               *, nb, ts, ns):
    @pl.when(pl.program_id(0) == 0)
    def _():
        state_scr[...] = jnp.zeros((nb, 2 * ns), F32)

    nc = 2 * ns // LANES
    for b in range(nb):
        bu = _dot(u_ref[b].astype(BF16), bmat_ref[...])
        for c in range(nc):
            xs_scr[c, pl.ds(b, ts, stride=nb), :] = bu[:, c * LANES:(c + 1) * LANES]

    a_re = jnp.broadcast_to(a_ref[0:1, :], (nb, ns))
    a_im = jnp.broadcast_to(a_ref[1:2, :], (nb, ns))

    def step(t, state):
        x_re, x_im = state
        r0 = pl.multiple_of(t * nb, nb)
        bu = jnp.concatenate([xs_scr[c, pl.ds(r0, nb), :] for c in range(nc)], axis=1)
        n_re = a_re * x_re - a_im * x_im + bu[:, :ns]
        n_im = a_re * x_im + a_im * x_re + bu[:, ns:]
        for c in range(nc // 2):
            xs_scr[c, pl.ds(r0, nb), :] = n_re[:, c * LANES:(c + 1) * LANES]
            xs_scr[nc // 2 + c, pl.ds(r0, nb), :] = n_im[:, c * LANES:(c + 1) * LANES]
        return n_re, n_im

    st = state_scr[...]
    x_re, x_im = lax.fori_loop(0, ts, step, (st[:, :ns], st[:, ns:]))
    state_scr[...] = jnp.concatenate([x_re, x_im], axis=1)

    width = u_ref.shape[-1]
    ys = []
    for b in range(nb):
        xs = jnp.concatenate([xs_scr[c, pl.ds(b, ts, stride=nb), :] for c in range(nc)], axis=1)
        ys.append((_dot(xs.astype(BF16), cmat_ref[...]) + d_ref[...] * u_ref[b]).astype(BF16))
    for b in range(nb):
        glu = _dot(ys[b], gw_ref[...]) + gb_ref[...]
        out_ref[b] = (glu[:, :width] * _sigmoid(glu[:, width:]) * z_ref[b]).astype(BF16)


def _s5(u, z, bmat, a_rows, cmat, d_row, glu_w, glu_b):
    nb, s, w = u.shape
    ts = min(TS_S5, s)
    ns = a_rows.shape[1]
    full = lambda a: pl.BlockSpec(a.shape, lambda i: (0, 0))
    tok = pl.BlockSpec((nb, ts, w), lambda i: (0, i, 0))
    return pl.pallas_call(
        functools.partial(_s5_kernel, nb=nb, ts=ts, ns=ns),
        grid=(s // ts,),
        in_specs=[tok, tok, full(bmat), full(a_rows), full(cmat), full(d_row), full(glu_w), full(glu_b)],
        out_specs=tok,
        out_shape=jax.ShapeDtypeStruct((nb, s, w), BF16),
        scratch_shapes=[pltpu.VMEM((2 * ns // LANES, ts * nb, LANES), F32), pltpu.VMEM((nb, 2 * ns), F32)],
        compiler_params=pltpu.CompilerParams(dimension_semantics=("arbitrary",), vmem_limit_bytes=VMEM_LIMIT),
        name="s5",
    )(u, z, bmat, a_rows, cmat, d_row, glu_w, glu_b)


def _merge_kernel(x_ref, ng_ref, o0_ref, o1_ref, o2_ref, o3_ref, wm_ref, wb_ref, wo_ref, out_ref):
    x = x_ref[...]
    ms = jnp.mean(x * x, axis=-1, keepdims=True)
    h = (x * lax.rsqrt(ms + NORM_EPS) * ng_ref[...]).astype(BF16)
    d = x.shape[1]
    mixed = None
    for m, o_ref in enumerate((o0_ref, o1_ref, o2_ref, o3_ref)):
        gate = _sigmoid(_dot(h, wm_ref[:, m * d:(m + 1) * d]))
        term = gate * _dot(o_ref[...], wb_ref[m])
        mixed = term if mixed is None else mixed + term
    out_ref[...] = x + _dot(mixed.astype(BF16), wo_ref[...])


def _merge(x2, ng, outs, wm, wb, wo):
    t, d = x2.shape
    tm = min(TM_PROJ, t)
    tok = lambda w: pl.BlockSpec((tm, w), lambda i: (i, 0))
    full = lambda a: pl.BlockSpec(a.shape, lambda i: (0,) * a.ndim)
    return pl.pallas_call(
        _merge_kernel,
        grid=(t // tm,),
        in_specs=[tok(d), full(ng)] + [tok(WIDTH)] * 4 + [full(wm), full(wb), full(wo)],
        out_specs=tok(d),
        out_shape=jax.ShapeDtypeStruct((t, d), F32),
        compiler_params=pltpu.CompilerParams(dimension_semantics=("arbitrary",), vmem_limit_bytes=VMEM_LIMIT),
        name="merge",
    )(x2, ng, *outs, wm, wb, wo)


def _wprep_kernel(w_ref, wqk_ref, wv_ref, wc_ref, wg_ref, wz_ref, wsc_ref, wsb_ref, ws5_ref, wm_ref, *, d):
    w = WIDTH
    o_q, o_kv, o_gate = 0, w, 4 * w
    o_nz = o_gate + 3 * N_HEADS
    o_sc, o_scz, o_sb, o_sbz, o_s5, o_s5z = o_nz + w, o_nz + 4 * w, o_nz + 5 * w, o_nz + 8 * w, o_nz + 9 * w, o_nz + 10 * w
    o_merge = o_nz + 11 * w
    col = lambda a, n: w_ref[0, :, a:a + n]
    kv = lambda i: o_kv + i * (w // 2)
    cat = lambda parts: jnp.concatenate(parts, axis=1).astype(BF16)

    def rep_heads(a):
        h0, h1 = col(a, HEAD_DIM), col(a + HEAD_DIM, HEAD_DIM)
        return [h0, h0, h1, h1]

    def spread_heads(a):
        h0, h1 = col(a, HEAD_DIM), col(a + HEAD_DIM, HEAD_DIM)
        return [h0, jnp.zeros_like(h0), h1, jnp.zeros_like(h0)]

    wqk_ref[...] = cat([col(o_q, w)] + rep_heads(kv(2)) + rep_heads(kv(4)))
    wv_ref[...] = cat(spread_heads(kv(3)) + spread_heads(kv(5)))
    wc_ref[...] = col(kv(0), w).astype(BF16)
    gates = col(o_gate, 3 * N_HEADS)
    wg_ref[...] = cat([jnp.broadcast_to(gates[:, c:c + 1], (gates.shape[0], HEAD_DIM)) for c in range(3 * N_HEADS)])
    wz_ref[...] = cat([col(o_nz, w), col(o_scz, w), col(o_sbz, w), col(o_s5z, w)])
    wsc_ref[...] = col(o_sc, 3 * w).astype(BF16)
    wsb_ref[...] = col(o_sb, 3 * w).astype(BF16)
    ws5_ref[...] = col(o_s5, w).astype(BF16)
    wm_ref[...] = col(o_merge, N_HEADS * d).astype(BF16)


def _wprep(w_in, layer):
    _, d, n_in = w_in.shape
    tr = min(TR_WPREP, d)
    widths = [3 * WIDTH, 2 * WIDTH, WIDTH, 3 * WIDTH, 4 * WIDTH, 3 * WIDTH, 3 * WIDTH, WIDTH, N_HEADS * d]
    return pl.pallas_call(
        functools.partial(_wprep_kernel, d=d),
        grid=(d // tr,),
        in_specs=[pl.BlockSpec((1, tr, n_in), lambda i: (layer, i, 0))],
        out_specs=[pl.BlockSpec((tr, n), lambda i: (i, 0)) for n in widths],
        out_shape=[jax.ShapeDtypeStruct((d, n), BF16) for n in widths],
        compiler_params=pltpu.CompilerParams(dimension_semantics=("arbitrary",), vmem_limit_bytes=VMEM_LIMIT),
        name="wprep",
    )(w_in)


def _block_diag(blocks):
    g, r, c = blocks.shape
    eye = jnp.eye(g, dtype=blocks.dtype)
    return (eye[:, None, :, None] * blocks[:, :, None, :]).reshape(g * r, g * c)


def _block_diag_pair(w1):
    eye = jnp.eye(NSA_KV_HEADS, dtype=w1.dtype)
    bd = w1[..., :, None, :, None, :] * eye[:, None, :, None]
    return bd.reshape(w1.shape[:-3] + (w1.shape[-3] * NSA_KV_HEADS * HEAD_DIM, NSA_KV_HEADS * HEAD_DIM))


def _layer(x, cos, sin, cos_c, sin_c, n_cmp, norm_g, w_in_all, layer, qk_g, cmp_pe, cmp_w1, cmp_w2, conv_w,
           a_re, a_im, log_dt, b_re, b_im, c_re, c_im, d_skip, glu_w, glu_b, w_branch, w_out):
    b, s, d = x.shape
    w = WIDTH
    *weights, w_merge = _wprep(w_in_all, layer)
    qkg = jnp.concatenate([jnp.tile(qk_g[0], N_HEADS), jnp.tile(qk_g[2], N_HEADS), jnp.tile(qk_g[3], N_HEADS)])[None, :]
    ng = norm_g[None, :]

    (q, ks, kw, vs, vw, gate, zn, zsb, zs5, sc_o, sbq, sbk, sbv, s5u, kc_in, vc_in) = _inproj(
        x, ng, weights, qkg, cos, sin, conv_w)

    half = NSA_CMP_STRIDE
    two = NSA_CMP_LEN // half
    pe = jnp.tile(cmp_pe.reshape(2, two, half, HEAD_DIM), (1, 1, 1, NSA_KV_HEADS))
    w1 = _block_diag_pair(cmp_w1.reshape(2, two, half, HEAD_DIM, HEAD_DIM))
    zeros = jnp.zeros_like(cmp_w2[0])
    w2k = jnp.concatenate([jnp.concatenate([cmp_w2[0], cmp_w2[0], zeros, zeros], axis=1),
                           jnp.concatenate([zeros, zeros, cmp_w2[0], cmp_w2[0]], axis=1)], axis=0)
    w2v = jnp.concatenate([jnp.concatenate([cmp_w2[1], zeros], axis=1),
                           jnp.concatenate([zeros, cmp_w2[1]], axis=1)], axis=0)
    kc, vct = _compress(kc_in, vc_in, pe, w1.astype(BF16), w2k.astype(BF16), w2v.T.astype(BF16),
                        jnp.tile(qk_g[1], N_HEADS)[None, :], cos_c, sin_c)

    ocmp, bias = _cmp_topk(q, kc, vct, n_cmp)
    nsa_o = _selwin(q, bias, ocmp, gate, zn, ks, vs, kw, vw)
    sb_o = _stickbrk(sbq, sbk, sbv, zsb)

    ab_re, ab_im, bb_re, bb_im = _s5_discretise(a_re, a_im, log_dt, b_re, b_im)
    bmat = jnp.concatenate([_block_diag(bb_re), _block_diag(bb_im)], axis=1).astype(BF16)
    cmat = jnp.concatenate([_block_diag(jnp.swapaxes(c_re, 1, 2)),
                            -_block_diag(jnp.swapaxes(c_im, 1, 2))], axis=0).astype(BF16)
    a_rows = jnp.stack([ab_re.reshape(-1), ab_im.reshape(-1)])
    s5_o = _s5(s5u, zs5, bmat, a_rows, cmat, d_skip.reshape(1, -1), glu_w.astype(BF16), glu_b[None, :])

    out = _merge(x.reshape(b * s, d), ng,
                 [o.reshape(b * s, w) for o in (nsa_o, sc_o, sb_o, s5_o)],
                 w_merge, w_branch.astype(BF16), w_out.astype(BF16))
    return out.reshape(b, s, d)


def kernel(x, positions, norm_g, w_in, nsa_qk_g, nsa_cmp_pe, nsa_cmp_w1, nsa_cmp_w2, sc_conv_w, s5_a_re, s5_a_im,
           s5_log_dt, s5_b_re, s5_b_im, s5_c_re, s5_c_im, s5_d, s5_glu_w, s5_glu_b, w_branch, w_out):
    b, s, _ = x.shape
    assert s % TK_SEL == 0 or s < TK_SB
    cos, sin = _rope_tables(positions.reshape(-1))
    cos, sin = cos.reshape(b, s, LANES), sin.reshape(b, s, LANES)
    nk = s // NSA_CMP_STRIDE
    n_cmp = (s - NSA_CMP_LEN) // NSA_CMP_STRIDE + 1
    pos_c = jnp.concatenate([positions[:, NSA_CMP_LEN - 1::NSA_CMP_STRIDE],
                             jnp.zeros((b, nk - n_cmp), positions.dtype)], axis=1)
    cos_c, sin_c = _rope_tables(pos_c.reshape(-1))
    cos_c, sin_c = cos_c.reshape(b, nk, LANES), sin_c.reshape(b, nk, LANES)
    for l in range(norm_g.shape[0]):
        x = _layer(x, cos, sin, cos_c, sin_c, n_cmp, norm_g[l], w_in, l, nsa_qk_g[l], nsa_cmp_pe[l], nsa_cmp_w1[l],
                   nsa_cmp_w2[l], sc_conv_w[l], s5_a_re[l], s5_a_im[l], s5_log_dt[l], s5_b_re[l], s5_b_im[l],
                   s5_c_re[l], s5_c_im[l], s5_d[l], s5_glu_w[l], s5_glu_b[l], w_branch[l], w_out[l])
    return x
```

```python
import functools
import math

import jax
import jax.numpy as jnp
from jax import lax
from jax.experimental import pallas as pl
from jax.experimental.pallas import tpu as pltpu

F32 = jnp.float32
BF16 = jnp.bfloat16

HEAD_DIM = 64
N_HEADS = 4
WIDTH = N_HEADS * HEAD_DIM
NSA_KV_HEADS = 2
NSA_CMP_LEN = 32
NSA_CMP_STRIDE = 16
NSA_SEL_LEN = 64
NSA_SEL_TOPN = 16
NSA_WINDOW = 512
NSA_FORCE_SCORE = 1.0e4
S5_GROUPS = 16
S5_GROUP_CH = 16
S5_STATE = 64
ROPE_THETA = 10000.0
NORM_EPS = 1e-6
QK_SCALE = HEAD_DIM ** -0.5

LANES = 128
SUBLANES = 8
NEG = -1.0e30
SEL_BIAS = -30000.0
SB_DEAD = -120.0
VMEM_LIMIT = 56 * 1024 * 1024

TM_PROJ = 512
TR_WPREP = 256
TT_CMP = 128
TQ_SB = 256
TK_SB = 256
TQ_SEL = 512
TK_SEL = 512
NSA_HEAD_ORDER = (0, 2, 1, 3)
TS_S5 = 128


def _dot(a, b):
    return jnp.dot(a, b, preferred_element_type=F32)


def _dot_nt(a, b):
    return lax.dot_general(a, b, (((1,), (1,)), ((), ())), preferred_element_type=F32)


def _silu(x):
    return x * (1.0 / (1.0 + jnp.exp(-x)))


def _sigmoid(x):
    return 1.0 / (1.0 + jnp.exp(-x))


def _iota(shape, dim):
    return lax.broadcasted_iota(jnp.int32, shape, dim)


def _group_mean_sq(x):
    outs = []
    lane = _iota((x.shape[0], LANES), 1)
    low = lane < HEAD_DIM
    for c in range(x.shape[1] // LANES):
        xc = x[:, c * LANES:(c + 1) * LANES]
        sq = xc * xc
        s_lo = jnp.sum(jnp.where(low, sq, 0.0), axis=-1, keepdims=True)
        s_hi = jnp.sum(jnp.where(low, 0.0, sq), axis=-1, keepdims=True)
        outs.append(jnp.where(low, s_lo, s_hi) * (1.0 / HEAD_DIM))
    return outs[0] if len(outs) == 1 else jnp.concatenate(outs, axis=1)


def _rot_half(y):
    w = y.shape[1]
    first = (_iota(y.shape, 1) & (HEAD_DIM // 2)) == 0
    return jnp.where(first, pltpu.roll(y, w - HEAD_DIM // 2, 1), pltpu.roll(y, HEAD_DIM // 2, 1))


def _norm_rope(x, gain, cos, sin_signed):
    reps = x.shape[1] // LANES
    y = x * lax.rsqrt(_group_mean_sq(x) + NORM_EPS) * gain
    c = cos if reps == 1 else jnp.concatenate([cos] * reps, axis=1)
    s = sin_signed if reps == 1 else jnp.concatenate([sin_signed] * reps, axis=1)
    return y * c + _rot_half(y) * s


def _stack_heads(q):
    head = _iota(q.shape, 1) // HEAD_DIM
    return jnp.concatenate([jnp.where(head == h, q, jnp.zeros_like(q)) for h in range(N_HEADS)], axis=0)


def _unstack_heads(o4, m):
    head = _iota((m, WIDTH), 1) // HEAD_DIM
    out = jnp.zeros((m, WIDTH), F32)
    for h in range(N_HEADS):
        out = jnp.where(head == h, o4[h * m:(h + 1) * m], out)
    return out


def _rope_table_kernel(pos_ref, freq_ref, sign_ref, cos_ref, sin_ref):
    ang = pos_ref[...].astype(F32) * freq_ref[...]
    cos_ref[...] = jnp.cos(ang)
    sin_ref[...] = jnp.sin(ang) * sign_ref[...]


def _rope_tables(pos_flat):
    n = pos_flat.shape[0]
    half = HEAD_DIM // 2
    inv_freq = jnp.power(ROPE_THETA, -jnp.arange(half, dtype=F32) / half)
    freq = jnp.tile(inv_freq, LANES // half)[None, :]
    sign = jnp.tile(jnp.concatenate([-jnp.ones((half,), F32), jnp.ones((half,), F32)]), LANES // HEAD_DIM)[None, :]
    tm = 512 if n % 512 == 0 else n
    return pl.pallas_call(
        _rope_table_kernel,
        grid=(n // tm,),
        in_specs=[pl.BlockSpec((tm, 1), lambda i: (i, 0)),
                  pl.BlockSpec((1, LANES), lambda i: (0, 0)),
                  pl.BlockSpec((1, LANES), lambda i: (0, 0))],
        out_specs=[pl.BlockSpec((tm, LANES), lambda i: (i, 0))] * 2,
        out_shape=[jax.ShapeDtypeStruct((n, LANES), F32)] * 2,
        name="rope_tables",
    )(pos_flat[:, None], freq, sign)


def _inproj_kernel(x_ref, ng_ref, wqk_ref, wv_ref, wc_ref, wg_ref, wz_ref, wsc_ref, wsb_ref, ws5_ref,
                   qkg_ref, cos_ref, sin_ref, convw_ref,
                   q_ref, ks_ref, kw_ref, vs_ref, vw_ref, gate_ref, zn_ref, zsb_ref, zs5_ref,
                   sc_ref, sbq_ref, sbk_ref, sbv_ref, s5u_ref, kc_ref, vc_ref, ubuf_ref, *, tm):
    x = x_ref[0]
    ms = jnp.mean(x * x, axis=-1, keepdims=True)
    h = (x * lax.rsqrt(ms + NORM_EPS) * ng_ref[...]).astype(BF16)

    qk = _norm_rope(_dot(h, wqk_ref[...]), qkg_ref[...], cos_ref[0], sin_ref[0])
    q_ref[0] = (qk[:, :WIDTH] * QK_SCALE).astype(BF16)
    ks_ref[0] = qk[:, WIDTH:WIDTH + LANES].astype(BF16)
    kw_ref[0] = qk[:, WIDTH + LANES:].astype(BF16)

    v = _dot(h, wv_ref[...])
    vs_ref[0] = v[:, :LANES].astype(BF16)
    vw_ref[0] = v[:, LANES:].astype(BF16)
    kvc = _dot(h, wc_ref[...])
    kc_ref[0] = kvc[:, :LANES]
    vc_ref[0] = kvc[:, LANES:]
    gate_ref[0] = _sigmoid(_dot(h, wg_ref[...]))

    z = _silu(_dot(h, wz_ref[...]))
    zn_ref[0] = z[:, :WIDTH]
    zsb_ref[0] = z[:, 2 * WIDTH:3 * WIDTH]
    zs5_ref[0] = z[:, 3 * WIDTH:]

    bcx = _dot(h, wsc_ref[...])
    u = bcx[:, WIDTH:2 * WIDTH] * bcx[:, 2 * WIDTH:]

    @pl.when(pl.program_id(1) == 0)
    def _():
        ubuf_ref[0:SUBLANES, :] = jnp.zeros((SUBLANES, WIDTH), F32)

    ubuf_ref[SUBLANES:SUBLANES + tm, :] = u
    u1 = ubuf_ref[SUBLANES - 1:SUBLANES - 1 + tm, :]
    u2 = ubuf_ref[SUBLANES - 2:SUBLANES - 2 + tm, :]
    cw = convw_ref[...]
    y = cw[2:3, :] * u + cw[1:2, :] * u1 + cw[0:1, :] * u2
    ubuf_ref[0:SUBLANES, :] = ubuf_ref[tm:tm + SUBLANES, :]
    sc_ref[0] = (bcx[:, :WIDTH] * y * z[:, WIDTH:2 * WIDTH]).astype(BF16)

    sb = _dot(h, wsb_ref[...])
    sbq_ref[0] = (sb[:, :WIDTH] * QK_SCALE).astype(BF16)
    sbk_ref[0] = sb[:, WIDTH:2 * WIDTH].astype(BF16)
    sbv_ref[0] = sb[:, 2 * WIDTH:].astype(BF16)
    s5u_ref[0] = _dot(h, ws5_ref[...])


def _inproj(x, ng, weights, qkg, cos, sin, convw):
    b, s, d = x.shape
    tm = min(TM_PROJ, s)
    full2 = lambda a: pl.BlockSpec(a.shape, lambda bi, si: (0, 0))
    tok = lambda w: pl.BlockSpec((1, tm, w), lambda bi, si: (bi, si, 0))
    out_widths = [(WIDTH, BF16)] + [(LANES, BF16)] * 4 + [(3 * WIDTH, F32)] + [(WIDTH, F32)] * 3 \
        + [(WIDTH, BF16)] * 4 + [(WIDTH, F32)] + [(LANES, F32)] * 2
    return pl.pallas_call(
        functools.partial(_inproj_kernel, tm=tm),
        grid=(b, s // tm),
        in_specs=[tok(d), full2(ng)] + [full2(w) for w in weights] + [full2(qkg), tok(LANES), tok(LANES), full2(convw)],
        out_specs=[tok(w) for w, _ in out_widths],
        out_shape=[jax.ShapeDtypeStruct((b, s, w), dt) for w, dt in out_widths],
        scratch_shapes=[pltpu.VMEM((tm + 2 * SUBLANES, WIDTH), F32)],
        compiler_params=pltpu.CompilerParams(dimension_semantics=("arbitrary", "arbitrary"),
                                             vmem_limit_bytes=VMEM_LIMIT),
        name="inproj",
    )(x, ng, *weights, qkg, cos, sin, convw)


def _compress_kernel(kc_ref, vc_ref, pe_ref, w1_ref, w2k_ref, w2v_ref, g_ref, cos_ref, sin_ref, kco_ref, vct_ref, *, nk):
    half = NSA_CMP_STRIDE
    for j, src in ((0, kc_ref), (1, vc_ref)):
        tok = [src[0, pl.ds(i, nk, stride=half), :] for i in range(half)]
        lo = jnp.concatenate([(tok[i] + pe_ref[j, 0, i:i + 1, :]).astype(BF16) for i in range(half)], axis=1)
        hi = jnp.concatenate([(tok[i] + pe_ref[j, 1, i:i + 1, :]).astype(BF16) for i in range(half)], axis=1)
        hid = _dot(lo, w1_ref[j, 0]) + pltpu.roll(_dot(hi, w1_ref[j, 1]), nk - 1, 0)
        act = _silu(hid).astype(BF16)
        if j == 0:
            kco_ref[0] = _norm_rope(_dot(act, w2k_ref[...]), g_ref[...], cos_ref[0], sin_ref[0]).astype(BF16)
        else:
            vct_ref[0] = _dot_nt(w2v_ref[...], act).astype(BF16)


def _compress(kc, vc, pe, w1, w2k, w2v, g, cos, sin):
    b, s, _ = kc.shape
    nk = s // NSA_CMP_STRIDE
    full = lambda a: pl.BlockSpec(a.shape, lambda bi: (0,) * a.ndim)
    seq = pl.BlockSpec((1, s, LANES), lambda bi: (bi, 0, 0))
    tab = pl.BlockSpec((1, nk, LANES), lambda bi: (bi, 0, 0))
    return pl.pallas_call(
        functools.partial(_compress_kernel, nk=nk),
        grid=(b,),
        in_specs=[seq, seq, full(pe), full(w1), full(w2k), full(w2v), full(g), tab, tab],
        out_specs=[pl.BlockSpec((1, nk, LANES), lambda bi: (bi, 0, 0)),
                   pl.BlockSpec((1, LANES, nk), lambda bi: (bi, 0, 0))],
        out_shape=[jax.ShapeDtypeStruct((b, nk, LANES), BF16), jax.ShapeDtypeStruct((b, LANES, nk), BF16)],
        compiler_params=pltpu.CompilerParams(dimension_semantics=("arbitrary",), vmem_limit_bytes=VMEM_LIMIT),
        name="compress",
    )(kc, vc, pe, w1, w2k, w2v, g, cos, sin)


def _cmp_topk_kernel(q_ref, kc_ref, vct_ref, ocmp_ref, bias_ref, p_scr, imp_scr, rank_scr,
                     *, tt, nk, n_cmp, n_sel, top_n):
    t0 = pl.program_id(1) * tt
    kc = kc_ref[0]
    half = _iota((tt, LANES), 1) // HEAD_DIM
    nrow = _iota((nk, tt), 0)
    tcol = t0 + _iota((nk, tt), 1)
    valid_t = (nrow * NSA_CMP_STRIDE + (NSA_CMP_LEN - 1) <= tcol) & (nrow < n_cmp)
    brow = _iota((n_sel, tt), 0)
    tsel = t0 + _iota((n_sel, tt), 1)
    cur = tsel // NSA_SEL_LEN
    forced = (brow == 0) | (brow == cur) | (brow == cur - 1)
    in_past = brow * NSA_SEL_LEN <= tsel
    last_block = (t0 + tt - 1) // NSA_SEL_LEN
    sub = _iota((SUBLANES, tt), 0)
    heads_per_kv = N_HEADS // NSA_KV_HEADS
    scores = []
    for h in range(N_HEADS):
        qc = q_ref[0, :, (h % 2) * LANES:(h % 2 + 1) * LANES]
        scores.append(_dot_nt(kc, jnp.where(half == h // heads_per_kv, qc, jnp.zeros_like(qc))))
    probs = []
    for st in scores:
        st = jnp.where(valid_t, st, NEG)
        et = jnp.where(valid_t, jnp.exp(st - jnp.max(st, axis=0, keepdims=True)), 0.0)
        probs.append(et * (1.0 / jnp.maximum(jnp.sum(et, axis=0, keepdims=True), 1e-30)))
    outs = [_dot(vct_ref[0, (h // heads_per_kv) * HEAD_DIM:(h // heads_per_kv + 1) * HEAD_DIM, :],
                 probs[h].astype(BF16)) for h in range(N_HEADS)]
    groups = n_sel // SUBLANES
    imp_g = []
    for kh in range(NSA_KV_HEADS):
        psum = probs[kh * heads_per_kv]
        for g in range(1, heads_per_kv):
            psum = psum + probs[kh * heads_per_kv + g]
        p_scr[kh, SUBLANES:SUBLANES + nk, :] = psum
        p_scr[kh, 0:SUBLANES, :] = jnp.zeros((SUBLANES, tt), F32)
        ratio = NSA_SEL_LEN // NSA_CMP_STRIDE
        imp = p_scr[kh, pl.ds(SUBLANES - 1, n_sel, stride=ratio), :]
        for k in range(1, NSA_CMP_LEN // NSA_CMP_STRIDE + ratio - 1):
            imp = imp + p_scr[kh, pl.ds(SUBLANES - 1 + k, n_sel, stride=ratio), :]
        imp = jnp.where(forced, NSA_FORCE_SCORE, jnp.where(in_past, imp, -NSA_FORCE_SCORE))
        imp_scr[kh] = imp
        imp_g.append([imp[k * SUBLANES:(k + 1) * SUBLANES] for k in range(groups)])

    rank_scr[...] = jnp.zeros((NSA_KV_HEADS, n_sel, tt), F32)
    for m in range(groups):
        @pl.when(m * SUBLANES <= last_block)
        def _(m=m):
            for kh in range(NSA_KV_HEADS):
                for k in range(m + 1):
                    mine = imp_g[kh][k]
                    cnt = jnp.zeros((SUBLANES, tt), F32)
                    for c in ([m] if k < m else range(m + 1)):
                        for r in range(SUBLANES):
                            row = imp_scr[kh, c * SUBLANES + r:c * SUBLANES + r + 1, :]
                            if k > c:
                                beats = row >= mine
                            elif k < c:
                                beats = row > mine
                            else:
                                beats = (row > mine) | ((row == mine) & (sub > r))
                            cnt = cnt + jnp.where(beats, 1.0, 0.0)
                    rank_scr[kh, k * SUBLANES:(k + 1) * SUBLANES, :] += cnt

    biases = [jnp.where(rank_scr[kh] < top_n, 0.0, SEL_BIAS) for kh in range(NSA_KV_HEADS)]
    ocmp_ref[0] = jnp.concatenate([outs[h] for h in NSA_HEAD_ORDER], axis=0).T
    pad = jnp.zeros((LANES - n_sel, tt), F32)
    bias_ref[0] = jnp.concatenate([biases[0], pad, biases[1], pad], axis=0).T.astype(BF16)


def _cmp_topk(q, kc, vct, n_cmp):
    b, s, _ = q.shape
    nk = kc.shape[1]
    n_sel = s // NSA_SEL_LEN
    tt = min(TT_CMP, s)
    kern = functools.partial(_cmp_topk_kernel, tt=tt, nk=nk, n_cmp=n_cmp, n_sel=n_sel, top_n=min(NSA_SEL_TOPN, n_sel))
    return pl.pallas_call(
        kern,
        grid=(b, s // tt),
        in_specs=[pl.BlockSpec((1, tt, WIDTH), lambda bi, ti: (bi, ti, 0)),
                  pl.BlockSpec((1, nk, LANES), lambda bi, ti: (bi, 0, 0)),
                  pl.BlockSpec((1, LANES, nk), lambda bi, ti: (bi, 0, 0))],
        out_specs=[pl.BlockSpec((1, tt, WIDTH), lambda bi, ti: (bi, ti, 0))] * 2,
        out_shape=[jax.ShapeDtypeStruct((b, s, WIDTH), F32), jax.ShapeDtypeStruct((b, s, WIDTH), BF16)],
        scratch_shapes=[pltpu.VMEM((NSA_KV_HEADS, nk + SUBLANES, tt), F32), pltpu.VMEM((NSA_KV_HEADS, n_sel, tt), F32),
                        pltpu.VMEM((NSA_KV_HEADS, n_sel, tt), F32)],
        compiler_params=pltpu.CompilerParams(dimension_semantics=("arbitrary", "arbitrary"),
                                             vmem_limit_bytes=VMEM_LIMIT),
        name="cmp_topk",
    )(q, kc, vct)


CAUSAL, WINDOW = 0, 1


def _selwin_kernel(q_ref, bias_ref, ocmp_ref, gate_ref, zn_ref, ks_ref, vs_ref, kw_ref, vw_ref, e_ref, cm_ref, out_ref,
                   acc_scr, m_scr, *, tq, tk):
    t0 = pl.program_id(1) * tq
    jd = t0 // tk
    heads_per_kv = N_HEADS // NSA_KV_HEADS
    half = _iota((tq, LANES), 1) // HEAD_DIM
    q_win, q_sel = [], []
    for h in range(N_HEADS):
        qc = q_ref[0, :, (h % 2) * LANES:(h % 2 + 1) * LANES]
        qh = jnp.where(half == h // heads_per_kv, qc, jnp.zeros_like(qc))
        q_win.append(qh)
        q_sel.append(jnp.concatenate([qh, bias_ref[0, :, (h // heads_per_kv) * LANES:(h // heads_per_kv + 1) * LANES]],
                                     axis=1))
    ones = jnp.ones((tk, LANES), BF16)
    lanes = lambda x, n: jnp.concatenate([x] * (n // LANES), axis=1)

    def tile(i, k_ref, v_ref, selected, masking, first):
        k0 = pl.multiple_of((jd - i) * tk, tk)
        kt = k_ref[0, pl.ds(k0, tk), :]
        if selected:
            kt = jnp.concatenate([kt, e_ref[pl.ds(k0, tk), :]], axis=1)
        scores = [_dot_nt((q_sel if selected else q_win)[h], kt) for h in range(N_HEADS)]
        vt = jnp.concatenate([v_ref[0, pl.ds(k0, tk), :], ones], axis=1)
        for h, s in enumerate(scores):
            blk = slice(h * tq, (h + 1) * tq)
            if masking is not None:
                s = s + cm_ref[masking]
            s_max = jnp.max(s, axis=-1, keepdims=True)
            if first:
                m_new = jnp.broadcast_to(s_max, (tq, LANES))
                p = jnp.exp(s - lanes(m_new, tk))
                acc_scr[blk, :] = _dot(p.astype(BF16), vt)
            else:
                m_old = m_scr[blk, :]
                m_new = jnp.maximum(m_old, s_max)
                alpha = jnp.exp(m_old - m_new)
                p = jnp.exp(s - lanes(m_new, tk))
                acc_scr[blk, :] = lanes(alpha, WIDTH) * acc_scr[blk, :] + _dot(p.astype(BF16), vt)
            m_scr[blk, :] = m_new

    def result():
        res = [acc_scr[h * tq:(h + 1) * tq, :LANES] * (1.0 / acc_scr[h * tq:(h + 1) * tq, LANES:])
               for h in range(N_HEADS)]
        return jnp.concatenate([jnp.where(half == 0, res[c], res[c + heads_per_kv]) for c in range(2)], axis=1)

    tile(0, ks_ref, vs_ref, True, CAUSAL, True)

    def sel_body(i, carry):
        tile(i, ks_ref, vs_ref, True, None, False)
        return carry

    lax.fori_loop(1, jd + 1, sel_body, 0)
    o_slc = result()

    tile(0, kw_ref, vw_ref, False, CAUSAL, True)

    @pl.when(jd >= 1)
    def _():
        tile(1, kw_ref, vw_ref, False, WINDOW, False)

    o_win = result()
    gate = gate_ref[0]
    o = gate[:, :WIDTH] * ocmp_ref[0] + gate[:, WIDTH:2 * WIDTH] * o_slc + gate[:, 2 * WIDTH:] * o_win
    out_ref[0] = (o * zn_ref[0]).astype(BF16)


def _selwin(q, bias, ocmp, gate, zn, ks, vs, kw, vw):
    b, s, _ = q.shape
    tq, tk = min(TQ_SEL, s), min(TK_SEL, s)
    assert NSA_WINDOW == tk and tk == tq and s // NSA_SEL_LEN <= HEAD_DIM
    member = (jnp.arange(s)[:, None] // NSA_SEL_LEN == jnp.arange(LANES)[None, :]).astype(BF16)
    row, col = jnp.arange(tq)[:, None], jnp.arange(tk)[None, :]
    cmask = jnp.where(jnp.stack([col <= row, col > row]), 0.0, NEG).astype(F32)
    tok = lambda w: pl.BlockSpec((1, tq, w), lambda bi, qi: (bi, qi, 0))
    seq = pl.BlockSpec((1, s, LANES), lambda bi, qi: (bi, 0, 0))
    return pl.pallas_call(
        functools.partial(_selwin_kernel, tq=tq, tk=tk),
        grid=(b, s // tq),
        in_specs=[tok(WIDTH), tok(WIDTH), tok(WIDTH), tok(3 * WIDTH), tok(WIDTH), seq, seq, seq, seq,
                  pl.BlockSpec(member.shape, lambda bi, qi: (0, 0)),
                  pl.BlockSpec(cmask.shape, lambda bi, qi: (0, 0, 0))],
        out_specs=tok(WIDTH),
        out_shape=jax.ShapeDtypeStruct((b, s, WIDTH), BF16),
        scratch_shapes=[pltpu.VMEM((N_HEADS * tq, WIDTH), F32), pltpu.VMEM((N_HEADS * tq, LANES), F32)],
        compiler_params=pltpu.CompilerParams(dimension_semantics=("arbitrary", "arbitrary"),
                                             vmem_limit_bytes=VMEM_LIMIT),
        name="selwin",
    )(q, bias, ocmp, gate, zn, ks, vs, kw, vw, member, cmask)


def _stickbrk_kernel(q_ref, k_ref, v_ref, z_ref, mk_ref, out_ref, acc_scr, carry_scr, *, tq, tk):
    t0 = pl.program_id(1) * tq
    jd = t0 // tk
    qs = _stack_heads(q_ref[0])
    rows = N_HEADS * tq
    tri = jnp.where(_iota((tk, tk), 0) >= _iota((tk, tk), 1), 1.0, 0.0).astype(BF16)
    lanes = lambda x, n: jnp.concatenate([x] * (n // LANES), axis=1)

    def tile(i, diagonal):
        k0 = pl.multiple_of((jd - i) * tk, tk)
        z = _dot_nt(qs, k_ref[0, pl.ds(k0, tk), :])
        log1mb = -(jnp.maximum(z, 0.0) + jnp.log(1.0 + jnp.exp(-jnp.abs(z))))
        if diagonal:
            log1mb = log1mb * jnp.concatenate([mk_ref[0]] * N_HEADS, axis=0)
        hi = log1mb.astype(BF16)
        lo = (log1mb - hi.astype(F32)).astype(BF16)
        suffix = _dot(hi, tri) + _dot(lo, tri)
        tile_sum = jnp.broadcast_to(jnp.sum(log1mb, axis=-1, keepdims=True), (rows, LANES))
        if diagonal:
            w = jnp.exp(z + suffix + jnp.concatenate([mk_ref[1]] * N_HEADS, axis=0))
            acc_scr[...] = _dot(w.astype(BF16), v_ref[0, pl.ds(k0, tk), :])
            carry = tile_sum
        else:
            carry = carry_scr[...]
            w = jnp.exp(z + suffix + lanes(carry, tk))
            acc_scr[...] += _dot(w.astype(BF16), v_ref[0, pl.ds(k0, tk), :])
            carry = carry + tile_sum
        carry_scr[...] = carry
        return (jnp.max(carry) < SB_DEAD).astype(jnp.int32)

    def cond(c):
        i, dead = c
        return (i <= jd) & (dead == 0)

    def body(c):
        return c[0] + 1, tile(c[0], False)

    lax.while_loop(cond, body, (jnp.int32(1), tile(0, True)))
    out_ref[0] = (_unstack_heads(acc_scr[...], tq) * z_ref[0]).astype(BF16)


def _stickbrk(q, k, v, z):
    b, s, _ = q.shape
    tq, tk = min(TQ_SB, s), min(TK_SB, s)
    assert tq == tk
    keep = jnp.arange(tk)[None, :] < jnp.arange(tq)[:, None]
    masks = jnp.stack([jnp.where(keep, 1.0, 0.0), jnp.where(keep, 0.0, NEG)]).astype(F32)
    tok = pl.BlockSpec((1, tq, WIDTH), lambda bi, qi: (bi, qi, 0))
    seq = pl.BlockSpec((1, s, WIDTH), lambda bi, qi: (bi, 0, 0))
    return pl.pallas_call(
        functools.partial(_stickbrk_kernel, tq=tq, tk=tk),
        grid=(b, s // tq),
        in_specs=[tok, seq, seq, tok, pl.BlockSpec(masks.shape, lambda bi, qi: (0, 0, 0))],
        out_specs=tok,
        out_shape=jax.ShapeDtypeStruct((b, s, WIDTH), BF16),
        scratch_shapes=[pltpu.VMEM((N_HEADS * tq, WIDTH), F32), pltpu.VMEM((N_HEADS * tq, LANES), F32)],
        compiler_params=pltpu.CompilerParams(dimension_semantics=("arbitrary", "arbitrary"),
                                             vmem_limit_bytes=VMEM_LIMIT),
        name="stickbrk",
    )(q, k, v, z, masks)


def _s5_disc_kernel(are_ref, aim_ref, ldt_ref, bre_ref, bim_ref, abre_ref, abim_ref, bbre_ref, bbim_ref):
    dt = jnp.exp(ldt_ref[...])
    lr, li = are_ref[...], aim_ref[...]
    mag = jnp.exp(lr * dt)
    ab_re, ab_im = mag * jnp.cos(li * dt), mag * jnp.sin(li * dt)
    den = lr * lr + li * li
    coef_re = ((ab_re - 1.0) * lr + ab_im * li) / den
    coef_im = (ab_im * lr - (ab_re - 1.0) * li) / den
    abre_ref[...] = ab_re
    abim_ref[...] = ab_im
    br, bi = bre_ref[...], bim_ref[...]
    bbre_ref[...] = coef_re[:, None, :] * br - coef_im[:, None, :] * bi
    bbim_ref[...] = coef_re[:, None, :] * bi + coef_im[:, None, :] * br


def _s5_discretise(a_re, a_im, log_dt, b_re, b_im):
    g, p = a_re.shape
    brt, bit = jnp.swapaxes(b_re, 1, 2), jnp.swapaxes(b_im, 1, 2)
    return pl.pallas_call(
        _s5_disc_kernel,
        out_shape=[jax.ShapeDtypeStruct((g, p), F32)] * 2 + [jax.ShapeDtypeStruct(brt.shape, F32)] * 2,
        name="s5_disc",
    )(a_re, a_im, log_dt[:, None], brt, bit)


def _s5_kernel(u_ref, z_ref, bmat_ref, a_ref, cmat_ref, d_ref, gw_ref, gb_ref, out_ref, xs_scr, state_scr,
               *, nb, ts, ns):
    @pl.when(pl.program_id(0) == 0)
    def _():
        state_scr[...] = jnp.zeros((nb, 2 * ns), F32)

    nc = 2 * ns // LANES
    for b in range(nb):
        bu = _dot(u_ref[b].astype(BF16), bmat_ref[...])
        for c in range(nc):
            xs_scr[c, pl.ds(b, ts, stride=nb), :] = bu[:, c * LANES:(c + 1) * LANES]

    a_re = jnp.broadcast_to(a_ref[0:1, :], (nb, ns))
    a_im = jnp.broadcast_to(a_ref[1:2, :], (nb, ns))

    def step(t, state):
        x_re, x_im = state
        r0 = pl.multiple_of(t * nb, nb)
        bu = jnp.concatenate([xs_scr[c, pl.ds(r0, nb), :] for c in range(nc)], axis=1)
        n_re = a_re * x_re - a_im * x_im + bu[:, :ns]
        n_im = a_re * x_im + a_im * x_re + bu[:, ns:]
        for c in range(nc // 2):
            xs_scr[c, pl.ds(r0, nb), :] = n_re[:, c * LANES:(c + 1) * LANES]
            xs_scr[nc // 2 + c, pl.ds(r0, nb), :] = n_im[:, c * LANES:(c + 1) * LANES]
        return n_re, n_im

    st = state_scr[...]
    x_re, x_im = lax.fori_loop(0, ts, step, (st[:, :ns], st[:, ns:]))
    state_scr[...] = jnp.concatenate([x_re, x_im], axis=1)

    width = u_ref.shape[-1]
    ys = []
    for b in range(nb):
        xs = jnp.concatenate([xs_scr[c, pl.ds(b, ts, stride=nb), :] for c in range(nc)], axis=1)
        ys.append((_dot(xs.astype(BF16), cmat_ref[...]) + d_ref[...] * u_ref[b]).astype(BF16))
    for b in range(nb):
        glu = _dot(ys[b], gw_ref[...]) + gb_ref[...]
        out_ref[b] = (glu[:, :width] * _sigmoid(glu[:, width:]) * z_ref[b]).astype(BF16)


def _s5(u, z, bmat, a_rows, cmat, d_row, glu_w, glu_b):
    nb, s, w = u.shape
    ts = min(TS_S5, s)
    ns = a_rows.shape[1]
    full = lambda a: pl.BlockSpec(a.shape, lambda i: (0, 0))
    tok = pl.BlockSpec((nb, ts, w), lambda i: (0, i, 0))
    return pl.pallas_call(
        functools.partial(_s5_kernel, nb=nb, ts=ts, ns=ns),
        grid=(s // ts,),
        in_specs=[tok, tok, full(bmat), full(a_rows), full(cmat), full(d_row), full(glu_w), full(glu_b)],
        out_specs=tok,
        out_shape=jax.ShapeDtypeStruct((nb, s, w), BF16),
        scratch_shapes=[pltpu.VMEM((2 * ns // LANES, ts * nb, LANES), F32), pltpu.VMEM((nb, 2 * ns), F32)],
        compiler_params=pltpu.CompilerParams(dimension_semantics=("arbitrary",), vmem_limit_bytes=VMEM_LIMIT),
        name="s5",
    )(u, z, bmat, a_rows, cmat, d_row, glu_w, glu_b)


def _merge_kernel(x_ref, ng_ref, o0_ref, o1_ref, o2_ref, o3_ref, wm_ref, wb_ref, wo_ref, out_ref):
    x = x_ref[...]
    ms = jnp.mean(x * x, axis=-1, keepdims=True)
    h = (x * lax.rsqrt(ms + NORM_EPS) * ng_ref[...]).astype(BF16)
    d = x.shape[1]
    mixed = None
    for m, o_ref in enumerate((o0_ref, o1_ref, o2_ref, o3_ref)):
        gate = _sigmoid(_dot(h, wm_ref[:, m * d:(m + 1) * d]))
        term = gate * _dot(o_ref[...], wb_ref[m])
        mixed = term if mixed is None else mixed + term
    out_ref[...] = x + _dot(mixed.astype(BF16), wo_ref[...])


def _merge(x2, ng, outs, wm, wb, wo):
    t, d = x2.shape
    tm = min(TM_PROJ, t)
    tok = lambda w: pl.BlockSpec((tm, w), lambda i: (i, 0))
    full = lambda a: pl.BlockSpec(a.shape, lambda i: (0,) * a.ndim)
    return pl.pallas_call(
        _merge_kernel,
        grid=(t // tm,),
        in_specs=[tok(d), full(ng)] + [tok(WIDTH)] * 4 + [full(wm), full(wb), full(wo)],
        out_specs=tok(d),
        out_shape=jax.ShapeDtypeStruct((t, d), F32),
        compiler_params=pltpu.CompilerParams(dimension_semantics=("arbitrary",), vmem_limit_bytes=VMEM_LIMIT),
        name="merge",
    )(x2, ng, *outs, wm, wb, wo)


def _wprep_kernel(w_ref, wqk_ref, wv_ref, wc_ref, wg_ref, wz_ref, wsc_ref, wsb_ref, ws5_ref, wm_ref, *, d):
    w = WIDTH
    o_q, o_kv, o_gate = 0, w, 4 * w
    o_nz = o_gate + 3 * N_HEADS
    o_sc, o_scz, o_sb, o_sbz, o_s5, o_s5z = o_nz + w, o_nz + 4 * w, o_nz + 5 * w, o_nz + 8 * w, o_nz + 9 * w, o_nz + 10 * w
    o_merge = o_nz + 11 * w
    col = lambda a, n: w_ref[0, :, a:a + n]
    kv = lambda i: o_kv + i * (w // 2)
    cat = lambda parts: jnp.concatenate(parts, axis=1).astype(BF16)

    def heads(a):
        return [col(a + h * HEAD_DIM, HEAD_DIM) for h in NSA_HEAD_ORDER]

    wqk_ref[...] = cat(heads(o_q) + [col(kv(2), w // 2), col(kv(4), w // 2)])
    wv_ref[...] = cat([col(kv(3), w // 2), col(kv(5), w // 2)])
    wc_ref[...] = col(kv(0), w).astype(BF16)
    gates = col(o_gate, 3 * N_HEADS)
    wg_ref[...] = cat([jnp.broadcast_to(gates[:, c:c + 1], (gates.shape[0], HEAD_DIM))
                       for c in (br * N_HEADS + h for br in range(3) for h in NSA_HEAD_ORDER)])
    wz_ref[...] = cat(heads(o_nz) + [col(o_scz, w), col(o_sbz, w), col(o_s5z, w)])
    wsc_ref[...] = col(o_sc, 3 * w).astype(BF16)
    wsb_ref[...] = col(o_sb, 3 * w).astype(BF16)
    ws5_ref[...] = col(o_s5, w).astype(BF16)
    wm_ref[...] = col(o_merge, N_HEADS * d).astype(BF16)


def _wprep(w_in, layer):
    _, d, n_in = w_in.shape
    tr = min(TR_WPREP, d)
    widths = [2 * WIDTH, WIDTH, WIDTH, 3 * WIDTH, 4 * WIDTH, 3 * WIDTH, 3 * WIDTH, WIDTH, N_HEADS * d]
    return pl.pallas_call(
        functools.partial(_wprep_kernel, d=d),
        grid=(d // tr,),
        in_specs=[pl.BlockSpec((1, tr, n_in), lambda i: (layer, i, 0))],
        out_specs=[pl.BlockSpec((tr, n), lambda i: (i, 0)) for n in widths],
        out_shape=[jax.ShapeDtypeStruct((d, n), BF16) for n in widths],
        compiler_params=pltpu.CompilerParams(dimension_semantics=("arbitrary",), vmem_limit_bytes=VMEM_LIMIT),
        name="wprep",
    )(w_in)


def _block_diag(blocks):
    g, r, c = blocks.shape
    eye = jnp.eye(g, dtype=blocks.dtype)
    return (eye[:, None, :, None] * blocks[:, :, None, :]).reshape(g * r, g * c)


def _block_diag_pair(w1):
    eye = jnp.eye(NSA_KV_HEADS, dtype=w1.dtype)
    bd = w1[..., :, None, :, None, :] * eye[:, None, :, None]
    return bd.reshape(w1.shape[:-3] + (w1.shape[-3] * NSA_KV_HEADS * HEAD_DIM, NSA_KV_HEADS * HEAD_DIM))


def _layer(x, cos, sin, cos_c, sin_c, n_cmp, norm_g, w_in_all, layer, qk_g, cmp_pe, cmp_w1, cmp_w2, conv_w,
           a_re, a_im, log_dt, b_re, b_im, c_re, c_im, d_skip, glu_w, glu_b, w_branch, w_out):
    b, s, d = x.shape
    w = WIDTH
    *weights, w_merge = _wprep(w_in_all, layer)
    qkg = jnp.concatenate([jnp.tile(qk_g[0], N_HEADS), jnp.tile(qk_g[2], NSA_KV_HEADS),
                           jnp.tile(qk_g[3], NSA_KV_HEADS)])[None, :]
    ng = norm_g[None, :]

    (q, ks, kw, vs, vw, gate, zn, zsb, zs5, sc_o, sbq, sbk, sbv, s5u, kc_in, vc_in) = _inproj(
        x, ng, weights, qkg, cos, sin, conv_w)

    half = NSA_CMP_STRIDE
    two = NSA_CMP_LEN // half
    pe = jnp.tile(cmp_pe.reshape(2, two, half, HEAD_DIM), (1, 1, 1, NSA_KV_HEADS))
    w1 = _block_diag_pair(cmp_w1.reshape(2, two, half, HEAD_DIM, HEAD_DIM))
    w2k, w2v = _block_diag(jnp.stack([cmp_w2[0]] * NSA_KV_HEADS)), _block_diag(jnp.stack([cmp_w2[1]] * NSA_KV_HEADS))
    kc, vct = _compress(kc_in, vc_in, pe, w1.astype(BF16), w2k.astype(BF16), w2v.T.astype(BF16),
                        jnp.tile(qk_g[1], NSA_KV_HEADS)[None, :], cos_c, sin_c)

    ocmp, bias = _cmp_topk(q, kc, vct, n_cmp)
    nsa_o = _selwin(q, bias, ocmp, gate, zn, ks, vs, kw, vw)
    sb_o = _stickbrk(sbq, sbk, sbv, zsb)

    ab_re, ab_im, bb_re, bb_im = _s5_discretise(a_re, a_im, log_dt, b_re, b_im)
    bmat = jnp.concatenate([_block_diag(bb_re), _block_diag(bb_im)], axis=1).astype(BF16)
    cmat = jnp.concatenate([_block_diag(jnp.swapaxes(c_re, 1, 2)),
                            -_block_diag(jnp.swapaxes(c_im, 1, 2))], axis=0).astype(BF16)
    a_rows = jnp.stack([ab_re.reshape(-1), ab_im.reshape(-1)])
    s5_o = _s5(s5u, zs5, bmat, a_rows, cmat, d_skip.reshape(1, -1), glu_w.astype(BF16), glu_b[None, :])

    wb_nsa = w_branch[0].reshape(N_HEADS, HEAD_DIM, d)[jnp.array(NSA_HEAD_ORDER)].reshape(w, d)
    wb = jnp.concatenate([wb_nsa[None], w_branch[1:]], axis=0).astype(BF16)
    out = _merge(x.reshape(b * s, d), ng,
                 [o.reshape(b * s, w) for o in (nsa_o, sc_o, sb_o, s5_o)],
                 w_merge, wb, w_out.astype(BF16))
    return out.reshape(b, s, d)


def kernel(x, positions, norm_g, w_in, nsa_qk_g, nsa_cmp_pe, nsa_cmp_w1, nsa_cmp_w2, sc_conv_w, s5_a_re, s5_a_im,
           s5_log_dt, s5_b_re, s5_b_im, s5_c_re, s5_c_im, s5_d, s5_glu_w, s5_glu_b, w_branch, w_out):
    b, s, _ = x.shape
    assert s % TK_SEL == 0 or s < TK_SB
    cos, sin = _rope_tables(positions.reshape(-1))
    cos, sin = cos.reshape(b, s, LANES), sin.reshape(b, s, LANES)
    nk = s // NSA_CMP_STRIDE
    n_cmp = (s - NSA_CMP_LEN) // NSA_CMP_STRIDE + 1
    pos_c = jnp.concatenate([positions[:, NSA_CMP_LEN - 1::NSA_CMP_STRIDE],
                             jnp.zeros((b, nk - n_cmp), positions.dtype)], axis=1)
    cos_c, sin_c = _rope_tables(pos_c.reshape(-1))
    cos_c, sin_c = cos_c.reshape(b, nk, LANES), sin_c.reshape(b, nk, LANES)
    for l in range(norm_g.shape[0]):
        x = _layer(x, cos, sin, cos_c, sin_c, n_cmp, norm_g[l], w_in, l, nsa_qk_g[l], nsa_cmp_pe[l], nsa_cmp_w1[l],
                   nsa_cmp_w2[l], sc_conv_w[l], s5_a_re[l], s5_a_im[l], s5_log_dt[l], s5_b_re[l], s5_b_im[l],
                   s5_c_re[l], s5_c_im[l], s5_d[l], s5_glu_w[l], s5_glu_b[l], w_branch[l], w_out[l])
    return x
```

```python
import functools
import math

import jax
import jax.numpy as jnp
from jax import lax
from jax.experimental import pallas as pl
from jax.experimental.pallas import tpu as pltpu

F32 = jnp.float32
BF16 = jnp.bfloat16

HEAD_DIM = 64
N_HEADS = 4
WIDTH = N_HEADS * HEAD_DIM
NSA_KV_HEADS = 2
NSA_CMP_LEN = 32
NSA_CMP_STRIDE = 16
NSA_SEL_LEN = 64
NSA_SEL_TOPN = 16
NSA_WINDOW = 512
NSA_FORCE_SCORE = 1.0e4
S5_GROUPS = 16
S5_GROUP_CH = 16
S5_STATE = 64
ROPE_THETA = 10000.0
NORM_EPS = 1e-6
QK_SCALE = HEAD_DIM ** -0.5

LANES = 128
SUBLANES = 8
NEG = -1.0e30
SEL_BIAS = -30000.0
SB_DEAD = -120.0
VMEM_LIMIT = 56 * 1024 * 1024

TM_PROJ = 512
TR_WPREP = 256
TT_CMP = 128
TQ_SB = 256
TK_SB = 256
TQ_SEL = 512
TK_SEL = 512
NSA_HEAD_ORDER = (0, 2, 1, 3)
TS_S5 = 128


def _dot(a, b):
    return jnp.dot(a, b, preferred_element_type=F32)


def _dot_nt(a, b):
    return lax.dot_general(a, b, (((1,), (1,)), ((), ())), preferred_element_type=F32)


def _silu(x):
    return x * (1.0 / (1.0 + jnp.exp(-x)))


def _sigmoid(x):
    return 1.0 / (1.0 + jnp.exp(-x))


def _iota(shape, dim):
    return lax.broadcasted_iota(jnp.int32, shape, dim)


def _group_mean_sq(x):
    outs = []
    lane = _iota((x.shape[0], LANES), 1)
    low = lane < HEAD_DIM
    for c in range(x.shape[1] // LANES):
        xc = x[:, c * LANES:(c + 1) * LANES]
        sq = xc * xc
        s_lo = jnp.sum(jnp.where(low, sq, 0.0), axis=-1, keepdims=True)
        s_hi = jnp.sum(jnp.where(low, 0.0, sq), axis=-1, keepdims=True)
        outs.append(jnp.where(low, s_lo, s_hi) * (1.0 / HEAD_DIM))
    return outs[0] if len(outs) == 1 else jnp.concatenate(outs, axis=1)


def _rot_half(y):
    w = y.shape[1]
    first = (_iota(y.shape, 1) & (HEAD_DIM // 2)) == 0
    return jnp.where(first, pltpu.roll(y, w - HEAD_DIM // 2, 1), pltpu.roll(y, HEAD_DIM // 2, 1))


def _norm_rope(x, gain, cos, sin_signed):
    reps = x.shape[1] // LANES
    y = x * lax.rsqrt(_group_mean_sq(x) + NORM_EPS) * gain
    c = cos if reps == 1 else jnp.concatenate([cos] * reps, axis=1)
    s = sin_signed if reps == 1 else jnp.concatenate([sin_signed] * reps, axis=1)
    return y * c + _rot_half(y) * s


def _stack_heads(q):
    head = _iota(q.shape, 1) // HEAD_DIM
    return jnp.concatenate([jnp.where(head == h, q, jnp.zeros_like(q)) for h in range(N_HEADS)], axis=0)


def _unstack_heads(o4, m):
    head = _iota((m, WIDTH), 1) // HEAD_DIM
    out = jnp.zeros((m, WIDTH), F32)
    for h in range(N_HEADS):
        out = jnp.where(head == h, o4[h * m:(h + 1) * m], out)
    return out


def _rope_table_kernel(pos_ref, freq_ref, sign_ref, cos_ref, sin_ref):
    ang = pos_ref[...].astype(F32) * freq_ref[...]
    cos_ref[...] = jnp.cos(ang)
    sin_ref[...] = jnp.sin(ang) * sign_ref[...]


def _rope_tables(pos_flat):
    n = pos_flat.shape[0]
    half = HEAD_DIM // 2
    inv_freq = jnp.power(ROPE_THETA, -jnp.arange(half, dtype=F32) / half)
    freq = jnp.tile(inv_freq, LANES // half)[None, :]
    sign = jnp.tile(jnp.concatenate([-jnp.ones((half,), F32), jnp.ones((half,), F32)]), LANES // HEAD_DIM)[None, :]
    tm = 512 if n % 512 == 0 else n
    return pl.pallas_call(
        _rope_table_kernel,
        grid=(n // tm,),
        in_specs=[pl.BlockSpec((tm, 1), lambda i: (i, 0)),
                  pl.BlockSpec((1, LANES), lambda i: (0, 0)),
                  pl.BlockSpec((1, LANES), lambda i: (0, 0))],
        out_specs=[pl.BlockSpec((tm, LANES), lambda i: (i, 0))] * 2,
        out_shape=[jax.ShapeDtypeStruct((n, LANES), F32)] * 2,
        name="rope_tables",
    )(pos_flat[:, None], freq, sign)


def _inproj_kernel(x_ref, ng_ref, wqk_ref, wv_ref, wc_ref, wg_ref, wz_ref, wsc_ref, wsb_ref, ws5_ref,
                   qkg_ref, cos_ref, sin_ref, convw_ref,
                   q_ref, ks_ref, kw_ref, vs_ref, vw_ref, gate_ref, zn_ref, zsb_ref, zs5_ref,
                   sc_ref, sbq_ref, sbk_ref, sbv_ref, s5u_ref, kc_ref, vc_ref, ubuf_ref, *, tm):
    x = x_ref[0]
    ms = jnp.mean(x * x, axis=-1, keepdims=True)
    h = (x * lax.rsqrt(ms + NORM_EPS) * ng_ref[...]).astype(BF16)

    qk = _norm_rope(_dot(h, wqk_ref[...]), qkg_ref[...], cos_ref[0], sin_ref[0])
    q_ref[0] = (qk[:, :WIDTH] * QK_SCALE).astype(BF16)
    ks_ref[0] = qk[:, WIDTH:WIDTH + LANES].astype(BF16)
    kw_ref[0] = qk[:, WIDTH + LANES:].astype(BF16)

    v = _dot(h, wv_ref[...])
    vs_ref[0] = v[:, :LANES].astype(BF16)
    vw_ref[0] = v[:, LANES:].astype(BF16)
    kvc = _dot(h, wc_ref[...])
    kc_ref[0] = kvc[:, :LANES]
    vc_ref[0] = kvc[:, LANES:]
    gate_ref[0] = _sigmoid(_dot(h, wg_ref[...]))

    z = _silu(_dot(h, wz_ref[...]))
    zn_ref[0] = z[:, :WIDTH]
    zsb_ref[0] = z[:, 2 * WIDTH:3 * WIDTH]
    zs5_ref[0] = z[:, 3 * WIDTH:]

    bcx = _dot(h, wsc_ref[...])
    u = bcx[:, WIDTH:2 * WIDTH] * bcx[:, 2 * WIDTH:]

    @pl.when(pl.program_id(1) == 0)
    def _():
        ubuf_ref[0:SUBLANES, :] = jnp.zeros((SUBLANES, WIDTH), F32)

    ubuf_ref[SUBLANES:SUBLANES + tm, :] = u
    u1 = ubuf_ref[SUBLANES - 1:SUBLANES - 1 + tm, :]
    u2 = ubuf_ref[SUBLANES - 2:SUBLANES - 2 + tm, :]
    cw = convw_ref[...]
    y = cw[2:3, :] * u + cw[1:2, :] * u1 + cw[0:1, :] * u2
    ubuf_ref[0:SUBLANES, :] = ubuf_ref[tm:tm + SUBLANES, :]
    sc_ref[0] = (bcx[:, :WIDTH] * y * z[:, WIDTH:2 * WIDTH]).astype(BF16)

    sb = _dot(h, wsb_ref[...])
    sbq_ref[0] = (sb[:, :WIDTH] * QK_SCALE).astype(BF16)
    sbk_ref[0] = sb[:, WIDTH:2 * WIDTH].astype(BF16)
    sbv_ref[0] = sb[:, 2 * WIDTH:].astype(BF16)
    s5u_ref[0] = _dot(h, ws5_ref[...])


def _inproj(x, ng, weights, qkg, cos, sin, convw):
    b, s, d = x.shape
    tm = min(TM_PROJ, s)
    full2 = lambda a: pl.BlockSpec(a.shape, lambda bi, si: (0, 0))
    tok = lambda w: pl.BlockSpec((1, tm, w), lambda bi, si: (bi, si, 0))
    out_widths = [(WIDTH, BF16)] + [(LANES, BF16)] * 4 + [(3 * WIDTH, F32)] + [(WIDTH, F32)] * 3 \
        + [(WIDTH, BF16)] * 4 + [(WIDTH, F32)] + [(LANES, F32)] * 2
    return pl.pallas_call(
        functools.partial(_inproj_kernel, tm=tm),
        grid=(b, s // tm),
        in_specs=[tok(d), full2(ng)] + [full2(w) for w in weights] + [full2(qkg), tok(LANES), tok(LANES), full2(convw)],
        out_specs=[tok(w) for w, _ in out_widths],
        out_shape=[jax.ShapeDtypeStruct((b, s, w), dt) for w, dt in out_widths],
        scratch_shapes=[pltpu.VMEM((tm + 2 * SUBLANES, WIDTH), F32)],
        compiler_params=pltpu.CompilerParams(dimension_semantics=("arbitrary", "arbitrary"),
                                             vmem_limit_bytes=VMEM_LIMIT),
        name="inproj",
    )(x, ng, *weights, qkg, cos, sin, convw)


def _compress_kernel(kc_ref, vc_ref, pe_ref, w1_ref, w2k_ref, w2v_ref, g_ref, cos_ref, sin_ref, kco_ref, vct_ref, *, nk):
    half = NSA_CMP_STRIDE
    for j, src in ((0, kc_ref), (1, vc_ref)):
        tok = [src[0, pl.ds(i, nk, stride=half), :] for i in range(half)]
        lo = jnp.concatenate([(tok[i] + pe_ref[j, 0, i:i + 1, :]).astype(BF16) for i in range(half)], axis=1)
        hi = jnp.concatenate([(tok[i] + pe_ref[j, 1, i:i + 1, :]).astype(BF16) for i in range(half)], axis=1)
        hid = _dot(lo, w1_ref[j, 0]) + pltpu.roll(_dot(hi, w1_ref[j, 1]), nk - 1, 0)
        act = _silu(hid).astype(BF16)
        if j == 0:
            kco_ref[0] = _norm_rope(_dot(act, w2k_ref[...]), g_ref[...], cos_ref[0], sin_ref[0]).astype(BF16)
        else:
            vct_ref[0] = _dot_nt(w2v_ref[...], act).astype(BF16)


def _compress(kc, vc, pe, w1, w2k, w2v, g, cos, sin):
    b, s, _ = kc.shape
    nk = s // NSA_CMP_STRIDE
    full = lambda a: pl.BlockSpec(a.shape, lambda bi: (0,) * a.ndim)
    seq = pl.BlockSpec((1, s, LANES), lambda bi: (bi, 0, 0))
    tab = pl.BlockSpec((1, nk, LANES), lambda bi: (bi, 0, 0))
    return pl.pallas_call(
        functools.partial(_compress_kernel, nk=nk),
        grid=(b,),
        in_specs=[seq, seq, full(pe), full(w1), full(w2k), full(w2v), full(g), tab, tab],
        out_specs=[pl.BlockSpec((1, nk, LANES), lambda bi: (bi, 0, 0)),
                   pl.BlockSpec((1, LANES, nk), lambda bi: (bi, 0, 0))],
        out_shape=[jax.ShapeDtypeStruct((b, nk, LANES), BF16), jax.ShapeDtypeStruct((b, LANES, nk), BF16)],
        compiler_params=pltpu.CompilerParams(dimension_semantics=("arbitrary",), vmem_limit_bytes=VMEM_LIMIT),
        name="compress",
    )(kc, vc, pe, w1, w2k, w2v, g, cos, sin)


def _cmp_topk_kernel(q_ref, kc_ref, vct_ref, ocmp_ref, bias_ref, p_scr, imp_scr, rank_scr,
                     *, tt, nk, n_cmp, n_sel, top_n):
    t0 = pl.program_id(1) * tt
    kc = kc_ref[0]
    half = _iota((tt, LANES), 1) // HEAD_DIM
    nrow = _iota((nk, tt), 0)
    tcol = t0 + _iota((nk, tt), 1)
    valid_t = (nrow * NSA_CMP_STRIDE + (NSA_CMP_LEN - 1) <= tcol) & (nrow < n_cmp)
    brow = _iota((n_sel, tt), 0)
    tsel = t0 + _iota((n_sel, tt), 1)
    cur = tsel // NSA_SEL_LEN
    forced = (brow == 0) | (brow == cur) | (brow == cur - 1)
    in_past = brow * NSA_SEL_LEN <= tsel
    last_block = (t0 + tt - 1) // NSA_SEL_LEN
    sub = _iota((SUBLANES, tt), 0)
    heads_per_kv = N_HEADS // NSA_KV_HEADS
    scores = []
    for h in range(N_HEADS):
        qc = q_ref[0, :, (h % 2) * LANES:(h % 2 + 1) * LANES]
        scores.append(_dot_nt(kc, jnp.where(half == h // heads_per_kv, qc, jnp.zeros_like(qc))))
    probs = []
    for st in scores:
        st = jnp.where(valid_t, st, NEG)
        et = jnp.where(valid_t, jnp.exp(st - jnp.max(st, axis=0, keepdims=True)), 0.0)
        probs.append(et * (1.0 / jnp.maximum(jnp.sum(et, axis=0, keepdims=True), 1e-30)))
    outs = [_dot(vct_ref[0, (h // heads_per_kv) * HEAD_DIM:(h // heads_per_kv + 1) * HEAD_DIM, :],
                 probs[h].astype(BF16)) for h in range(N_HEADS)]
    groups = n_sel // SUBLANES
    imp_g = []
    for kh in range(NSA_KV_HEADS):
        psum = probs[kh * heads_per_kv]
        for g in range(1, heads_per_kv):
            psum = psum + probs[kh * heads_per_kv + g]
        p_scr[kh, SUBLANES:SUBLANES + nk, :] = psum
        p_scr[kh, 0:SUBLANES, :] = jnp.zeros((SUBLANES, tt), F32)
        ratio = NSA_SEL_LEN // NSA_CMP_STRIDE
        imp = p_scr[kh, pl.ds(SUBLANES - 1, n_sel, stride=ratio), :]
        for k in range(1, NSA_CMP_LEN // NSA_CMP_STRIDE + ratio - 1):
            imp = imp + p_scr[kh, pl.ds(SUBLANES - 1 + k, n_sel, stride=ratio), :]
        imp = jnp.where(forced, NSA_FORCE_SCORE, jnp.where(in_past, imp, -NSA_FORCE_SCORE))
        imp_scr[kh] = imp
        imp_g.append([imp[k * SUBLANES:(k + 1) * SUBLANES] for k in range(groups)])

    rank_scr[...] = jnp.zeros((NSA_KV_HEADS, n_sel, tt), F32)
    for m in range(groups):
        @pl.when(m * SUBLANES <= last_block)
        def _(m=m):
            for kh in range(NSA_KV_HEADS):
                for k in range(m + 1):
                    mine = imp_g[kh][k]
                    cnt = jnp.zeros((SUBLANES, tt), F32)
                    for c in ([m] if k < m else range(m + 1)):
                        for r in range(SUBLANES):
                            row = imp_scr[kh, c * SUBLANES + r:c * SUBLANES + r + 1, :]
                            if k > c:
                                beats = row >= mine
                            elif k < c:
                                beats = row > mine
                            else:
                                beats = (row > mine) | ((row == mine) & (sub > r))
                            cnt = cnt + jnp.where(beats, 1.0, 0.0)
                    rank_scr[kh, k * SUBLANES:(k + 1) * SUBLANES, :] += cnt

    biases = [jnp.where(rank_scr[kh] < top_n, 0.0, SEL_BIAS) for kh in range(NSA_KV_HEADS)]
    ocmp_ref[0] = jnp.concatenate([outs[h] for h in NSA_HEAD_ORDER], axis=0).T
    pad = jnp.zeros((LANES - n_sel, tt), F32)
    bias_ref[0] = jnp.concatenate([biases[0], pad, biases[1], pad], axis=0).T.astype(BF16)


def _cmp_topk(q, kc, vct, n_cmp):
    b, s, _ = q.shape
    nk = kc.shape[1]
    n_sel = s // NSA_SEL_LEN
    tt = min(TT_CMP, s)
    kern = functools.partial(_cmp_topk_kernel, tt=tt, nk=nk, n_cmp=n_cmp, n_sel=n_sel, top_n=min(NSA_SEL_TOPN, n_sel))
    return pl.pallas_call(
        kern,
        grid=(b, s // tt),
        in_specs=[pl.BlockSpec((1, tt, WIDTH), lambda bi, ti: (bi, ti, 0)),
                  pl.BlockSpec((1, nk, LANES), lambda bi, ti: (bi, 0, 0)),
                  pl.BlockSpec((1, LANES, nk), lambda bi, ti: (bi, 0, 0))],
        out_specs=[pl.BlockSpec((1, tt, WIDTH), lambda bi, ti: (bi, ti, 0))] * 2,
        out_shape=[jax.ShapeDtypeStruct((b, s, WIDTH), F32), jax.ShapeDtypeStruct((b, s, WIDTH), BF16)],
        scratch_shapes=[pltpu.VMEM((NSA_KV_HEADS, nk + SUBLANES, tt), F32), pltpu.VMEM((NSA_KV_HEADS, n_sel, tt), F32),
                        pltpu.VMEM((NSA_KV_HEADS, n_sel, tt), F32)],
        compiler_params=pltpu.CompilerParams(dimension_semantics=("arbitrary", "arbitrary"),
                                             vmem_limit_bytes=VMEM_LIMIT),
        name="cmp_topk",
    )(q, kc, vct)


CAUSAL, WINDOW = 0, 1


def _selwin_kernel(q_ref, bias_ref, ocmp_ref, gate_ref, zn_ref, ks_ref, vs_ref, kw_ref, vw_ref, e_ref, cm_ref, out_ref,
                   acc_scr, m_scr, sa_scr, sb_scr, *, tq, tk):
    t0 = pl.program_id(1) * tq
    jd = t0 // tk
    heads_per_kv = N_HEADS // NSA_KV_HEADS
    half = _iota((tq, LANES), 1) // HEAD_DIM
    q_win, q_sel = [], []
    for h in range(N_HEADS):
        qc = q_ref[0, :, (h % 2) * LANES:(h % 2 + 1) * LANES]
        qh = jnp.where(half == h // heads_per_kv, qc, jnp.zeros_like(qc))
        q_win.append(qh)
        q_sel.append(jnp.concatenate([qh, bias_ref[0, :, (h // heads_per_kv) * LANES:(h // heads_per_kv + 1) * LANES]],
                                     axis=1))
    ones = jnp.ones((tk, LANES), BF16)
    lanes = lambda x, n: jnp.concatenate([x] * (n // LANES), axis=1)

    start = lambda j: j * tk if isinstance(j, int) else pl.multiple_of(j * tk, tk)

    def scores(k_ref, j, selected, masking, buf):
        k0 = start(j)
        kt = k_ref[0, pl.ds(k0, tk), :]
        if selected:
            kt = jnp.concatenate([kt, e_ref[pl.ds(k0, tk), :]], axis=1)
        for h in range(N_HEADS):
            s = _dot_nt((q_sel if selected else q_win)[h], kt)
            buf[h] = s if masking is None else s + cm_ref[masking]

    def attend(v_ref, j, buf):
        k0 = start(j)
        vt = jnp.concatenate([v_ref[0, pl.ds(k0, tk), :], ones], axis=1)
        for h in range(N_HEADS):
            blk = slice(h * tq, (h + 1) * tq)
            s = buf[h]
            m_old = m_scr[blk, :]
            m_new = jnp.maximum(m_old, jnp.max(s, axis=-1, keepdims=True))
            alpha = jnp.exp(m_old - m_new)
            p = jnp.exp(s - lanes(m_new, tk))
            acc_scr[blk, :] = lanes(alpha, WIDTH) * acc_scr[blk, :] + _dot(p.astype(BF16), vt)
            m_scr[blk, :] = m_new

    def reset():
        m_scr[...] = jnp.full(m_scr.shape, NEG, F32)
        acc_scr[...] = jnp.zeros(acc_scr.shape, F32)

    def result():
        res = [acc_scr[h * tq:(h + 1) * tq, :LANES] * (1.0 / acc_scr[h * tq:(h + 1) * tq, LANES:])
               for h in range(N_HEADS)]
        return jnp.concatenate([jnp.where(half == 0, res[c], res[c + heads_per_kv]) for c in range(2)], axis=1)

    reset()
    scores(ks_ref, jd, True, CAUSAL, sa_scr)

    def sel_body(pair, carry):
        j = jd - 2 * pair
        scores(ks_ref, j - 1, True, None, sb_scr)
        attend(vs_ref, j, sa_scr)
        scores(ks_ref, j - 2, True, None, sa_scr)
        attend(vs_ref, j - 1, sb_scr)
        return carry

    lax.fori_loop(0, jd // 2, sel_body, 0)

    @pl.when(jd % 2 == 1)
    def _():
        scores(ks_ref, 0, True, None, sb_scr)
        attend(vs_ref, 1, sa_scr)
        attend(vs_ref, 0, sb_scr)

    @pl.when(jd % 2 == 0)
    def _():
        attend(vs_ref, 0, sa_scr)

    o_slc = result()

    reset()
    scores(kw_ref, jd, False, CAUSAL, sa_scr)

    @pl.when(jd >= 1)
    def _():
        scores(kw_ref, jd - 1, False, WINDOW, sb_scr)
        attend(vw_ref, jd, sa_scr)
        attend(vw_ref, jd - 1, sb_scr)

    @pl.when(jd == 0)
    def _():
        attend(vw_ref, 0, sa_scr)

    o_win = result()
    gate = gate_ref[0]
    o = gate[:, :WIDTH] * ocmp_ref[0] + gate[:, WIDTH:2 * WIDTH] * o_slc + gate[:, 2 * WIDTH:] * o_win
    out_ref[0] = (o * zn_ref[0]).astype(BF16)


def _selwin(q, bias, ocmp, gate, zn, ks, vs, kw, vw):
    b, s, _ = q.shape
    tq, tk = min(TQ_SEL, s), min(TK_SEL, s)
    assert NSA_WINDOW == tk and tk == tq and s // NSA_SEL_LEN <= HEAD_DIM
    member = (jnp.arange(s)[:, None] // NSA_SEL_LEN == jnp.arange(LANES)[None, :]).astype(BF16)
    row, col = jnp.arange(tq)[:, None], jnp.arange(tk)[None, :]
    cmask = jnp.where(jnp.stack([col <= row, col > row]), 0.0, NEG).astype(F32)
    tok = lambda w: pl.BlockSpec((1, tq, w), lambda bi, qi: (bi, qi, 0))
    seq = pl.BlockSpec((1, s, LANES), lambda bi, qi: (bi, 0, 0))
    return pl.pallas_call(
        functools.partial(_selwin_kernel, tq=tq, tk=tk),
        grid=(b, s // tq),
        in_specs=[tok(WIDTH), tok(WIDTH), tok(WIDTH), tok(3 * WIDTH), tok(WIDTH), seq, seq, seq, seq,
                  pl.BlockSpec(member.shape, lambda bi, qi: (0, 0)),
                  pl.BlockSpec(cmask.shape, lambda bi, qi: (0, 0, 0))],
        out_specs=tok(WIDTH),
        out_shape=jax.ShapeDtypeStruct((b, s, WIDTH), BF16),
        scratch_shapes=[pltpu.VMEM((N_HEADS * tq, WIDTH), F32), pltpu.VMEM((N_HEADS * tq, LANES), F32),
                        pltpu.VMEM((N_HEADS, tq, tk), F32), pltpu.VMEM((N_HEADS, tq, tk), F32)],
        compiler_params=pltpu.CompilerParams(dimension_semantics=("arbitrary", "arbitrary"),
                                             vmem_limit_bytes=VMEM_LIMIT),
        name="selwin",
    )(q, bias, ocmp, gate, zn, ks, vs, kw, vw, member, cmask)


def _stickbrk_kernel(q_ref, k_ref, v_ref, z_ref, mk_ref, out_ref, acc_scr, carry_scr, *, tq, tk):
    t0 = pl.program_id(1) * tq
    jd = t0 // tk
    qs = _stack_heads(q_ref[0])
    rows = N_HEADS * tq
    tri = jnp.where(_iota((tk, tk), 0) >= _iota((tk, tk), 1), 1.0, 0.0).astype(BF16)
    lanes = lambda x, n: jnp.concatenate([x] * (n // LANES), axis=1)

    def tile(i, diagonal):
        k0 = pl.multiple_of((jd - i) * tk, tk)
        z = _dot_nt(qs, k_ref[0, pl.ds(k0, tk), :])
        log1mb = -(jnp.maximum(z, 0.0) + jnp.log(1.0 + jnp.exp(-jnp.abs(z))))
        if diagonal:
            log1mb = log1mb * jnp.concatenate([mk_ref[0]] * N_HEADS, axis=0)
        hi = log1mb.astype(BF16)
        lo = (log1mb - hi.astype(F32)).astype(BF16)
        suffix = _dot(hi, tri) + _dot(lo, tri)
        tile_sum = jnp.broadcast_to(jnp.sum(log1mb, axis=-1, keepdims=True), (rows, LANES))
        if diagonal:
            w = jnp.exp(z + suffix + jnp.concatenate([mk_ref[1]] * N_HEADS, axis=0))
            acc_scr[...] = _dot(w.astype(BF16), v_ref[0, pl.ds(k0, tk), :])
            carry = tile_sum
        else:
            carry = carry_scr[...]
            w = jnp.exp(z + suffix + lanes(carry, tk))
            acc_scr[...] += _dot(w.astype(BF16), v_ref[0, pl.ds(k0, tk), :])
            carry = carry + tile_sum
        carry_scr[...] = carry
        return (jnp.max(carry) < SB_DEAD).astype(jnp.int32)

    def cond(c):
        i, dead = c
        return (i <= jd) & (dead == 0)

    def body(c):
        return c[0] + 1, tile(c[0], False)

    lax.while_loop(cond, body, (jnp.int32(1), tile(0, True)))
    out_ref[0] = (_unstack_heads(acc_scr[...], tq) * z_ref[0]).astype(BF16)


def _stickbrk(q, k, v, z):
    b, s, _ = q.shape
    tq, tk = min(TQ_SB, s), min(TK_SB, s)
    assert tq == tk
    keep = jnp.arange(tk)[None, :] < jnp.arange(tq)[:, None]
    masks = jnp.stack([jnp.where(keep, 1.0, 0.0), jnp.where(keep, 0.0, NEG)]).astype(F32)
    tok = pl.BlockSpec((1, tq, WIDTH), lambda bi, qi: (bi, qi, 0))
    seq = pl.BlockSpec((1, s, WIDTH), lambda bi, qi: (bi, 0, 0))
    return pl.pallas_call(
        functools.partial(_stickbrk_kernel, tq=tq, tk=tk),
        grid=(b, s // tq),
        in_specs=[tok, seq, seq, tok, pl.BlockSpec(masks.shape, lambda bi, qi: (0, 0, 0))],
        out_specs=tok,
        out_shape=jax.ShapeDtypeStruct((b, s, WIDTH), BF16),
        scratch_shapes=[pltpu.VMEM((N_HEADS * tq, WIDTH), F32), pltpu.VMEM((N_HEADS * tq, LANES), F32)],
        compiler_params=pltpu.CompilerParams(dimension_semantics=("arbitrary", "arbitrary"),
                                             vmem_limit_bytes=VMEM_LIMIT),
        name="stickbrk",
    )(q, k, v, z, masks)


def _s5_disc_kernel(are_ref, aim_ref, ldt_ref, bre_ref, bim_ref, abre_ref, abim_ref, bbre_ref, bbim_ref):
    dt = jnp.exp(ldt_ref[...])
    lr, li = are_ref[...], aim_ref[...]
    mag = jnp.exp(lr * dt)
    ab_re, ab_im = mag * jnp.cos(li * dt), mag * jnp.sin(li * dt)
    den = lr * lr + li * li
    coef_re = ((ab_re - 1.0) * lr + ab_im * li) / den
    coef_im = (ab_im * lr - (ab_re - 1.0) * li) / den
    abre_ref[...] = ab_re
    abim_ref[...] = ab_im
    br, bi = bre_ref[...], bim_ref[...]
    bbre_ref[...] = coef_re[:, None, :] * br - coef_im[:, None, :] * bi
    bbim_ref[...] = coef_re[:, None, :] * bi + coef_im[:, None, :] * br


def _s5_discretise(a_re, a_im, log_dt, b_re, b_im):
    g, p = a_re.shape
    brt, bit = jnp.swapaxes(b_re, 1, 2), jnp.swapaxes(b_im, 1, 2)
    return pl.pallas_call(
        _s5_disc_kernel,
        out_shape=[jax.ShapeDtypeStruct((g, p), F32)] * 2 + [jax.ShapeDtypeStruct(brt.shape, F32)] * 2,
        name="s5_disc",
    )(a_re, a_im, log_dt[:, None], brt, bit)


def _s5_kernel(u_ref, z_ref, bmat_ref, a_ref, cmat_ref, d_ref, gw_ref, gb_ref, out_ref, xs_scr, state_scr,
               *, nb, ts, ns):
    @pl.when(pl.program_id(0) == 0)
    def _():
        state_scr[...] = jnp.zeros((nb, 2 * ns), F32)

    nc = 2 * ns // LANES
    for b in range(nb):
        bu = _dot(u_ref[b].astype(BF16), bmat_ref[...])
        for c in range(nc):
            xs_scr[c, pl.ds(b, ts, stride=nb), :] = bu[:, c * LANES:(c + 1) * LANES]

    a_re = jnp.broadcast_to(a_ref[0:1, :], (nb, ns))
    a_im = jnp.broadcast_to(a_ref[1:2, :], (nb, ns))

    def step(t, state):
        x_re, x_im = state
        r0 = pl.multiple_of(t * nb, nb)
        bu = jnp.concatenate([xs_scr[c, pl.ds(r0, nb), :] for c in range(nc)], axis=1)
        n_re = a_re * x_re - a_im * x_im + bu[:, :ns]
        n_im = a_re * x_im + a_im * x_re + bu[:, ns:]
        for c in range(nc // 2):
            xs_scr[c, pl.ds(r0, nb), :] = n_re[:, c * LANES:(c + 1) * LANES]
            xs_scr[nc // 2 + c, pl.ds(r0, nb), :] = n_im[:, c * LANES:(c + 1) * LANES]
        return n_re, n_im

    st = state_scr[...]
    x_re, x_im = lax.fori_loop(0, ts, step, (st[:, :ns], st[:, ns:]))
    state_scr[...] = jnp.concatenate([x_re, x_im], axis=1)

    width = u_ref.shape[-1]
    ys = []
    for b in range(nb):
        xs = jnp.concatenate([xs_scr[c, pl.ds(b, ts, stride=nb), :] for c in range(nc)], axis=1)
        ys.append((_dot(xs.astype(BF16), cmat_ref[...]) + d_ref[...] * u_ref[b]).astype(BF16))
    for b in range(nb):
        glu = _dot(ys[b], gw_ref[...]) + gb_ref[...]
        out_ref[b] = (glu[:, :width] * _sigmoid(glu[:, width:]) * z_ref[b]).astype(BF16)


def _s5(u, z, bmat, a_rows, cmat, d_row, glu_w, glu_b):
    nb, s, w = u.shape
    ts = min(TS_S5, s)
    ns = a_rows.shape[1]
    full = lambda a: pl.BlockSpec(a.shape, lambda i: (0, 0))
    tok = pl.BlockSpec((nb, ts, w), lambda i: (0, i, 0))
    return pl.pallas_call(
        functools.partial(_s5_kernel, nb=nb, ts=ts, ns=ns),
        grid=(s // ts,),
        in_specs=[tok, tok, full(bmat), full(a_rows), full(cmat), full(d_row), full(glu_w), full(glu_b)],
        out_specs=tok,
        out_shape=jax.ShapeDtypeStruct((nb, s, w), BF16),
        scratch_shapes=[pltpu.VMEM((2 * ns // LANES, ts * nb, LANES), F32), pltpu.VMEM((nb, 2 * ns), F32)],
        compiler_params=pltpu.CompilerParams(dimension_semantics=("arbitrary",), vmem_limit_bytes=VMEM_LIMIT),
        name="s5",
    )(u, z, bmat, a_rows, cmat, d_row, glu_w, glu_b)


def _merge_kernel(x_ref, ng_ref, o0_ref, o1_ref, o2_ref, o3_ref, wm_ref, wb_ref, wo_ref, out_ref):
    x = x_ref[...]
    ms = jnp.mean(x * x, axis=-1, keepdims=True)
    h = (x * lax.rsqrt(ms + NORM_EPS) * ng_ref[...]).astype(BF16)
    d = x.shape[1]
    mixed = None
    for m, o_ref in enumerate((o0_ref, o1_ref, o2_ref, o3_ref)):
        gate = _sigmoid(_dot(h, wm_ref[:, m * d:(m + 1) * d]))
        term = gate * _dot(o_ref[...], wb_ref[m])
        mixed = term if mixed is None else mixed + term
    out_ref[...] = x + _dot(mixed.astype(BF16), wo_ref[...])


def _merge(x2, ng, outs, wm, wb, wo):
    t, d = x2.shape
    tm = min(TM_PROJ, t)
    tok = lambda w: pl.BlockSpec((tm, w), lambda i: (i, 0))
    full = lambda a: pl.BlockSpec(a.shape, lambda i: (0,) * a.ndim)
    return pl.pallas_call(
        _merge_kernel,
        grid=(t // tm,),
        in_specs=[tok(d), full(ng)] + [tok(WIDTH)] * 4 + [full(wm), full(wb), full(wo)],
        out_specs=tok(d),
        out_shape=jax.ShapeDtypeStruct((t, d), F32),
        compiler_params=pltpu.CompilerParams(dimension_semantics=("arbitrary",), vmem_limit_bytes=VMEM_LIMIT),
        name="merge",
    )(x2, ng, *outs, wm, wb, wo)


def _wprep_kernel(w_ref, wqk_ref, wv_ref, wc_ref, wg_ref, wz_ref, wsc_ref, wsb_ref, ws5_ref, wm_ref, *, d):
    w = WIDTH
    o_q, o_kv, o_gate = 0, w, 4 * w
    o_nz = o_gate + 3 * N_HEADS
    o_sc, o_scz, o_sb, o_sbz, o_s5, o_s5z = o_nz + w, o_nz + 4 * w, o_nz + 5 * w, o_nz + 8 * w, o_nz + 9 * w, o_nz + 10 * w
    o_merge = o_nz + 11 * w
    col = lambda a, n: w_ref[0, :, a:a + n]
    kv = lambda i: o_kv + i * (w // 2)
    cat = lambda parts: jnp.concatenate(parts, axis=1).astype(BF16)

    def heads(a):
        return [col(a + h * HEAD_DIM, HEAD_DIM) for h in NSA_HEAD_ORDER]

    wqk_ref[...] = cat(heads(o_q) + [col(kv(2), w // 2), col(kv(4), w // 2)])
    wv_ref[...] = cat([col(kv(3), w // 2), col(kv(5), w // 2)])
    wc_ref[...] = col(kv(0), w).astype(BF16)
    gates = col(o_gate, 3 * N_HEADS)
    wg_ref[...] = cat([jnp.broadcast_to(gates[:, c:c + 1], (gates.shape[0], HEAD_DIM))
                       for c in (br * N_HEADS + h for br in range(3) for h in NSA_HEAD_ORDER)])
    wz_ref[...] = cat(heads(o_nz) + [col(o_scz, w), col(o_sbz, w), col(o_s5z, w)])
    wsc_ref[...] = col(o_sc, 3 * w).astype(BF16)
    wsb_ref[...] = col(o_sb, 3 * w).astype(BF16)
    ws5_ref[...] = col(o_s5, w).astype(BF16)
    wm_ref[...] = col(o_merge, N_HEADS * d).astype(BF16)


def _wprep(w_in, layer):
    _, d, n_in = w_in.shape
    tr = min(TR_WPREP, d)
    widths = [2 * WIDTH, WIDTH, WIDTH, 3 * WIDTH, 4 * WIDTH, 3 * WIDTH, 3 * WIDTH, WIDTH, N_HEADS * d]
    return pl.pallas_call(
        functools.partial(_wprep_kernel, d=d),
        grid=(d // tr,),
        in_specs=[pl.BlockSpec((1, tr, n_in), lambda i: (layer, i, 0))],
        out_specs=[pl.BlockSpec((tr, n), lambda i: (i, 0)) for n in widths],
        out_shape=[jax.ShapeDtypeStruct((d, n), BF16) for n in widths],
        compiler_params=pltpu.CompilerParams(dimension_semantics=("arbitrary",), vmem_limit_bytes=VMEM_LIMIT),
        name="wprep",
    )(w_in)


def _block_diag(blocks):
    g, r, c = blocks.shape
    eye = jnp.eye(g, dtype=blocks.dtype)
    return (eye[:, None, :, None] * blocks[:, :, None, :]).reshape(g * r, g * c)


def _block_diag_pair(w1):
    eye = jnp.eye(NSA_KV_HEADS, dtype=w1.dtype)
    bd = w1[..., :, None, :, None, :] * eye[:, None, :, None]
    return bd.reshape(w1.shape[:-3] + (w1.shape[-3] * NSA_KV_HEADS * HEAD_DIM, NSA_KV_HEADS * HEAD_DIM))


def _layer(x, cos, sin, cos_c, sin_c, n_cmp, norm_g, w_in_all, layer, qk_g, cmp_pe, cmp_w1, cmp_w2, conv_w,
           a_re, a_im, log_dt, b_re, b_im, c_re, c_im, d_skip, glu_w, glu_b, w_branch, w_out):
    b, s, d = x.shape
    w = WIDTH
    *weights, w_merge = _wprep(w_in_all, layer)
    qkg = jnp.concatenate([jnp.tile(qk_g[0], N_HEADS), jnp.tile(qk_g[2], NSA_KV_HEADS),
                           jnp.tile(qk_g[3], NSA_KV_HEADS)])[None, :]
    ng = norm_g[None, :]

    (q, ks, kw, vs, vw, gate, zn, zsb, zs5, sc_o, sbq, sbk, sbv, s5u, kc_in, vc_in) = _inproj(
        x, ng, weights, qkg, cos, sin, conv_w)

    half = NSA_CMP_STRIDE
    two = NSA_CMP_LEN // half
    pe = jnp.tile(cmp_pe.reshape(2, two, half, HEAD_DIM), (1, 1, 1, NSA_KV_HEADS))
    w1 = _block_diag_pair(cmp_w1.reshape(2, two, half, HEAD_DIM, HEAD_DIM))
    w2k, w2v = _block_diag(jnp.stack([cmp_w2[0]] * NSA_KV_HEADS)), _block_diag(jnp.stack([cmp_w2[1]] * NSA_KV_HEADS))
    kc, vct = _compress(kc_in, vc_in, pe, w1.astype(BF16), w2k.astype(BF16), w2v.T.astype(BF16),
                        jnp.tile(qk_g[1], NSA_KV_HEADS)[None, :], cos_c, sin_c)

    ocmp, bias = _cmp_topk(q, kc, vct, n_cmp)
    nsa_o = _selwin(q, bias, ocmp, gate, zn, ks, vs, kw, vw)
    sb_o = _stickbrk(sbq, sbk, sbv, zsb)

    ab_re, ab_im, bb_re, bb_im = _s5_discretise(a_re, a_im, log_dt, b_re, b_im)
    bmat = jnp.concatenate([_block_diag(bb_re), _block_diag(bb_im)], axis=1).astype(BF16)
    cmat = jnp.concatenate([_block_diag(jnp.swapaxes(c_re, 1, 2)),
                            -_block_diag(jnp.swapaxes(c_im, 1, 2))], axis=0).astype(BF16)
    a_rows = jnp.stack([ab_re.reshape(-1), ab_im.reshape(-1)])
    s5_o = _s5(s5u, zs5, bmat, a_rows, cmat, d_skip.reshape(1, -1), glu_w.astype(BF16), glu_b[None, :])

    wb_nsa = w_branch[0].reshape(N_HEADS, HEAD_DIM, d)[jnp.array(NSA_HEAD_ORDER)].reshape(w, d)
    wb = jnp.concatenate([wb_nsa[None], w_branch[1:]], axis=0).astype(BF16)
    out = _merge(x.reshape(b * s, d), ng,
                 [o.reshape(b * s, w) for o in (nsa_o, sc_o, sb_o, s5_o)],
                 w_merge, wb, w_out.astype(BF16))
    return out.reshape(b, s, d)


def kernel(x, positions, norm_g, w_in, nsa_qk_g, nsa_cmp_pe, nsa_cmp_w1, nsa_cmp_w2, sc_conv_w, s5_a_re, s5_a_im,
           s5_log_dt, s5_b_re, s5_b_im, s5_c_re, s5_c_im, s5_d, s5_glu_w, s5_glu_b, w_branch, w_out):
    b, s, _ = x.shape
    assert s % TK_SEL == 0 or s < TK_SB
    cos, sin = _rope_tables(positions.reshape(-1))
    cos, sin = cos.reshape(b, s, LANES), sin.reshape(b, s, LANES)
    nk = s // NSA_CMP_STRIDE
    n_cmp = (s - NSA_CMP_LEN) // NSA_CMP_STRIDE + 1
    pos_c = jnp.concatenate([positions[:, NSA_CMP_LEN - 1::NSA_CMP_STRIDE],
                             jnp.zeros((b, nk - n_cmp), positions.dtype)], axis=1)
    cos_c, sin_c = _rope_tables(pos_c.reshape(-1))
    cos_c, sin_c = cos_c.reshape(b, nk, LANES), sin_c.reshape(b, nk, LANES)
    for l in range(norm_g.shape[0]):
        x = _layer(x, cos, sin, cos_c, sin_c, n_cmp, norm_g[l], w_in, l, nsa_qk_g[l], nsa_cmp_pe[l], nsa_cmp_w1[l],
                   nsa_cmp_w2[l], sc_conv_w[l], s5_a_re[l], s5_a_im[l], s5_log_dt[l], s5_b_re[l], s5_b_im[l],
                   s5_c_re[l], s5_c_im[l], s5_d[l], s5_glu_w[l], s5_glu_b[l], w_branch[l], w_out[l])
    return x
```

```python
import functools
import math

import jax
import jax.numpy as jnp
from jax import lax
from jax.experimental import pallas as pl
from jax.experimental.pallas import tpu as pltpu

F32 = jnp.float32
BF16 = jnp.bfloat16

HEAD_DIM = 64
N_HEADS = 4
WIDTH = N_HEADS * HEAD_DIM
NSA_KV_HEADS = 2
NSA_CMP_LEN = 32
NSA_CMP_STRIDE = 16
NSA_SEL_LEN = 64
NSA_SEL_TOPN = 16
NSA_WINDOW = 512
NSA_FORCE_SCORE = 1.0e4
S5_GROUPS = 16
S5_GROUP_CH = 16
S5_STATE = 64
ROPE_THETA = 10000.0
NORM_EPS = 1e-6
QK_SCALE = HEAD_DIM ** -0.5

LANES = 128
SUBLANES = 8
NEG = -1.0e30
SEL_BIAS = -30000.0
SB_DEAD_LOG2 = 160.0
LOG2_E = 1.4426950408889634
VMEM_LIMIT = 56 * 1024 * 1024

TM_PROJ = 512
TR_WPREP = 256
TT_CMP = 128
TQ_SB = 256
TK_SB = 256
TQ_SEL = 512
TK_SEL = 512
NSA_HEAD_ORDER = (0, 2, 1, 3)
TS_S5 = 128


def _dot(a, b):
    return jnp.dot(a, b, preferred_element_type=F32)


def _dot_nt(a, b):
    return lax.dot_general(a, b, (((1,), (1,)), ((), ())), preferred_element_type=F32)


def _silu(x):
    return x * (1.0 / (1.0 + jnp.exp(-x)))


def _sigmoid(x):
    return 1.0 / (1.0 + jnp.exp(-x))


def _iota(shape, dim):
    return lax.broadcasted_iota(jnp.int32, shape, dim)


def _group_mean_sq(x):
    outs = []
    lane = _iota((x.shape[0], LANES), 1)
    low = lane < HEAD_DIM
    for c in range(x.shape[1] // LANES):
        xc = x[:, c * LANES:(c + 1) * LANES]
        sq = xc * xc
        s_lo = jnp.sum(jnp.where(low, sq, 0.0), axis=-1, keepdims=True)
        s_hi = jnp.sum(jnp.where(low, 0.0, sq), axis=-1, keepdims=True)
        outs.append(jnp.where(low, s_lo, s_hi) * (1.0 / HEAD_DIM))
    return outs[0] if len(outs) == 1 else jnp.concatenate(outs, axis=1)


def _rot_half(y):
    w = y.shape[1]
    first = (_iota(y.shape, 1) & (HEAD_DIM // 2)) == 0
    return jnp.where(first, pltpu.roll(y, w - HEAD_DIM // 2, 1), pltpu.roll(y, HEAD_DIM // 2, 1))


def _norm_rope(x, gain, cos, sin_signed):
    reps = x.shape[1] // LANES
    y = x * lax.rsqrt(_group_mean_sq(x) + NORM_EPS) * gain
    c = cos if reps == 1 else jnp.concatenate([cos] * reps, axis=1)
    s = sin_signed if reps == 1 else jnp.concatenate([sin_signed] * reps, axis=1)
    return y * c + _rot_half(y) * s


def _stack_heads(q):
    head = _iota(q.shape, 1) // HEAD_DIM
    return jnp.concatenate([jnp.where(head == h, q, jnp.zeros_like(q)) for h in range(N_HEADS)], axis=0)


def _unstack_heads(o4, m):
    head = _iota((m, WIDTH), 1) // HEAD_DIM
    out = jnp.zeros((m, WIDTH), F32)
    for h in range(N_HEADS):
        out = jnp.where(head == h, o4[h * m:(h + 1) * m], out)
    return out


def _rope_table_kernel(pos_ref, freq_ref, sign_ref, cos_ref, sin_ref):
    ang = pos_ref[...].astype(F32) * freq_ref[...]
    cos_ref[...] = jnp.cos(ang)
    sin_ref[...] = jnp.sin(ang) * sign_ref[...]


def _rope_tables(pos_flat):
    n = pos_flat.shape[0]
    half = HEAD_DIM // 2
    inv_freq = jnp.power(ROPE_THETA, -jnp.arange(half, dtype=F32) / half)
    freq = jnp.tile(inv_freq, LANES // half)[None, :]
    sign = jnp.tile(jnp.concatenate([-jnp.ones((half,), F32), jnp.ones((half,), F32)]), LANES // HEAD_DIM)[None, :]
    tm = 512 if n % 512 == 0 else n
    return pl.pallas_call(
        _rope_table_kernel,
        grid=(n // tm,),
        in_specs=[pl.BlockSpec((tm, 1), lambda i: (i, 0)),
                  pl.BlockSpec((1, LANES), lambda i: (0, 0)),
                  pl.BlockSpec((1, LANES), lambda i: (0, 0))],
        out_specs=[pl.BlockSpec((tm, LANES), lambda i: (i, 0))] * 2,
        out_shape=[jax.ShapeDtypeStruct((n, LANES), F32)] * 2,
        name="rope_tables",
    )(pos_flat[:, None], freq, sign)


def _inproj_kernel(x_ref, ng_ref, wqk_ref, wv_ref, wc_ref, wg_ref, wz_ref, wsc_ref, wsb_ref, ws5_ref,
                   qkg_ref, cos_ref, sin_ref, convw_ref,
                   q_ref, ks_ref, kw_ref, vs_ref, vw_ref, gate_ref, zn_ref, zsb_ref, zs5_ref,
                   sc_ref, sbq_ref, sbk_ref, sbv_ref, s5u_ref, kc_ref, vc_ref, ubuf_ref, *, tm):
    x = x_ref[0]
    ms = jnp.mean(x * x, axis=-1, keepdims=True)
    h = (x * lax.rsqrt(ms + NORM_EPS) * ng_ref[...]).astype(BF16)

    qk = _norm_rope(_dot(h, wqk_ref[...]), qkg_ref[...], cos_ref[0], sin_ref[0])
    q_ref[0] = (qk[:, :WIDTH] * QK_SCALE).astype(BF16)
    ks_ref[0] = qk[:, WIDTH:WIDTH + LANES].astype(BF16)
    kw_ref[0] = qk[:, WIDTH + LANES:].astype(BF16)

    v = _dot(h, wv_ref[...])
    vs_ref[0] = v[:, :LANES].astype(BF16)
    vw_ref[0] = v[:, LANES:].astype(BF16)
    kvc = _dot(h, wc_ref[...])
    kc_ref[0] = kvc[:, :LANES]
    vc_ref[0] = kvc[:, LANES:]
    gate_ref[0] = _sigmoid(_dot(h, wg_ref[...]))

    z = _silu(_dot(h, wz_ref[...]))
    zn_ref[0] = z[:, :WIDTH]
    zsb_ref[0] = z[:, 2 * WIDTH:3 * WIDTH]
    zs5_ref[0] = z[:, 3 * WIDTH:]

    bcx = _dot(h, wsc_ref[...])
    u = bcx[:, WIDTH:2 * WIDTH] * bcx[:, 2 * WIDTH:]

    @pl.when(pl.program_id(1) == 0)
    def _():
        ubuf_ref[0:SUBLANES, :] = jnp.zeros((SUBLANES, WIDTH), F32)

    ubuf_ref[SUBLANES:SUBLANES + tm, :] = u
    u1 = ubuf_ref[SUBLANES - 1:SUBLANES - 1 + tm, :]
    u2 = ubuf_ref[SUBLANES - 2:SUBLANES - 2 + tm, :]
    cw = convw_ref[...]
    y = cw[2:3, :] * u + cw[1:2, :] * u1 + cw[0:1, :] * u2
    ubuf_ref[0:SUBLANES, :] = ubuf_ref[tm:tm + SUBLANES, :]
    sc_ref[0] = (bcx[:, :WIDTH] * y * z[:, WIDTH:2 * WIDTH]).astype(BF16)

    sb = _dot(h, wsb_ref[...])
    sbq_ref[0] = (sb[:, :WIDTH] * QK_SCALE).astype(BF16)
    sbk_ref[0] = sb[:, WIDTH:2 * WIDTH].astype(BF16)
    sbv_ref[0] = sb[:, 2 * WIDTH:].astype(BF16)
    s5u_ref[0] = _dot(h, ws5_ref[...])


def _inproj(x, ng, weights, qkg, cos, sin, convw):
    b, s, d = x.shape
    tm = min(TM_PROJ, s)
    full2 = lambda a: pl.BlockSpec(a.shape, lambda bi, si: (0, 0))
    tok = lambda w: pl.BlockSpec((1, tm, w), lambda bi, si: (bi, si, 0))
    out_widths = [(WIDTH, BF16)] + [(LANES, BF16)] * 4 + [(3 * WIDTH, F32)] + [(WIDTH, F32)] * 3 \
        + [(WIDTH, BF16)] * 4 + [(WIDTH, F32)] + [(LANES, F32)] * 2
    return pl.pallas_call(
        functools.partial(_inproj_kernel, tm=tm),
        grid=(b, s // tm),
        in_specs=[tok(d), full2(ng)] + [full2(w) for w in weights] + [full2(qkg), tok(LANES), tok(LANES), full2(convw)],
        out_specs=[tok(w) for w, _ in out_widths],
        out_shape=[jax.ShapeDtypeStruct((b, s, w), dt) for w, dt in out_widths],
        scratch_shapes=[pltpu.VMEM((tm + 2 * SUBLANES, WIDTH), F32)],
        compiler_params=pltpu.CompilerParams(dimension_semantics=("arbitrary", "arbitrary"),
                                             vmem_limit_bytes=VMEM_LIMIT),
        name="inproj",
    )(x, ng, *weights, qkg, cos, sin, convw)


def _compress_kernel(kc_ref, vc_ref, pe_ref, w1_ref, w2k_ref, w2v_ref, g_ref, cos_ref, sin_ref, kco_ref, vct_ref, *, nk):
    half = NSA_CMP_STRIDE
    for j, src in ((0, kc_ref), (1, vc_ref)):
        tok = [src[0, pl.ds(i, nk, stride=half), :] for i in range(half)]
        lo = jnp.concatenate([(tok[i] + pe_ref[j, 0, i:i + 1, :]).astype(BF16) for i in range(half)], axis=1)
        hi = jnp.concatenate([(tok[i] + pe_ref[j, 1, i:i + 1, :]).astype(BF16) for i in range(half)], axis=1)
        hid = _dot(lo, w1_ref[j, 0]) + pltpu.roll(_dot(hi, w1_ref[j, 1]), nk - 1, 0)
        act = _silu(hid).astype(BF16)
        if j == 0:
            kco_ref[0] = _norm_rope(_dot(act, w2k_ref[...]), g_ref[...], cos_ref[0], sin_ref[0]).astype(BF16)
        else:
            vct_ref[0] = _dot_nt(w2v_ref[...], act).astype(BF16)


def _compress(kc, vc, pe, w1, w2k, w2v, g, cos, sin):
    b, s, _ = kc.shape
    nk = s // NSA_CMP_STRIDE
    full = lambda a: pl.BlockSpec(a.shape, lambda bi: (0,) * a.ndim)
    seq = pl.BlockSpec((1, s, LANES), lambda bi: (bi, 0, 0))
    tab = pl.BlockSpec((1, nk, LANES), lambda bi: (bi, 0, 0))
    return pl.pallas_call(
        functools.partial(_compress_kernel, nk=nk),
        grid=(b,),
        in_specs=[seq, seq, full(pe), full(w1), full(w2k), full(w2v), full(g), tab, tab],
        out_specs=[pl.BlockSpec((1, nk, LANES), lambda bi: (bi, 0, 0)),
                   pl.BlockSpec((1, LANES, nk), lambda bi: (bi, 0, 0))],
        out_shape=[jax.ShapeDtypeStruct((b, nk, LANES), BF16), jax.ShapeDtypeStruct((b, LANES, nk), BF16)],
        compiler_params=pltpu.CompilerParams(dimension_semantics=("arbitrary",), vmem_limit_bytes=VMEM_LIMIT),
        name="compress",
    )(kc, vc, pe, w1, w2k, w2v, g, cos, sin)


def _cmp_topk_kernel(q_ref, kc_ref, vct_ref, ocmp_ref, bias_ref, p_scr, imp_scr, rank_scr,
                     *, tt, nk, n_cmp, n_sel, top_n):
    t0 = pl.program_id(1) * tt
    kc = kc_ref[0]
    half = _iota((tt, LANES), 1) // HEAD_DIM
    nrow = _iota((nk, tt), 0)
    tcol = t0 + _iota((nk, tt), 1)
    valid_t = (nrow * NSA_CMP_STRIDE + (NSA_CMP_LEN - 1) <= tcol) & (nrow < n_cmp)
    brow = _iota((n_sel, tt), 0)
    tsel = t0 + _iota((n_sel, tt), 1)
    cur = tsel // NSA_SEL_LEN
    forced = (brow == 0) | (brow == cur) | (brow == cur - 1)
    in_past = brow * NSA_SEL_LEN <= tsel
    last_block = (t0 + tt - 1) // NSA_SEL_LEN
    sub = _iota((SUBLANES, tt), 0)
    heads_per_kv = N_HEADS // NSA_KV_HEADS
    scores = []
    for h in range(N_HEADS):
        qc = q_ref[0, :, (h % 2) * LANES:(h % 2 + 1) * LANES]
        scores.append(_dot_nt(kc, jnp.where(half == h // heads_per_kv, qc, jnp.zeros_like(qc))))
    probs = []
    for st in scores:
        st = jnp.where(valid_t, st, NEG)
        et = jnp.where(valid_t, jnp.exp(st - jnp.max(st, axis=0, keepdims=True)), 0.0)
        probs.append(et * (1.0 / jnp.maximum(jnp.sum(et, axis=0, keepdims=True), 1e-30)))
    outs = [_dot(vct_ref[0, (h // heads_per_kv) * HEAD_DIM:(h // heads_per_kv + 1) * HEAD_DIM, :],
                 probs[h].astype(BF16)) for h in range(N_HEADS)]
    groups = n_sel // SUBLANES
    imp_g = []
    for kh in range(NSA_KV_HEADS):
        psum = probs[kh * heads_per_kv]
        for g in range(1, heads_per_kv):
            psum = psum + probs[kh * heads_per_kv + g]
        p_scr[kh, SUBLANES:SUBLANES + nk, :] = psum
        p_scr[kh, 0:SUBLANES, :] = jnp.zeros((SUBLANES, tt), F32)
        ratio = NSA_SEL_LEN // NSA_CMP_STRIDE
        imp = p_scr[kh, pl.ds(SUBLANES - 1, n_sel, stride=ratio), :]
        for k in range(1, NSA_CMP_LEN // NSA_CMP_STRIDE + ratio - 1):
            imp = imp + p_scr[kh, pl.ds(SUBLANES - 1 + k, n_sel, stride=ratio), :]
        imp = jnp.where(forced, NSA_FORCE_SCORE, jnp.where(in_past, imp, -NSA_FORCE_SCORE))
        imp_scr[kh] = imp
        imp_g.append([imp[k * SUBLANES:(k + 1) * SUBLANES] for k in range(groups)])

    rank_scr[...] = jnp.zeros((NSA_KV_HEADS, n_sel, tt), F32)
    for m in range(groups):
        @pl.when(m * SUBLANES <= last_block)
        def _(m=m):
            for kh in range(NSA_KV_HEADS):
                for k in range(m + 1):
                    mine = imp_g[kh][k]
                    cnt = jnp.zeros((SUBLANES, tt), F32)
                    for c in ([m] if k < m else range(m + 1)):
                        for r in range(SUBLANES):
                            row = imp_scr[kh, c * SUBLANES + r:c * SUBLANES + r + 1, :]
                            if k > c:
                                beats = row >= mine
                            elif k < c:
                                beats = row > mine
                            else:
                                beats = (row > mine) | ((row == mine) & (sub > r))
                            cnt = cnt + jnp.where(beats, 1.0, 0.0)
                    rank_scr[kh, k * SUBLANES:(k + 1) * SUBLANES, :] += cnt

    biases = [jnp.where(rank_scr[kh] < top_n, 0.0, SEL_BIAS) for kh in range(NSA_KV_HEADS)]
    ocmp_ref[0] = jnp.concatenate([outs[h] for h in NSA_HEAD_ORDER], axis=0).T
    pad = jnp.zeros((LANES - n_sel, tt), F32)
    bias_ref[0] = jnp.concatenate([biases[0], pad, biases[1], pad], axis=0).T.astype(BF16)


def _cmp_topk(q, kc, vct, n_cmp):
    b, s, _ = q.shape
    nk = kc.shape[1]
    n_sel = s // NSA_SEL_LEN
    tt = min(TT_CMP, s)
    kern = functools.partial(_cmp_topk_kernel, tt=tt, nk=nk, n_cmp=n_cmp, n_sel=n_sel, top_n=min(NSA_SEL_TOPN, n_sel))
    return pl.pallas_call(
        kern,
        grid=(b, s // tt),
        in_specs=[pl.BlockSpec((1, tt, WIDTH), lambda bi, ti: (bi, ti, 0)),
                  pl.BlockSpec((1, nk, LANES), lambda bi, ti: (bi, 0, 0)),
                  pl.BlockSpec((1, LANES, nk), lambda bi, ti: (bi, 0, 0))],
        out_specs=[pl.BlockSpec((1, tt, WIDTH), lambda bi, ti: (bi, ti, 0))] * 2,
        out_shape=[jax.ShapeDtypeStruct((b, s, WIDTH), F32), jax.ShapeDtypeStruct((b, s, WIDTH), BF16)],
        scratch_shapes=[pltpu.VMEM((NSA_KV_HEADS, nk + SUBLANES, tt), F32), pltpu.VMEM((NSA_KV_HEADS, n_sel, tt), F32),
                        pltpu.VMEM((NSA_KV_HEADS, n_sel, tt), F32)],
        compiler_params=pltpu.CompilerParams(dimension_semantics=("arbitrary", "arbitrary"),
                                             vmem_limit_bytes=VMEM_LIMIT),
        name="cmp_topk",
    )(q, kc, vct)


CAUSAL, WINDOW = 0, 1


def _selwin_kernel(q_ref, bias_ref, ocmp_ref, gate_ref, zn_ref, ks_ref, vs_ref, kw_ref, vw_ref, e_ref, cm_ref, out_ref,
                   acc_scr, m_scr, *, tq, tk):
    t0 = pl.program_id(1) * tq
    jd = t0 // tk
    heads_per_kv = N_HEADS // NSA_KV_HEADS
    half = _iota((tq, LANES), 1) // HEAD_DIM
    q_win, q_sel = [], []
    for h in range(N_HEADS):
        qc = q_ref[0, :, (h % 2) * LANES:(h % 2 + 1) * LANES]
        qh = jnp.where(half == h // heads_per_kv, qc, jnp.zeros_like(qc))
        q_win.append(qh)
        q_sel.append(jnp.concatenate([qh, bias_ref[0, :, (h // heads_per_kv) * LANES:(h // heads_per_kv + 1) * LANES]],
                                     axis=1))
    ones = jnp.ones((tk, LANES), BF16)
    lanes = lambda x, n: jnp.concatenate([x] * (n // LANES), axis=1)

    def tile(i, k_ref, v_ref, selected, masking, first):
        k0 = pl.multiple_of((jd - i) * tk, tk)
        kt = k_ref[0, pl.ds(k0, tk), :]
        if selected:
            kt = jnp.concatenate([kt, e_ref[pl.ds(k0, tk), :]], axis=1)
        scores = [_dot_nt((q_sel if selected else q_win)[h], kt) for h in range(N_HEADS)]
        vt = jnp.concatenate([v_ref[0, pl.ds(k0, tk), :], ones], axis=1)
        ps, alphas = [], []
        for h, s in enumerate(scores):
            blk = slice(h * tq, (h + 1) * tq)
            if masking is not None:
                s = s + cm_ref[masking]
            s_max = jnp.max(s, axis=-1, keepdims=True)
            if first:
                m_new = jnp.broadcast_to(s_max, (tq, LANES))
            else:
                m_old = m_scr[blk, :]
                m_new = jnp.maximum(m_old, s_max)
                alphas.append(lanes(jnp.exp(m_old - m_new), WIDTH))
            ps.append(jnp.exp(s - lanes(m_new, tk)).astype(BF16))
            m_scr[blk, :] = m_new
        pv = _dot(jnp.concatenate(ps, axis=0), vt)
        acc_scr[...] = pv if first else jnp.concatenate(alphas, axis=0) * acc_scr[...] + pv

    def result():
        res = [acc_scr[h * tq:(h + 1) * tq, :LANES] * (1.0 / acc_scr[h * tq:(h + 1) * tq, LANES:])
               for h in range(N_HEADS)]
        return jnp.concatenate([jnp.where(half == 0, res[c], res[c + heads_per_kv]) for c in range(2)], axis=1)

    tile(0, ks_ref, vs_ref, True, CAUSAL, True)

    def sel_body(i, carry):
        tile(i, ks_ref, vs_ref, True, None, False)
        return carry

    lax.fori_loop(1, jd + 1, sel_body, 0)
    o_slc = result()

    tile(0, kw_ref, vw_ref, False, CAUSAL, True)

    @pl.when(jd >= 1)
    def _():
        tile(1, kw_ref, vw_ref, False, WINDOW, False)

    o_win = result()
    gate = gate_ref[0]
    o = gate[:, :WIDTH] * ocmp_ref[0] + gate[:, WIDTH:2 * WIDTH] * o_slc + gate[:, 2 * WIDTH:] * o_win
    out_ref[0] = (o * zn_ref[0]).astype(BF16)


def _selwin(q, bias, ocmp, gate, zn, ks, vs, kw, vw):
    b, s, _ = q.shape
    tq, tk = min(TQ_SEL, s), min(TK_SEL, s)
    assert NSA_WINDOW == tk and tk == tq and s // NSA_SEL_LEN <= HEAD_DIM
    member = (jnp.arange(s)[:, None] // NSA_SEL_LEN == jnp.arange(LANES)[None, :]).astype(BF16)
    row, col = jnp.arange(tq)[:, None], jnp.arange(tk)[None, :]
    cmask = jnp.where(jnp.stack([col <= row, col > row]), 0.0, NEG).astype(F32)
    tok = lambda w: pl.BlockSpec((1, tq, w), lambda bi, qi: (bi, qi, 0))
    seq = pl.BlockSpec((1, s, LANES), lambda bi, qi: (bi, 0, 0))
    return pl.pallas_call(
        functools.partial(_selwin_kernel, tq=tq, tk=tk),
        grid=(b, s // tq),
        in_specs=[tok(WIDTH), tok(WIDTH), tok(WIDTH), tok(3 * WIDTH), tok(WIDTH), seq, seq, seq, seq,
                  pl.BlockSpec(member.shape, lambda bi, qi: (0, 0)),
                  pl.BlockSpec(cmask.shape, lambda bi, qi: (0, 0, 0))],
        out_specs=tok(WIDTH),
        out_shape=jax.ShapeDtypeStruct((b, s, WIDTH), BF16),
        scratch_shapes=[pltpu.VMEM((N_HEADS * tq, WIDTH), F32), pltpu.VMEM((N_HEADS * tq, LANES), F32)],
        compiler_params=pltpu.CompilerParams(dimension_semantics=("arbitrary", "arbitrary"),
                                             vmem_limit_bytes=VMEM_LIMIT),
        name="selwin",
    )(q, bias, ocmp, gate, zn, ks, vs, kw, vw, member, cmask)


def _stickbrk_kernel(q_ref, k_ref, v_ref, z_ref, mk_ref, out_ref, acc_scr, carry_scr, *, tq, tk):
    t0 = pl.program_id(1) * tq
    jd = t0 // tk
    q = q_ref[0]
    head = _iota(q.shape, 1) // HEAD_DIM
    qh = [jnp.where(head == h, q, jnp.zeros_like(q)) for h in range(N_HEADS)]
    tri = jnp.where(_iota((tk, tk), 0) >= _iota((tk, tk), 1), 1.0, 0.0).astype(BF16)
    tri2 = jnp.concatenate([tri, tri], axis=0)
    lanes = lambda x, n: jnp.concatenate([x] * (n // LANES), axis=1)

    def tile(i, diagonal):
        k0 = pl.multiple_of((jd - i) * tk, tk)
        kt = k_ref[0, pl.ds(k0, tk), :]
        zs = [_dot_nt(qh[h], kt) * LOG2_E for h in range(N_HEADS)]
        sps, sufs = [], []
        for z2 in zs:
            sp2 = jnp.maximum(z2, 0.0) + jnp.log2(1.0 + jnp.exp2(-jnp.abs(z2)))
            if diagonal:
                sp2 = sp2 * mk_ref[0]
            hi = sp2.astype(BF16)
            lo = (sp2 - hi.astype(F32)).astype(BF16)
            sps.append(sp2)
            sufs.append(_dot(jnp.concatenate([hi, lo], axis=1), tri2))
        vt = v_ref[0, pl.ds(k0, tk), :]
        dead = None
        for h in range(N_HEADS):
            blk = slice(h * tq, (h + 1) * tq)
            tile_sum = jnp.broadcast_to(jnp.sum(sps[h], axis=-1, keepdims=True), (tq, LANES))
            if diagonal:
                w = jnp.exp2(zs[h] - sufs[h] + mk_ref[1])
                acc_scr[blk, :] = _dot(w.astype(BF16), vt)
                carry = tile_sum
            else:
                carry = carry_scr[blk, :]
                w = jnp.exp2(zs[h] - sufs[h] - lanes(carry, tk))
                acc_scr[blk, :] += _dot(w.astype(BF16), vt)
                carry = carry + tile_sum
            carry_scr[blk, :] = carry
            head_dead = jnp.min(carry) > SB_DEAD_LOG2
            dead = head_dead if dead is None else dead & head_dead
        return dead.astype(jnp.int32)

    def cond(c):
        i, dead = c
        return (i <= jd) & (dead == 0)

    def body(c):
        return c[0] + 1, tile(c[0], False)

    lax.while_loop(cond, body, (jnp.int32(1), tile(0, True)))
    out_ref[0] = (_unstack_heads(acc_scr[...], tq) * z_ref[0]).astype(BF16)


def _stickbrk(q, k, v, z):
    b, s, _ = q.shape
    tq, tk = min(TQ_SB, s), min(TK_SB, s)
    assert tq == tk
    keep = jnp.arange(tk)[None, :] < jnp.arange(tq)[:, None]
    masks = jnp.stack([jnp.where(keep, 1.0, 0.0), jnp.where(keep, 0.0, NEG)]).astype(F32)
    tok = pl.BlockSpec((1, tq, WIDTH), lambda bi, qi: (bi, qi, 0))
    seq = pl.BlockSpec((1, s, WIDTH), lambda bi, qi: (bi, 0, 0))
    return pl.pallas_call(
        functools.partial(_stickbrk_kernel, tq=tq, tk=tk),
        grid=(b, s // tq),
        in_specs=[tok, seq, seq, tok, pl.BlockSpec(masks.shape, lambda bi, qi: (0, 0, 0))],
        out_specs=tok,
        out_shape=jax.ShapeDtypeStruct((b, s, WIDTH), BF16),
        scratch_shapes=[pltpu.VMEM((N_HEADS * tq, WIDTH), F32), pltpu.VMEM((N_HEADS * tq, LANES), F32)],
        compiler_params=pltpu.CompilerParams(dimension_semantics=("arbitrary", "arbitrary"),
                                             vmem_limit_bytes=VMEM_LIMIT),
        name="stickbrk",
    )(q, k, v, z, masks)


def _s5_disc_kernel(are_ref, aim_ref, ldt_ref, bre_ref, bim_ref, abre_ref, abim_ref, bbre_ref, bbim_ref):
    dt = jnp.exp(ldt_ref[...])
    lr, li = are_ref[...], aim_ref[...]
    mag = jnp.exp(lr * dt)
    ab_re, ab_im = mag * jnp.cos(li * dt), mag * jnp.sin(li * dt)
    den = lr * lr + li * li
    coef_re = ((ab_re - 1.0) * lr + ab_im * li) / den
    coef_im = (ab_im * lr - (ab_re - 1.0) * li) / den
    abre_ref[...] = ab_re
    abim_ref[...] = ab_im
    br, bi = bre_ref[...], bim_ref[...]
    bbre_ref[...] = coef_re[:, None, :] * br - coef_im[:, None, :] * bi
    bbim_ref[...] = coef_re[:, None, :] * bi + coef_im[:, None, :] * br


def _s5_discretise(a_re, a_im, log_dt, b_re, b_im):
    g, p = a_re.shape
    brt, bit = jnp.swapaxes(b_re, 1, 2), jnp.swapaxes(b_im, 1, 2)
    return pl.pallas_call(
        _s5_disc_kernel,
        out_shape=[jax.ShapeDtypeStruct((g, p), F32)] * 2 + [jax.ShapeDtypeStruct(brt.shape, F32)] * 2,
        name="s5_disc",
    )(a_re, a_im, log_dt[:, None], brt, bit)


def _s5_kernel(u_ref, z_ref, bmat_ref, a_ref, cmat_ref, d_ref, gw_ref, gb_ref, out_ref, xs_scr, state_scr,
               *, nb, ts, ns):
    @pl.when(pl.program_id(0) == 0)
    def _():
        state_scr[...] = jnp.zeros((nb, 2 * ns), F32)

    nc = 2 * ns // LANES
    for b in range(nb):
        bu = _dot(u_ref[b].astype(BF16), bmat_ref[...])
        for c in range(nc):
            xs_scr[c, pl.ds(b, ts, stride=nb), :] = bu[:, c * LANES:(c + 1) * LANES]

    a_re = jnp.broadcast_to(a_ref[0:1, :], (nb, ns))
    a_im = jnp.broadcast_to(a_ref[1:2, :], (nb, ns))

    def step(t, state):
        x_re, x_im = state
        r0 = pl.multiple_of(t * nb, nb)
        bu = jnp.concatenate([xs_scr[c, pl.ds(r0, nb), :] for c in range(nc)], axis=1)
        n_re = a_re * x_re - a_im * x_im + bu[:, :ns]
        n_im = a_re * x_im + a_im * x_re + bu[:, ns:]
        for c in range(nc // 2):
            xs_scr[c, pl.ds(r0, nb), :] = n_re[:, c * LANES:(c + 1) * LANES]
            xs_scr[nc // 2 + c, pl.ds(r0, nb), :] = n_im[:, c * LANES:(c + 1) * LANES]
        return n_re, n_im

    st = state_scr[...]
    x_re, x_im = lax.fori_loop(0, ts, step, (st[:, :ns], st[:, ns:]))
    state_scr[...] = jnp.concatenate([x_re, x_im], axis=1)

    width = u_ref.shape[-1]
    ys = []
    for b in range(nb):
        xs = jnp.concatenate([xs_scr[c, pl.ds(b, ts, stride=nb), :] for c in range(nc)], axis=1)
        ys.append((_dot(xs.astype(BF16), cmat_ref[...]) + d_ref[...] * u_ref[b]).astype(BF16))
    for b in range(nb):
        glu = _dot(ys[b], gw_ref[...]) + gb_ref[...]
        out_ref[b] = (glu[:, :width] * _sigmoid(glu[:, width:]) * z_ref[b]).astype(BF16)


def _s5(u, z, bmat, a_rows, cmat, d_row, glu_w, glu_b):
    nb, s, w = u.shape
    ts = min(TS_S5, s)
    ns = a_rows.shape[1]
    full = lambda a: pl.BlockSpec(a.shape, lambda i: (0, 0))
    tok = pl.BlockSpec((nb, ts, w), lambda i: (0, i, 0))
    return pl.pallas_call(
        functools.partial(_s5_kernel, nb=nb, ts=ts, ns=ns),
        grid=(s // ts,),
        in_specs=[tok, tok, full(bmat), full(a_rows), full(cmat), full(d_row), full(glu_w), full(glu_b)],
        out_specs=tok,
        out_shape=jax.ShapeDtypeStruct((nb, s, w), BF16),
        scratch_shapes=[pltpu.VMEM((2 * ns // LANES, ts * nb, LANES), F32), pltpu.VMEM((nb, 2 * ns), F32)],
        compiler_params=pltpu.CompilerParams(dimension_semantics=("arbitrary",), vmem_limit_bytes=VMEM_LIMIT),
        name="s5",
    )(u, z, bmat, a_rows, cmat, d_row, glu_w, glu_b)


def _merge_kernel(x_ref, ng_ref, o0_ref, o1_ref, o2_ref, o3_ref, wm_ref, wb_ref, wo_ref, out_ref):
    x = x_ref[...]
    ms = jnp.mean(x * x, axis=-1, keepdims=True)
    h = (x * lax.rsqrt(ms + NORM_EPS) * ng_ref[...]).astype(BF16)
    d = x.shape[1]
    mixed = None
    for m, o_ref in enumerate((o0_ref, o1_ref, o2_ref, o3_ref)):
        gate = _sigmoid(_dot(h, wm_ref[:, m * d:(m + 1) * d]))
        term = gate * _dot(o_ref[...], wb_ref[m])
        mixed = term if mixed is None else mixed + term
    out_ref[...] = x + _dot(mixed.astype(BF16), wo_ref[...])


def _merge(x2, ng, outs, wm, wb, wo):
    t, d = x2.shape
    tm = min(TM_PROJ, t)
    tok = lambda w: pl.BlockSpec((tm, w), lambda i: (i, 0))
    full = lambda a: pl.BlockSpec(a.shape, lambda i: (0,) * a.ndim)
    return pl.pallas_call(
        _merge_kernel,
        grid=(t // tm,),
        in_specs=[tok(d), full(ng)] + [tok(WIDTH)] * 4 + [full(wm), full(wb), full(wo)],
        out_specs=tok(d),
        out_shape=jax.ShapeDtypeStruct((t, d), F32),
        compiler_params=pltpu.CompilerParams(dimension_semantics=("arbitrary",), vmem_limit_bytes=VMEM_LIMIT),
        name="merge",
    )(x2, ng, *outs, wm, wb, wo)


def _wprep_kernel(w_ref, wqk_ref, wv_ref, wc_ref, wg_ref, wz_ref, wsc_ref, wsb_ref, ws5_ref, wm_ref, *, d):
    w = WIDTH
    o_q, o_kv, o_gate = 0, w, 4 * w
    o_nz = o_gate + 3 * N_HEADS
    o_sc, o_scz, o_sb, o_sbz, o_s5, o_s5z = o_nz + w, o_nz + 4 * w, o_nz + 5 * w, o_nz + 8 * w, o_nz + 9 * w, o_nz + 10 * w
    o_merge = o_nz + 11 * w
    col = lambda a, n: w_ref[0, :, a:a + n]
    kv = lambda i: o_kv + i * (w // 2)
    cat = lambda parts: jnp.concatenate(parts, axis=1).astype(BF16)

    def heads(a):
        return [col(a + h * HEAD_DIM, HEAD_DIM) for h in NSA_HEAD_ORDER]

    wqk_ref[...] = cat(heads(o_q) + [col(kv(2), w // 2), col(kv(4), w // 2)])
    wv_ref[...] = cat([col(kv(3), w // 2), col(kv(5), w // 2)])
    wc_ref[...] = col(kv(0), w).astype(BF16)
    gates = col(o_gate, 3 * N_HEADS)
    wg_ref[...] = cat([jnp.broadcast_to(gates[:, c:c + 1], (gates.shape[0], HEAD_DIM))
                       for c in (br * N_HEADS + h for br in range(3) for h in NSA_HEAD_ORDER)])
    wz_ref[...] = cat(heads(o_nz) + [col(o_scz, w), col(o_sbz, w), col(o_s5z, w)])
    wsc_ref[...] = col(o_sc, 3 * w).astype(BF16)
    wsb_ref[...] = col(o_sb, 3 * w).astype(BF16)
    ws5_ref[...] = col(o_s5, w).astype(BF16)
    wm_ref[...] = col(o_merge, N_HEADS * d).astype(BF16)


def _wprep(w_in, layer):
    _, d, n_in = w_in.shape
    tr = min(TR_WPREP, d)
    widths = [2 * WIDTH, WIDTH, WIDTH, 3 * WIDTH, 4 * WIDTH, 3 * WIDTH, 3 * WIDTH, WIDTH, N_HEADS * d]
    return pl.pallas_call(
        functools.partial(_wprep_kernel, d=d),
        grid=(d // tr,),
        in_specs=[pl.BlockSpec((1, tr, n_in), lambda i: (layer, i, 0))],
        out_specs=[pl.BlockSpec((tr, n), lambda i: (i, 0)) for n in widths],
        out_shape=[jax.ShapeDtypeStruct((d, n), BF16) for n in widths],
        compiler_params=pltpu.CompilerParams(dimension_semantics=("arbitrary",), vmem_limit_bytes=VMEM_LIMIT),
        name="wprep",
    )(w_in)


def _block_diag(blocks):
    g, r, c = blocks.shape
    eye = jnp.eye(g, dtype=blocks.dtype)
    return (eye[:, None, :, None] * blocks[:, :, None, :]).reshape(g * r, g * c)


def _block_diag_pair(w1):
    eye = jnp.eye(NSA_KV_HEADS, dtype=w1.dtype)
    bd = w1[..., :, None, :, None, :] * eye[:, None, :, None]
    return bd.reshape(w1.shape[:-3] + (w1.shape[-3] * NSA_KV_HEADS * HEAD_DIM, NSA_KV_HEADS * HEAD_DIM))


def _layer(x, cos, sin, cos_c, sin_c, n_cmp, norm_g, w_in_all, layer, qk_g, cmp_pe, cmp_w1, cmp_w2, conv_w,
           a_re, a_im, log_dt, b_re, b_im, c_re, c_im, d_skip, glu_w, glu_b, w_branch, w_out):
    b, s, d = x.shape
    w = WIDTH
    *weights, w_merge = _wprep(w_in_all, layer)
    qkg = jnp.concatenate([jnp.tile(qk_g[0], N_HEADS), jnp.tile(qk_g[2], NSA_KV_HEADS),
                           jnp.tile(qk_g[3], NSA_KV_HEADS)])[None, :]
    ng = norm_g[None, :]

    (q, ks, kw, vs, vw, gate, zn, zsb, zs5, sc_o, sbq, sbk, sbv, s5u, kc_in, vc_in) = _inproj(
        x, ng, weights, qkg, cos, sin, conv_w)

    half = NSA_CMP_STRIDE
    two = NSA_CMP_LEN // half
    pe = jnp.tile(cmp_pe.reshape(2, two, half, HEAD_DIM), (1, 1, 1, NSA_KV_HEADS))
    w1 = _block_diag_pair(cmp_w1.reshape(2, two, half, HEAD_DIM, HEAD_DIM))
    w2k, w2v = _block_diag(jnp.stack([cmp_w2[0]] * NSA_KV_HEADS)), _block_diag(jnp.stack([cmp_w2[1]] * NSA_KV_HEADS))
    kc, vct = _compress(kc_in, vc_in, pe, w1.astype(BF16), w2k.astype(BF16), w2v.T.astype(BF16),
                        jnp.tile(qk_g[1], NSA_KV_HEADS)[None, :], cos_c, sin_c)

    ocmp, bias = _cmp_topk(q, kc, vct, n_cmp)
    nsa_o = _selwin(q, bias, ocmp, gate, zn, ks, vs, kw, vw)
    sb_o = _stickbrk(sbq, sbk, sbv, zsb)

    ab_re, ab_im, bb_re, bb_im = _s5_discretise(a_re, a_im, log_dt, b_re, b_im)
    bmat = jnp.concatenate([_block_diag(bb_re), _block_diag(bb_im)], axis=1).astype(BF16)
    cmat = jnp.concatenate([_block_diag(jnp.swapaxes(c_re, 1, 2)),
                            -_block_diag(jnp.swapaxes(c_im, 1, 2))], axis=0).astype(BF16)
    a_rows = jnp.stack([ab_re.reshape(-1), ab_im.reshape(-1)])
    s5_o = _s5(s5u, zs5, bmat, a_rows, cmat, d_skip.reshape(1, -1), glu_w.astype(BF16), glu_b[None, :])

    wb_nsa = w_branch[0].reshape(N_HEADS, HEAD_DIM, d)[jnp.array(NSA_HEAD_ORDER)].reshape(w, d)
    wb = jnp.concatenate([wb_nsa[None], w_branch[1:]], axis=0).astype(BF16)
    out = _merge(x.reshape(b * s, d), ng,
                 [o.reshape(b * s, w) for o in (nsa_o, sc_o, sb_o, s5_o)],
                 w_merge, wb, w_out.astype(BF16))
    return out.reshape(b, s, d)


def kernel(x, positions, norm_g, w_in, nsa_qk_g, nsa_cmp_pe, nsa_cmp_w1, nsa_cmp_w2, sc_conv_w, s5_a_re, s5_a_im,
           s5_log_dt, s5_b_re, s5_b_im, s5_c_re, s5_c_im, s5_d, s5_glu_w, s5_glu_b, w_branch, w_out):
    b, s, _ = x.shape
    assert s % TK_SEL == 0 or s < TK_SB
    cos, sin = _rope_tables(positions.reshape(-1))
    cos, sin = cos.reshape(b, s, LANES), sin.reshape(b, s, LANES)
    nk = s // NSA_CMP_STRIDE
    n_cmp = (s - NSA_CMP_LEN) // NSA_CMP_STRIDE + 1
    pos_c = jnp.concatenate([positions[:, NSA_CMP_LEN - 1::NSA_CMP_STRIDE],
                             jnp.zeros((b, nk - n_cmp), positions.dtype)], axis=1)
    cos_c, sin_c = _rope_tables(pos_c.reshape(-1))
    cos_c, sin_c = cos_c.reshape(b, nk, LANES), sin_c.reshape(b, nk, LANES)
    for l in range(norm_g.shape[0]):
        x = _layer(x, cos, sin, cos_c, sin_c, n_cmp, norm_g[l], w_in, l, nsa_qk_g[l], nsa_cmp_pe[l], nsa_cmp_w1[l],
                   nsa_cmp_w2[l], sc_conv_w[l], s5_a_re[l], s5_a_im[l], s5_log_dt[l], s5_b_re[l], s5_b_im[l],
                   s5_c_re[l], s5_c_im[l], s5_d[l], s5_glu_w[l], s5_glu_b[l], w_branch[l], w_out[l])
    return x
```

```python
import functools
import math

import jax
import jax.numpy as jnp
from jax import lax
from jax.experimental import pallas as pl
from jax.experimental.pallas import tpu as pltpu

F32 = jnp.float32
BF16 = jnp.bfloat16

HEAD_DIM = 64
N_HEADS = 4
WIDTH = N_HEADS * HEAD_DIM
NSA_KV_HEADS = 2
NSA_CMP_LEN = 32
NSA_CMP_STRIDE = 16
NSA_SEL_LEN = 64
NSA_SEL_TOPN = 16
NSA_WINDOW = 512
NSA_FORCE_SCORE = 1.0e4
S5_GROUPS = 16
S5_GROUP_CH = 16
S5_STATE = 64
ROPE_THETA = 10000.0
NORM_EPS = 1e-6
QK_SCALE = HEAD_DIM ** -0.5

LANES = 128
SUBLANES = 8
NEG = -1.0e30
SEL_BIAS = -30000.0
SB_DEAD_LOG2 = 160.0
LOG2_E = 1.4426950408889634
VMEM_LIMIT = 56 * 1024 * 1024

TM_PROJ = 512
TR_WPREP = 256
TT_CMP = 128
TQ_SB = 256
TK_SB = 256
TQ_SEL = 512
TK_SEL = 512
NSA_HEAD_ORDER = (0, 2, 1, 3)
TS_S5 = 128


def _dot(a, b):
    return jnp.dot(a, b, preferred_element_type=F32)


def _dot_nt(a, b):
    return lax.dot_general(a, b, (((1,), (1,)), ((), ())), preferred_element_type=F32)


def _silu(x):
    return x * (1.0 / (1.0 + jnp.exp(-x)))


def _sigmoid(x):
    return 1.0 / (1.0 + jnp.exp(-x))


def _iota(shape, dim):
    return lax.broadcasted_iota(jnp.int32, shape, dim)


def _group_mean_sq(x):
    outs = []
    lane = _iota((x.shape[0], LANES), 1)
    low = lane < HEAD_DIM
    for c in range(x.shape[1] // LANES):
        xc = x[:, c * LANES:(c + 1) * LANES]
        sq = xc * xc
        s_lo = jnp.sum(jnp.where(low, sq, 0.0), axis=-1, keepdims=True)
        s_hi = jnp.sum(jnp.where(low, 0.0, sq), axis=-1, keepdims=True)
        outs.append(jnp.where(low, s_lo, s_hi) * (1.0 / HEAD_DIM))
    return outs[0] if len(outs) == 1 else jnp.concatenate(outs, axis=1)


def _rot_half(y):
    w = y.shape[1]
    first = (_iota(y.shape, 1) & (HEAD_DIM // 2)) == 0
    return jnp.where(first, pltpu.roll(y, w - HEAD_DIM // 2, 1), pltpu.roll(y, HEAD_DIM // 2, 1))


def _norm_rope(x, gain, cos, sin_signed):
    reps = x.shape[1] // LANES
    y = x * lax.rsqrt(_group_mean_sq(x) + NORM_EPS) * gain
    c = cos if reps == 1 else jnp.concatenate([cos] * reps, axis=1)
    s = sin_signed if reps == 1 else jnp.concatenate([sin_signed] * reps, axis=1)
    return y * c + _rot_half(y) * s


def _stack_heads(q):
    head = _iota(q.shape, 1) // HEAD_DIM
    return jnp.concatenate([jnp.where(head == h, q, jnp.zeros_like(q)) for h in range(N_HEADS)], axis=0)


def _unstack_heads(o4, m):
    head = _iota((m, WIDTH), 1) // HEAD_DIM
    out = jnp.zeros((m, WIDTH), F32)
    for h in range(N_HEADS):
        out = jnp.where(head == h, o4[h * m:(h + 1) * m], out)
    return out


def _rope_table_kernel(pos_ref, freq_ref, sign_ref, cos_ref, sin_ref):
    ang = pos_ref[...].astype(F32) * freq_ref[...]
    cos_ref[...] = jnp.cos(ang)
    sin_ref[...] = jnp.sin(ang) * sign_ref[...]


def _rope_tables(pos_flat):
    n = pos_flat.shape[0]
    half = HEAD_DIM // 2
    inv_freq = jnp.power(ROPE_THETA, -jnp.arange(half, dtype=F32) / half)
    freq = jnp.tile(inv_freq, LANES // half)[None, :]
    sign = jnp.tile(jnp.concatenate([-jnp.ones((half,), F32), jnp.ones((half,), F32)]), LANES // HEAD_DIM)[None, :]
    tm = 512 if n % 512 == 0 else n
    return pl.pallas_call(
        _rope_table_kernel,
        grid=(n // tm,),
        in_specs=[pl.BlockSpec((tm, 1), lambda i: (i, 0)),
                  pl.BlockSpec((1, LANES), lambda i: (0, 0)),
                  pl.BlockSpec((1, LANES), lambda i: (0, 0))],
        out_specs=[pl.BlockSpec((tm, LANES), lambda i: (i, 0))] * 2,
        out_shape=[jax.ShapeDtypeStruct((n, LANES), F32)] * 2,
        name="rope_tables",
    )(pos_flat[:, None], freq, sign)


def _inproj_kernel(x_ref, ng_ref, wqk_ref, wv_ref, wc_ref, wg_ref, wz_ref, wsc_ref, wsb_ref, ws5_ref,
                   qkg_ref, cos_ref, sin_ref, convw_ref,
                   q_ref, ks_ref, kw_ref, vs_ref, vw_ref, gate_ref, zn_ref, zsb_ref, zs5_ref,
                   sc_ref, sbq_ref, sbk_ref, sbv_ref, s5u_ref, kc_ref, vc_ref, ubuf_ref, *, tm):
    x = x_ref[0]
    ms = jnp.mean(x * x, axis=-1, keepdims=True)
    h = (x * lax.rsqrt(ms + NORM_EPS) * ng_ref[...]).astype(BF16)

    qk = _norm_rope(_dot(h, wqk_ref[...]), qkg_ref[...], cos_ref[0], sin_ref[0])
    q_ref[0] = (qk[:, :WIDTH] * QK_SCALE).astype(BF16)
    ks_ref[0] = qk[:, WIDTH:WIDTH + LANES].astype(BF16)
    kw_ref[0] = qk[:, WIDTH + LANES:].astype(BF16)

    v = _dot(h, wv_ref[...])
    vs_ref[0] = v[:, :LANES].astype(BF16)
    vw_ref[0] = v[:, LANES:].astype(BF16)
    kvc = _dot(h, wc_ref[...])
    kc_ref[0] = kvc[:, :LANES]
    vc_ref[0] = kvc[:, LANES:]
    gate_ref[0] = _sigmoid(_dot(h, wg_ref[...]))

    z = _silu(_dot(h, wz_ref[...]))
    zn_ref[0] = z[:, :WIDTH]
    zsb_ref[0] = z[:, 2 * WIDTH:3 * WIDTH]
    zs5_ref[0] = z[:, 3 * WIDTH:]

    bcx = _dot(h, wsc_ref[...])
    u = bcx[:, WIDTH:2 * WIDTH] * bcx[:, 2 * WIDTH:]

    @pl.when(pl.program_id(1) == 0)
    def _():
        ubuf_ref[0:SUBLANES, :] = jnp.zeros((SUBLANES, WIDTH), F32)

    ubuf_ref[SUBLANES:SUBLANES + tm, :] = u
    u1 = ubuf_ref[SUBLANES - 1:SUBLANES - 1 + tm, :]
    u2 = ubuf_ref[SUBLANES - 2:SUBLANES - 2 + tm, :]
    cw = convw_ref[...]
    y = cw[2:3, :] * u + cw[1:2, :] * u1 + cw[0:1, :] * u2
    ubuf_ref[0:SUBLANES, :] = ubuf_ref[tm:tm + SUBLANES, :]
    sc_ref[0] = (bcx[:, :WIDTH] * y * z[:, WIDTH:2 * WIDTH]).astype(BF16)

    sb = _dot(h, wsb_ref[...])
    sbq_ref[0] = (sb[:, :WIDTH] * QK_SCALE).astype(BF16)
    sbk_ref[0] = sb[:, WIDTH:2 * WIDTH].astype(BF16)
    sbv_ref[0] = sb[:, 2 * WIDTH:].astype(BF16)
    s5u_ref[0] = _dot(h, ws5_ref[...])


def _inproj(x, ng, weights, qkg, cos, sin, convw):
    b, s, d = x.shape
    tm = min(TM_PROJ, s)
    full2 = lambda a: pl.BlockSpec(a.shape, lambda bi, si: (0, 0))
    tok = lambda w: pl.BlockSpec((1, tm, w), lambda bi, si: (bi, si, 0))
    out_widths = [(WIDTH, BF16)] + [(LANES, BF16)] * 4 + [(3 * WIDTH, F32)] + [(WIDTH, F32)] * 3 \
        + [(WIDTH, BF16)] * 4 + [(WIDTH, F32)] + [(LANES, F32)] * 2
    return pl.pallas_call(
        functools.partial(_inproj_kernel, tm=tm),
        grid=(b, s // tm),
        in_specs=[tok(d), full2(ng)] + [full2(w) for w in weights] + [full2(qkg), tok(LANES), tok(LANES), full2(convw)],
        out_specs=[tok(w) for w, _ in out_widths],
        out_shape=[jax.ShapeDtypeStruct((b, s, w), dt) for w, dt in out_widths],
        scratch_shapes=[pltpu.VMEM((tm + 2 * SUBLANES, WIDTH), F32)],
        compiler_params=pltpu.CompilerParams(dimension_semantics=("arbitrary", "arbitrary"),
                                             vmem_limit_bytes=VMEM_LIMIT),
        name="inproj",
    )(x, ng, *weights, qkg, cos, sin, convw)


def _compress_kernel(kc_ref, vc_ref, pe_ref, w1_ref, w2k_ref, w2v_ref, g_ref, cos_ref, sin_ref, kco_ref, vct_ref, *, nk):
    half = NSA_CMP_STRIDE
    for j, src in ((0, kc_ref), (1, vc_ref)):
        tok = [src[0, pl.ds(i, nk, stride=half), :] for i in range(half)]
        lo = jnp.concatenate([(tok[i] + pe_ref[j, 0, i:i + 1, :]).astype(BF16) for i in range(half)], axis=1)
        hi = jnp.concatenate([(tok[i] + pe_ref[j, 1, i:i + 1, :]).astype(BF16) for i in range(half)], axis=1)
        hid = _dot(lo, w1_ref[j, 0]) + pltpu.roll(_dot(hi, w1_ref[j, 1]), nk - 1, 0)
        act = _silu(hid).astype(BF16)
        if j == 0:
            kco_ref[0] = _norm_rope(_dot(act, w2k_ref[...]), g_ref[...], cos_ref[0], sin_ref[0]).astype(BF16)
        else:
            vct_ref[0] = _dot_nt(w2v_ref[...], act).astype(BF16)


def _compress(kc, vc, pe, w1, w2k, w2v, g, cos, sin):
    b, s, _ = kc.shape
    nk = s // NSA_CMP_STRIDE
    full = lambda a: pl.BlockSpec(a.shape, lambda bi: (0,) * a.ndim)
    seq = pl.BlockSpec((1, s, LANES), lambda bi: (bi, 0, 0))
    tab = pl.BlockSpec((1, nk, LANES), lambda bi: (bi, 0, 0))
    return pl.pallas_call(
        functools.partial(_compress_kernel, nk=nk),
        grid=(b,),
        in_specs=[seq, seq, full(pe), full(w1), full(w2k), full(w2v), full(g), tab, tab],
        out_specs=[pl.BlockSpec((1, nk, LANES), lambda bi: (bi, 0, 0)),
                   pl.BlockSpec((1, LANES, nk), lambda bi: (bi, 0, 0))],
        out_shape=[jax.ShapeDtypeStruct((b, nk, LANES), BF16), jax.ShapeDtypeStruct((b, LANES, nk), BF16)],
        compiler_params=pltpu.CompilerParams(dimension_semantics=("arbitrary",), vmem_limit_bytes=VMEM_LIMIT),
        name="compress",
    )(kc, vc, pe, w1, w2k, w2v, g, cos, sin)


def _cmp_topk_kernel(q_ref, kc_ref, vct_ref, ocmp_ref, bias_ref, p_scr, imp_scr, rank_scr,
                     *, tt, nk, n_cmp, n_sel, top_n):
    t0 = pl.program_id(1) * tt
    kc = kc_ref[0]
    half = _iota((tt, LANES), 1) // HEAD_DIM
    nrow = _iota((nk, tt), 0)
    tcol = t0 + _iota((nk, tt), 1)
    valid_t = (nrow * NSA_CMP_STRIDE + (NSA_CMP_LEN - 1) <= tcol) & (nrow < n_cmp)
    brow = _iota((n_sel, tt), 0)
    tsel = t0 + _iota((n_sel, tt), 1)
    cur = tsel // NSA_SEL_LEN
    forced = (brow == 0) | (brow == cur) | (brow == cur - 1)
    in_past = brow * NSA_SEL_LEN <= tsel
    last_block = (t0 + tt - 1) // NSA_SEL_LEN
    sub = _iota((SUBLANES, tt), 0)
    heads_per_kv = N_HEADS // NSA_KV_HEADS
    scores = []
    for h in range(N_HEADS):
        qc = q_ref[0, :, (h % 2) * LANES:(h % 2 + 1) * LANES]
        scores.append(_dot_nt(kc, jnp.where(half == h // heads_per_kv, qc, jnp.zeros_like(qc))))
    probs = []
    for st in scores:
        st = jnp.where(valid_t, st, NEG)
        et = jnp.where(valid_t, jnp.exp(st - jnp.max(st, axis=0, keepdims=True)), 0.0)
        probs.append(et * (1.0 / jnp.maximum(jnp.sum(et, axis=0, keepdims=True), 1e-30)))
    outs = [_dot(vct_ref[0, (h // heads_per_kv) * HEAD_DIM:(h // heads_per_kv + 1) * HEAD_DIM, :],
                 probs[h].astype(BF16)) for h in range(N_HEADS)]
    groups = n_sel // SUBLANES
    imp_g = []
    for kh in range(NSA_KV_HEADS):
        psum = probs[kh * heads_per_kv]
        for g in range(1, heads_per_kv):
            psum = psum + probs[kh * heads_per_kv + g]
        p_scr[kh, SUBLANES:SUBLANES + nk, :] = psum
        p_scr[kh, 0:SUBLANES, :] = jnp.zeros((SUBLANES, tt), F32)
        ratio = NSA_SEL_LEN // NSA_CMP_STRIDE
        imp = p_scr[kh, pl.ds(SUBLANES - 1, n_sel, stride=ratio), :]
        for k in range(1, NSA_CMP_LEN // NSA_CMP_STRIDE + ratio - 1):
            imp = imp + p_scr[kh, pl.ds(SUBLANES - 1 + k, n_sel, stride=ratio), :]
        imp = jnp.where(forced, NSA_FORCE_SCORE, jnp.where(in_past, imp, -NSA_FORCE_SCORE))
        imp_scr[kh] = imp
        imp_g.append([imp[k * SUBLANES:(k + 1) * SUBLANES] for k in range(groups)])

    rank_scr[...] = jnp.zeros((NSA_KV_HEADS, n_sel, tt), F32)
    for m in range(groups):
        @pl.when(m * SUBLANES <= last_block)
        def _(m=m):
            for kh in range(NSA_KV_HEADS):
                for k in range(m + 1):
                    mine = imp_g[kh][k]
                    cnt = jnp.zeros((SUBLANES, tt), F32)
                    for c in ([m] if k < m else range(m + 1)):
                        for r in range(SUBLANES):
                            row = imp_scr[kh, c * SUBLANES + r:c * SUBLANES + r + 1, :]
                            if k > c:
                                beats = row >= mine
                            elif k < c:
                                beats = row > mine
                            else:
                                beats = (row > mine) | ((row == mine) & (sub > r))
                            cnt = cnt + jnp.where(beats, 1.0, 0.0)
                    rank_scr[kh, k * SUBLANES:(k + 1) * SUBLANES, :] += cnt

    biases = [jnp.where(rank_scr[kh] < top_n, 0.0, SEL_BIAS) for kh in range(NSA_KV_HEADS)]
    ocmp_ref[0] = jnp.concatenate([outs[h] for h in NSA_HEAD_ORDER], axis=0).T
    pad = jnp.zeros((LANES - n_sel, tt), F32)
    bias_ref[0] = jnp.concatenate([biases[0], pad, biases[1], pad], axis=0).T.astype(BF16)


def _cmp_topk(q, kc, vct, n_cmp):
    b, s, _ = q.shape
    nk = kc.shape[1]
    n_sel = s // NSA_SEL_LEN
    tt = min(TT_CMP, s)
    kern = functools.partial(_cmp_topk_kernel, tt=tt, nk=nk, n_cmp=n_cmp, n_sel=n_sel, top_n=min(NSA_SEL_TOPN, n_sel))
    return pl.pallas_call(
        kern,
        grid=(b, s // tt),
        in_specs=[pl.BlockSpec((1, tt, WIDTH), lambda bi, ti: (bi, ti, 0)),
                  pl.BlockSpec((1, nk, LANES), lambda bi, ti: (bi, 0, 0)),
                  pl.BlockSpec((1, LANES, nk), lambda bi, ti: (bi, 0, 0))],
        out_specs=[pl.BlockSpec((1, tt, WIDTH), lambda bi, ti: (bi, ti, 0))] * 2,
        out_shape=[jax.ShapeDtypeStruct((b, s, WIDTH), F32), jax.ShapeDtypeStruct((b, s, WIDTH), BF16)],
        scratch_shapes=[pltpu.VMEM((NSA_KV_HEADS, nk + SUBLANES, tt), F32), pltpu.VMEM((NSA_KV_HEADS, n_sel, tt), F32),
                        pltpu.VMEM((NSA_KV_HEADS, n_sel, tt), F32)],
        compiler_params=pltpu.CompilerParams(dimension_semantics=("arbitrary", "arbitrary"),
                                             vmem_limit_bytes=VMEM_LIMIT),
        name="cmp_topk",
    )(q, kc, vct)


CAUSAL, WINDOW = 0, 1


def _selwin_kernel(q_ref, bias_ref, ocmp_ref, gate_ref, zn_ref, ks_ref, vs_ref, kw_ref, vw_ref, e_ref, cm_ref, out_ref,
                   acc_scr, m_scr, *, tq, tk):
    t0 = pl.program_id(1) * tq
    jd = t0 // tk
    heads_per_kv = N_HEADS // NSA_KV_HEADS
    half = _iota((tq, LANES), 1) // HEAD_DIM
    q_win, q_sel = [], []
    for h in range(N_HEADS):
        qc = q_ref[0, :, (h % 2) * LANES:(h % 2 + 1) * LANES]
        qh = jnp.where(half == h // heads_per_kv, qc, jnp.zeros_like(qc))
        q_win.append(qh)
        q_sel.append(jnp.concatenate([qh, bias_ref[0, :, (h // heads_per_kv) * LANES:(h // heads_per_kv + 1) * LANES]],
                                     axis=1))
    ones = jnp.ones((tk, LANES), BF16)
    lanes = lambda x, n: jnp.concatenate([x] * (n // LANES), axis=1)

    def tile(i, k_ref, v_ref, selected, masking, first):
        k0 = pl.multiple_of((jd - i) * tk, tk)
        kt = k_ref[0, pl.ds(k0, tk), :]
        if selected:
            kt = jnp.concatenate([kt, e_ref[pl.ds(k0, tk), :]], axis=1)
        scores = [_dot_nt((q_sel if selected else q_win)[h], kt) for h in range(N_HEADS)]
        vt = jnp.concatenate([v_ref[0, pl.ds(k0, tk), :], ones], axis=1)
        ps, alphas = [], []
        for h, s in enumerate(scores):
            blk = slice(h * tq, (h + 1) * tq)
            if masking is not None:
                s = s + cm_ref[masking]
            s_max = jnp.max(s, axis=-1, keepdims=True)
            if first:
                m_new = jnp.broadcast_to(s_max, (tq, LANES))
            else:
                m_old = m_scr[blk, :]
                m_new = jnp.maximum(m_old, s_max)
                alphas.append(lanes(jnp.exp(m_old - m_new), WIDTH))
            ps.append(jnp.exp(s - lanes(m_new, tk)).astype(BF16))
            m_scr[blk, :] = m_new
        pv = _dot(jnp.concatenate(ps, axis=0), vt)
        acc_scr[...] = pv if first else jnp.concatenate(alphas, axis=0) * acc_scr[...] + pv

    def result():
        res = [acc_scr[h * tq:(h + 1) * tq, :LANES] * (1.0 / acc_scr[h * tq:(h + 1) * tq, LANES:])
               for h in range(N_HEADS)]
        return jnp.concatenate([jnp.where(half == 0, res[c], res[c + heads_per_kv]) for c in range(2)], axis=1)

    tile(0, ks_ref, vs_ref, True, CAUSAL, True)

    def sel_body(i, carry):
        tile(i, ks_ref, vs_ref, True, None, False)
        return carry

    lax.fori_loop(1, jd + 1, sel_body, 0)
    o_slc = result()

    tile(0, kw_ref, vw_ref, False, CAUSAL, True)

    @pl.when(jd >= 1)
    def _():
        tile(1, kw_ref, vw_ref, False, WINDOW, False)

    o_win = result()
    gate = gate_ref[0]
    o = gate[:, :WIDTH] * ocmp_ref[0] + gate[:, WIDTH:2 * WIDTH] * o_slc + gate[:, 2 * WIDTH:] * o_win
    out_ref[0] = (o * zn_ref[0]).astype(BF16)


def _selwin(q, bias, ocmp, gate, zn, ks, vs, kw, vw):
    b, s, _ = q.shape
    tq, tk = min(TQ_SEL, s), min(TK_SEL, s)
    assert NSA_WINDOW == tk and tk == tq and s // NSA_SEL_LEN <= HEAD_DIM
    member = (jnp.arange(s)[:, None] // NSA_SEL_LEN == jnp.arange(LANES)[None, :]).astype(BF16)
    row, col = jnp.arange(tq)[:, None], jnp.arange(tk)[None, :]
    cmask = jnp.where(jnp.stack([col <= row, col > row]), 0.0, NEG).astype(F32)
    tok = lambda w: pl.BlockSpec((1, tq, w), lambda bi, qi: (bi, qi, 0))
    seq = pl.BlockSpec((1, s, LANES), lambda bi, qi: (bi, 0, 0))
    return pl.pallas_call(
        functools.partial(_selwin_kernel, tq=tq, tk=tk),
        grid=(b, s // tq),
        in_specs=[tok(WIDTH), tok(WIDTH), tok(WIDTH), tok(3 * WIDTH), tok(WIDTH), seq, seq, seq, seq,
                  pl.BlockSpec(member.shape, lambda bi, qi: (0, 0)),
                  pl.BlockSpec(cmask.shape, lambda bi, qi: (0, 0, 0))],
        out_specs=tok(WIDTH),
        out_shape=jax.ShapeDtypeStruct((b, s, WIDTH), BF16),
        scratch_shapes=[pltpu.VMEM((N_HEADS * tq, WIDTH), F32), pltpu.VMEM((N_HEADS * tq, LANES), F32)],
        compiler_params=pltpu.CompilerParams(dimension_semantics=("arbitrary", "arbitrary"),
                                             vmem_limit_bytes=VMEM_LIMIT),
        name="selwin",
    )(q, bias, ocmp, gate, zn, ks, vs, kw, vw, member, cmask)


def _stickbrk_kernel(q_ref, k_ref, v_ref, z_ref, mk_ref, out_ref, acc_scr, carry_scr, *, tq, tk):
    t0 = pl.program_id(1) * tq
    jd = t0 // tk
    q = q_ref[0]
    head = _iota(q.shape, 1) // HEAD_DIM
    qh = [jnp.where(head == h, q, jnp.zeros_like(q)) for h in range(N_HEADS)]
    tri = jnp.where(_iota((tk, tk), 0) >= _iota((tk, tk), 1), 1.0, 0.0).astype(BF16)
    tri2 = jnp.concatenate([tri, tri], axis=0)
    lanes = lambda x, n: jnp.concatenate([x] * (n // LANES), axis=1)

    def tile(i, diagonal):
        k0 = pl.multiple_of((jd - i) * tk, tk)
        kt = k_ref[0, pl.ds(k0, tk), :]
        zs = [_dot_nt(qh[h], kt) * LOG2_E for h in range(N_HEADS)]
        sps, sufs = [], []
        for z2 in zs:
            sp2 = jnp.maximum(z2, 0.0) + jnp.log2(1.0 + jnp.exp2(-jnp.abs(z2)))
            if diagonal:
                sp2 = sp2 * mk_ref[0]
            hi = sp2.astype(BF16)
            lo = (sp2 - hi.astype(F32)).astype(BF16)
            sps.append(sp2)
            sufs.append(_dot(jnp.concatenate([hi, lo], axis=1), tri2))
        vt = v_ref[0, pl.ds(k0, tk), :]
        dead = None
        for h in range(N_HEADS):
            blk = slice(h * tq, (h + 1) * tq)
            tile_sum = jnp.broadcast_to(jnp.sum(sps[h], axis=-1, keepdims=True), (tq, LANES))
            if diagonal:
                w = jnp.exp2(zs[h] - sufs[h] + mk_ref[1])
                acc_scr[blk, :] = _dot(w.astype(BF16), vt)
                carry = tile_sum
            else:
                carry = carry_scr[blk, :]
                w = jnp.exp2(zs[h] - sufs[h] - lanes(carry, tk))
                acc_scr[blk, :] += _dot(w.astype(BF16), vt)
                carry = carry + tile_sum
            carry_scr[blk, :] = carry
            head_dead = jnp.min(carry) > SB_DEAD_LOG2
            dead = head_dead if dead is None else dead & head_dead
        return dead.astype(jnp.int32)

    def cond(c):
        i, dead = c
        return (i <= jd) & (dead == 0)

    def body(c):
        return c[0] + 1, tile(c[0], False)

    lax.while_loop(cond, body, (jnp.int32(1), tile(0, True)))
    out_ref[0] = (_unstack_heads(acc_scr[...], tq) * z_ref[0]).astype(BF16)


def _stickbrk(q, k, v, z):
    b, s, _ = q.shape
    tq, tk = min(TQ_SB, s), min(TK_SB, s)
    assert tq == tk
    keep = jnp.arange(tk)[None, :] < jnp.arange(tq)[:, None]
    masks = jnp.stack([jnp.where(keep, 1.0, 0.0), jnp.where(keep, 0.0, NEG)]).astype(F32)
    tok = pl.BlockSpec((1, tq, WIDTH), lambda bi, qi: (bi, qi, 0))
    seq = pl.BlockSpec((1, s, WIDTH), lambda bi, qi: (bi, 0, 0))
    return pl.pallas_call(
        functools.partial(_stickbrk_kernel, tq=tq, tk=tk),
        grid=(b, s // tq),
        in_specs=[tok, seq, seq, tok, pl.BlockSpec(masks.shape, lambda bi, qi: (0, 0, 0))],
        out_specs=tok,
        out_shape=jax.ShapeDtypeStruct((b, s, WIDTH), BF16),
        scratch_shapes=[pltpu.VMEM((N_HEADS * tq, WIDTH), F32), pltpu.VMEM((N_HEADS * tq, LANES), F32)],
        compiler_params=pltpu.CompilerParams(dimension_semantics=("arbitrary", "arbitrary"),
                                             vmem_limit_bytes=VMEM_LIMIT),
        name="stickbrk",
    )(q, k, v, z, masks)


def _s5_disc_kernel(are_ref, aim_ref, ldt_ref, bre_ref, bim_ref, abre_ref, abim_ref, bbre_ref, bbim_ref):
    dt = jnp.exp(ldt_ref[...])
    lr, li = are_ref[...], aim_ref[...]
    mag = jnp.exp(lr * dt)
    ab_re, ab_im = mag * jnp.cos(li * dt), mag * jnp.sin(li * dt)
    den = lr * lr + li * li
    coef_re = ((ab_re - 1.0) * lr + ab_im * li) / den
    coef_im = (ab_im * lr - (ab_re - 1.0) * li) / den
    abre_ref[...] = ab_re
    abim_ref[...] = ab_im
    br, bi = bre_ref[...], bim_ref[...]
    bbre_ref[...] = coef_re[:, None, :] * br - coef_im[:, None, :] * bi
    bbim_ref[...] = coef_re[:, None, :] * bi + coef_im[:, None, :] * br


def _s5_discretise(a_re, a_im, log_dt, b_re, b_im):
    g, p = a_re.shape
    brt, bit = jnp.swapaxes(b_re, 1, 2), jnp.swapaxes(b_im, 1, 2)
    return pl.pallas_call(
        _s5_disc_kernel,
        out_shape=[jax.ShapeDtypeStruct((g, p), F32)] * 2 + [jax.ShapeDtypeStruct(brt.shape, F32)] * 2,
        name="s5_disc",
    )(a_re, a_im, log_dt[:, None], brt, bit)


def _s5_kernel(u_ref, z_ref, bmat_ref, a_ref, cmat_ref, d_ref, gw_ref, gb_ref, out_ref, xs_scr, state_scr,
               *, nb, ts, ns):
    @pl.when(pl.program_id(0) == 0)
    def _():
        state_scr[...] = jnp.zeros((nb, 2 * ns), F32)

    nc = 2 * ns // LANES
    for b in range(nb):
        bu = _dot(u_ref[b].astype(BF16), bmat_ref[...])
        for c in range(nc):
            xs_scr[c, pl.ds(b, ts, stride=nb), :] = bu[:, c * LANES:(c + 1) * LANES]

    a_re = jnp.broadcast_to(a_ref[0:1, :], (nb, ns))
    a_im = jnp.broadcast_to(a_ref[1:2, :], (nb, ns))

    def step(t, state):
        x_re, x_im = state
        r0 = pl.multiple_of(t * nb, nb)
        bu = jnp.concatenate([xs_scr[c, pl.ds(r0, nb), :] for c in range(nc)], axis=1)
        n_re = a_re * x_re - a_im * x_im + bu[:, :ns]
        n_im = a_re * x_im + a_im * x_re + bu[:, ns:]
        for c in range(nc // 2):
            xs_scr[c, pl.ds(r0, nb), :] = n_re[:, c * LANES:(c + 1) * LANES]
            xs_scr[nc // 2 + c, pl.ds(r0, nb), :] = n_im[:, c * LANES:(c + 1) * LANES]
        return n_re, n_im

    st = state_scr[...]
    x_re, x_im = lax.fori_loop(0, ts, step, (st[:, :ns], st[:, ns:]), unroll=4)
    state_scr[...] = jnp.concatenate([x_re, x_im], axis=1)

    width = u_ref.shape[-1]
    ys = []
    for b in range(nb):
        xs = jnp.concatenate([xs_scr[c, pl.ds(b, ts, stride=nb), :] for c in range(nc)], axis=1)
        ys.append((_dot(xs.astype(BF16), cmat_ref[...]) + d_ref[...] * u_ref[b]).astype(BF16))
    for b in range(nb):
        glu = _dot(ys[b], gw_ref[...]) + gb_ref[...]
        out_ref[b] = (glu[:, :width] * _sigmoid(glu[:, width:]) * z_ref[b]).astype(BF16)


def _s5(u, z, bmat, a_rows, cmat, d_row, glu_w, glu_b):
    nb, s, w = u.shape
    ts = min(TS_S5, s)
    ns = a_rows.shape[1]
    full = lambda a: pl.BlockSpec(a.shape, lambda i: (0, 0))
    tok = pl.BlockSpec((nb, ts, w), lambda i: (0, i, 0))
    return pl.pallas_call(
        functools.partial(_s5_kernel, nb=nb, ts=ts, ns=ns),
        grid=(s // ts,),
        in_specs=[tok, tok, full(bmat), full(a_rows), full(cmat), full(d_row), full(glu_w), full(glu_b)],
        out_specs=tok,
        out_shape=jax.ShapeDtypeStruct((nb, s, w), BF16),
        scratch_shapes=[pltpu.VMEM((2 * ns // LANES, ts * nb, LANES), F32), pltpu.VMEM((nb, 2 * ns), F32)],
        compiler_params=pltpu.CompilerParams(dimension_semantics=("arbitrary",), vmem_limit_bytes=VMEM_LIMIT),
        name="s5",
    )(u, z, bmat, a_rows, cmat, d_row, glu_w, glu_b)


def _merge_kernel(x_ref, ng_ref, o0_ref, o1_ref, o2_ref, o3_ref, wm_ref, wb_ref, wo_ref, out_ref):
    x = x_ref[...]
    ms = jnp.mean(x * x, axis=-1, keepdims=True)
    h = (x * lax.rsqrt(ms + NORM_EPS) * ng_ref[...]).astype(BF16)
    d = x.shape[1]
    mixed = None
    for m, o_ref in enumerate((o0_ref, o1_ref, o2_ref, o3_ref)):
        gate = _sigmoid(_dot(h, wm_ref[:, m * d:(m + 1) * d]))
        term = gate * _dot(o_ref[...], wb_ref[m])
        mixed = term if mixed is None else mixed + term
    out_ref[...] = x + _dot(mixed.astype(BF16), wo_ref[...])


def _merge(x2, ng, outs, wm, wb, wo):
    t, d = x2.shape
    tm = min(TM_PROJ, t)
    tok = lambda w: pl.BlockSpec((tm, w), lambda i: (i, 0))
    full = lambda a: pl.BlockSpec(a.shape, lambda i: (0,) * a.ndim)
    return pl.pallas_call(
        _merge_kernel,
        grid=(t // tm,),
        in_specs=[tok(d), full(ng)] + [tok(WIDTH)] * 4 + [full(wm), full(wb), full(wo)],
        out_specs=tok(d),
        out_shape=jax.ShapeDtypeStruct((t, d), F32),
        compiler_params=pltpu.CompilerParams(dimension_semantics=("arbitrary",), vmem_limit_bytes=VMEM_LIMIT),
        name="merge",
    )(x2, ng, *outs, wm, wb, wo)


def _wprep_kernel(w_ref, wqk_ref, wv_ref, wc_ref, wg_ref, wz_ref, wsc_ref, wsb_ref, ws5_ref, wm_ref, *, d):
    w = WIDTH
    o_q, o_kv, o_gate = 0, w, 4 * w
    o_nz = o_gate + 3 * N_HEADS
    o_sc, o_scz, o_sb, o_sbz, o_s5, o_s5z = o_nz + w, o_nz + 4 * w, o_nz + 5 * w, o_nz + 8 * w, o_nz + 9 * w, o_nz + 10 * w
    o_merge = o_nz + 11 * w
    col = lambda a, n: w_ref[0, :, a:a + n]
    kv = lambda i: o_kv + i * (w // 2)
    cat = lambda parts: jnp.concatenate(parts, axis=1).astype(BF16)

    def heads(a):
        return [col(a + h * HEAD_DIM, HEAD_DIM) for h in NSA_HEAD_ORDER]

    wqk_ref[...] = cat(heads(o_q) + [col(kv(2), w // 2), col(kv(4), w // 2)])
    wv_ref[...] = cat([col(kv(3), w // 2), col(kv(5), w // 2)])
    wc_ref[...] = col(kv(0), w).astype(BF16)
    gates = col(o_gate, 3 * N_HEADS)
    wg_ref[...] = cat([jnp.broadcast_to(gates[:, c:c + 1], (gates.shape[0], HEAD_DIM))
                       for c in (br * N_HEADS + h for br in range(3) for h in NSA_HEAD_ORDER)])
    wz_ref[...] = cat(heads(o_nz) + [col(o_scz, w), col(o_sbz, w), col(o_s5z, w)])
    wsc_ref[...] = col(o_sc, 3 * w).astype(BF16)
    wsb_ref[...] = col(o_sb, 3 * w).astype(BF16)
    ws5_ref[...] = col(o_s5, w).astype(BF16)
    wm_ref[...] = col(o_merge, N_HEADS * d).astype(BF16)


def _wprep(w_in, layer):
    _, d, n_in = w_in.shape
    tr = min(TR_WPREP, d)
    widths = [2 * WIDTH, WIDTH, WIDTH, 3 * WIDTH, 4 * WIDTH, 3 * WIDTH, 3 * WIDTH, WIDTH, N_HEADS * d]
    return pl.pallas_call(
        functools.partial(_wprep_kernel, d=d),
        grid=(d // tr,),
        in_specs=[pl.BlockSpec((1, tr, n_in), lambda i: (layer, i, 0))],
        out_specs=[pl.BlockSpec((tr, n), lambda i: (i, 0)) for n in widths],
        out_shape=[jax.ShapeDtypeStruct((d, n), BF16) for n in widths],
        compiler_params=pltpu.CompilerParams(dimension_semantics=("arbitrary",), vmem_limit_bytes=VMEM_LIMIT),
        name="wprep",
    )(w_in)


def _block_diag(blocks):
    g, r, c = blocks.shape
    eye = jnp.eye(g, dtype=blocks.dtype)
    return (eye[:, None, :, None] * blocks[:, :, None, :]).reshape(g * r, g * c)


def _block_diag_pair(w1):
    eye = jnp.eye(NSA_KV_HEADS, dtype=w1.dtype)
    bd = w1[..., :, None, :, None, :] * eye[:, None, :, None]
    return bd.reshape(w1.shape[:-3] + (w1.shape[-3] * NSA_KV_HEADS * HEAD_DIM, NSA_KV_HEADS * HEAD_DIM))


def _layer(x, cos, sin, cos_c, sin_c, n_cmp, norm_g, w_in_all, layer, qk_g, cmp_pe, cmp_w1, cmp_w2, conv_w,
           a_re, a_im, log_dt, b_re, b_im, c_re, c_im, d_skip, glu_w, glu_b, w_branch, w_out):
    b, s, d = x.shape
    w = WIDTH
    *weights, w_merge = _wprep(w_in_all, layer)
    qkg = jnp.concatenate([jnp.tile(qk_g[0], N_HEADS), jnp.tile(qk_g[2], NSA_KV_HEADS),
                           jnp.tile(qk_g[3], NSA_KV_HEADS)])[None, :]
    ng = norm_g[None, :]

    (q, ks, kw, vs, vw, gate, zn, zsb, zs5, sc_o, sbq, sbk, sbv, s5u, kc_in, vc_in) = _inproj(
        x, ng, weights, qkg, cos, sin, conv_w)

    half = NSA_CMP_STRIDE
    two = NSA_CMP_LEN // half
    pe = jnp.tile(cmp_pe.reshape(2, two, half, HEAD_DIM), (1, 1, 1, NSA_KV_HEADS))
    w1 = _block_diag_pair(cmp_w1.reshape(2, two, half, HEAD_DIM, HEAD_DIM))
    w2k, w2v = _block_diag(jnp.stack([cmp_w2[0]] * NSA_KV_HEADS)), _block_diag(jnp.stack([cmp_w2[1]] * NSA_KV_HEADS))
    kc, vct = _compress(kc_in, vc_in, pe, w1.astype(BF16), w2k.astype(BF16), w2v.T.astype(BF16),
                        jnp.tile(qk_g[1], NSA_KV_HEADS)[None, :], cos_c, sin_c)

    ocmp, bias = _cmp_topk(q, kc, vct, n_cmp)
    nsa_o = _selwin(q, bias, ocmp, gate, zn, ks, vs, kw, vw)
    sb_o = _stickbrk(sbq, sbk, sbv, zsb)

    ab_re, ab_im, bb_re, bb_im = _s5_discretise(a_re, a_im, log_dt, b_re, b_im)
    bmat = jnp.concatenate([_block_diag(bb_re), _block_diag(bb_im)], axis=1).astype(BF16)
    cmat = jnp.concatenate([_block_diag(jnp.swapaxes(c_re, 1, 2)),
                            -_block_diag(jnp.swapaxes(c_im, 1, 2))], axis=0).astype(BF16)
    a_rows = jnp.stack([ab_re.reshape(-1), ab_im.reshape(-1)])
    s5_o = _s5(s5u, zs5, bmat, a_rows, cmat, d_skip.reshape(1, -1), glu_w.astype(BF16), glu_b[None, :])

    wb_nsa = w_branch[0].reshape(N_HEADS, HEAD_DIM, d)[jnp.array(NSA_HEAD_ORDER)].reshape(w, d)
    wb = jnp.concatenate([wb_nsa[None], w_branch[1:]], axis=0).astype(BF16)
    out = _merge(x.reshape(b * s, d), ng,
                 [o.reshape(b * s, w) for o in (nsa_o, sc_o, sb_o, s5_o)],
                 w_merge, wb, w_out.astype(BF16))
    return out.reshape(b, s, d)


def kernel(x, positions, norm_g, w_in, nsa_qk_g, nsa_cmp_pe, nsa_cmp_w1, nsa_cmp_w2, sc_conv_w, s5_a_re, s5_a_im,
           s5_log_dt, s5_b_re, s5_b_im, s5_c_re, s5_c_im, s5_d, s5_glu_w, s5_glu_b, w_branch, w_out):
    b, s, _ = x.shape
    assert s % TK_SEL == 0 or s < TK_SB
    cos, sin = _rope_tables(positions.reshape(-1))
    cos, sin = cos.reshape(b, s, LANES), sin.reshape(b, s, LANES)
    nk = s // NSA_CMP_STRIDE
    n_cmp = (s - NSA_CMP_LEN) // NSA_CMP_STRIDE + 1
    pos_c = jnp.concatenate([positions[:, NSA_CMP_LEN - 1::NSA_CMP_STRIDE],
                             jnp.zeros((b, nk - n_cmp), positions.dtype)], axis=1)
    cos_c, sin_c = _rope_tables(pos_c.reshape(-1))
    cos_c, sin_c = cos_c.reshape(b, nk, LANES), sin_c.reshape(b, nk, LANES)
    for l in range(norm_g.shape[0]):
        x = _layer(x, cos, sin, cos_c, sin_c, n_cmp, norm_g[l], w_in, l, nsa_qk_g[l], nsa_cmp_pe[l], nsa_cmp_w1[l],
                   nsa_cmp_w2[l], sc_conv_w[l], s5_a_re[l], s5_a_im[l], s5_log_dt[l], s5_b_re[l], s5_b_im[l],
                   s5_c_re[l], s5_c_im[l], s5_d[l], s5_glu_w[l], s5_glu_b[l], w_branch[l], w_out[l])
    return x
```

```python
import functools
import math

import jax
import jax.numpy as jnp
from jax import lax
from jax.experimental import pallas as pl
from jax.experimental.pallas import tpu as pltpu

F32 = jnp.float32
BF16 = jnp.bfloat16

HEAD_DIM = 64
N_HEADS = 4
WIDTH = N_HEADS * HEAD_DIM
NSA_KV_HEADS = 2
NSA_CMP_LEN = 32
NSA_CMP_STRIDE = 16
NSA_SEL_LEN = 64
NSA_SEL_TOPN = 16
NSA_WINDOW = 512
NSA_FORCE_SCORE = 1.0e4
S5_GROUPS = 16
S5_GROUP_CH = 16
S5_STATE = 64
ROPE_THETA = 10000.0
NORM_EPS = 1e-6
QK_SCALE = HEAD_DIM ** -0.5

LANES = 128
SUBLANES = 8
NEG = -1.0e30
SEL_BIAS = -30000.0
SB_DEAD_LOG2 = 160.0
LOG2_E = 1.4426950408889634
VMEM_LIMIT = 56 * 1024 * 1024

TM_PROJ = 512
TR_WPREP = 256
TT_CMP = 128
TQ_SB = 256
TK_SB = 256
TQ_SEL = 512
TK_SEL = 512
NSA_HEAD_ORDER = (0, 2, 1, 3)
TS_S5 = 128


def _dot(a, b):
    return jnp.dot(a, b, preferred_element_type=F32)


def _dot_nt(a, b):
    return lax.dot_general(a, b, (((1,), (1,)), ((), ())), preferred_element_type=F32)


def _silu(x):
    return x * (1.0 / (1.0 + jnp.exp(-x)))


def _sigmoid(x):
    return 1.0 / (1.0 + jnp.exp(-x))


def _iota(shape, dim):
    return lax.broadcasted_iota(jnp.int32, shape, dim)


def _group_mean_sq(x):
    outs = []
    lane = _iota((x.shape[0], LANES), 1)
    low = lane < HEAD_DIM
    for c in range(x.shape[1] // LANES):
        xc = x[:, c * LANES:(c + 1) * LANES]
        sq = xc * xc
        s_lo = jnp.sum(jnp.where(low, sq, 0.0), axis=-1, keepdims=True)
        s_hi = jnp.sum(jnp.where(low, 0.0, sq), axis=-1, keepdims=True)
        outs.append(jnp.where(low, s_lo, s_hi) * (1.0 / HEAD_DIM))
    return outs[0] if len(outs) == 1 else jnp.concatenate(outs, axis=1)


def _rot_half(y):
    w = y.shape[1]
    first = (_iota(y.shape, 1) & (HEAD_DIM // 2)) == 0
    return jnp.where(first, pltpu.roll(y, w - HEAD_DIM // 2, 1), pltpu.roll(y, HEAD_DIM // 2, 1))


def _norm_rope(x, gain, cos, sin_signed):
    reps = x.shape[1] // LANES
    y = x * lax.rsqrt(_group_mean_sq(x) + NORM_EPS) * gain
    c = cos if reps == 1 else jnp.concatenate([cos] * reps, axis=1)
    s = sin_signed if reps == 1 else jnp.concatenate([sin_signed] * reps, axis=1)
    return y * c + _rot_half(y) * s


def _stack_heads(q):
    head = _iota(q.shape, 1) // HEAD_DIM
    return jnp.concatenate([jnp.where(head == h, q, jnp.zeros_like(q)) for h in range(N_HEADS)], axis=0)


def _unstack_heads(o4, m):
    head = _iota((m, WIDTH), 1) // HEAD_DIM
    out = jnp.zeros((m, WIDTH), F32)
    for h in range(N_HEADS):
        out = jnp.where(head == h, o4[h * m:(h + 1) * m], out)
    return out


def _rope_table_kernel(pos_ref, freq_ref, sign_ref, cos_ref, sin_ref):
    ang = pos_ref[...].astype(F32) * freq_ref[...]
    cos_ref[...] = jnp.cos(ang)
    sin_ref[...] = jnp.sin(ang) * sign_ref[...]


def _rope_tables(pos_flat):
    n = pos_flat.shape[0]
    half = HEAD_DIM // 2
    inv_freq = jnp.power(ROPE_THETA, -jnp.arange(half, dtype=F32) / half)
    freq = jnp.tile(inv_freq, LANES // half)[None, :]
    sign = jnp.tile(jnp.concatenate([-jnp.ones((half,), F32), jnp.ones((half,), F32)]), LANES // HEAD_DIM)[None, :]
    tm = 512 if n % 512 == 0 else n
    return pl.pallas_call(
        _rope_table_kernel,
        grid=(n // tm,),
        in_specs=[pl.BlockSpec((tm, 1), lambda i: (i, 0)),
                  pl.BlockSpec((1, LANES), lambda i: (0, 0)),
                  pl.BlockSpec((1, LANES), lambda i: (0, 0))],
        out_specs=[pl.BlockSpec((tm, LANES), lambda i: (i, 0))] * 2,
        out_shape=[jax.ShapeDtypeStruct((n, LANES), F32)] * 2,
        name="rope_tables",
    )(pos_flat[:, None], freq, sign)


def _inproj_kernel(x_ref, ng_ref, wqk_ref, wv_ref, wc_ref, wg_ref, wz_ref, wsc_ref, wsb_ref, ws5_ref,
                   qkg_ref, cos_ref, sin_ref, convw_ref,
                   q_ref, ks_ref, kw_ref, vs_ref, vw_ref, gate_ref, zn_ref, zsb_ref, zs5_ref,
                   sc_ref, sbq_ref, sbk_ref, sbv_ref, s5u_ref, kc_ref, vc_ref, ubuf_ref, *, tm):
    x = x_ref[0]
    ms = jnp.mean(x * x, axis=-1, keepdims=True)
    h = (x * lax.rsqrt(ms + NORM_EPS) * ng_ref[...]).astype(BF16)

    qk = _norm_rope(_dot(h, wqk_ref[...]), qkg_ref[...], cos_ref[0], sin_ref[0])
    q_ref[0] = (qk[:, :WIDTH] * QK_SCALE).astype(BF16)
    ks_ref[0] = qk[:, WIDTH:WIDTH + LANES].astype(BF16)
    kw_ref[0] = qk[:, WIDTH + LANES:].astype(BF16)

    v = _dot(h, wv_ref[...])
    vs_ref[0] = v[:, :LANES].astype(BF16)
    vw_ref[0] = v[:, LANES:].astype(BF16)
    kvc = _dot(h, wc_ref[...])
    kc_ref[0] = kvc[:, :LANES]
    vc_ref[0] = kvc[:, LANES:]
    gate_ref[0] = _sigmoid(_dot(h, wg_ref[...]))

    z = _silu(_dot(h, wz_ref[...]))
    zn_ref[0] = z[:, :WIDTH]
    zsb_ref[0] = z[:, 2 * WIDTH:3 * WIDTH]
    zs5_ref[0] = z[:, 3 * WIDTH:]

    bcx = _dot(h, wsc_ref[...])
    u = bcx[:, WIDTH:2 * WIDTH] * bcx[:, 2 * WIDTH:]

    @pl.when(pl.program_id(1) == 0)
    def _():
        ubuf_ref[0:SUBLANES, :] = jnp.zeros((SUBLANES, WIDTH), F32)

    ubuf_ref[SUBLANES:SUBLANES + tm, :] = u
    u1 = ubuf_ref[SUBLANES - 1:SUBLANES - 1 + tm, :]
    u2 = ubuf_ref[SUBLANES - 2:SUBLANES - 2 + tm, :]
    cw = convw_ref[...]
    y = cw[2:3, :] * u + cw[1:2, :] * u1 + cw[0:1, :] * u2
    ubuf_ref[0:SUBLANES, :] = ubuf_ref[tm:tm + SUBLANES, :]
    sc_ref[0] = (bcx[:, :WIDTH] * y * z[:, WIDTH:2 * WIDTH]).astype(BF16)

    sb = _dot(h, wsb_ref[...])
    sbq_ref[0] = (sb[:, :WIDTH] * QK_SCALE).astype(BF16)
    sbk_ref[0] = sb[:, WIDTH:2 * WIDTH].astype(BF16)
    sbv_ref[0] = sb[:, 2 * WIDTH:].astype(BF16)
    s5u_ref[0] = _dot(h, ws5_ref[...])


def _inproj(x, ng, weights, qkg, cos, sin, convw):
    b, s, d = x.shape
    tm = min(TM_PROJ, s)
    full2 = lambda a: pl.BlockSpec(a.shape, lambda bi, si: (0, 0))
    tok = lambda w: pl.BlockSpec((1, tm, w), lambda bi, si: (bi, si, 0))
    out_widths = [(WIDTH, BF16)] + [(LANES, BF16)] * 4 + [(3 * WIDTH, F32)] + [(WIDTH, F32)] * 3 \
        + [(WIDTH, BF16)] * 4 + [(WIDTH, F32)] + [(LANES, F32)] * 2
    return pl.pallas_call(
        functools.partial(_inproj_kernel, tm=tm),
        grid=(b, s // tm),
        in_specs=[tok(d), full2(ng)] + [full2(w) for w in weights] + [full2(qkg), tok(LANES), tok(LANES), full2(convw)],
        out_specs=[tok(w) for w, _ in out_widths],
        out_shape=[jax.ShapeDtypeStruct((b, s, w), dt) for w, dt in out_widths],
        scratch_shapes=[pltpu.VMEM((tm + 2 * SUBLANES, WIDTH), F32)],
        compiler_params=pltpu.CompilerParams(dimension_semantics=("arbitrary", "arbitrary"),
                                             vmem_limit_bytes=VMEM_LIMIT),
        name="inproj",
    )(x, ng, *weights, qkg, cos, sin, convw)


def _compress_kernel(kc_ref, vc_ref, pe_ref, w1_ref, w2k_ref, w2v_ref, g_ref, cos_ref, sin_ref, kco_ref, vct_ref, *, nk):
    half = NSA_CMP_STRIDE
    for j, src in ((0, kc_ref), (1, vc_ref)):
        tok = [src[0, pl.ds(i, nk, stride=half), :] for i in range(half)]
        lo = jnp.concatenate([(tok[i] + pe_ref[j, 0, i:i + 1, :]).astype(BF16) for i in range(half)], axis=1)
        hi = jnp.concatenate([(tok[i] + pe_ref[j, 1, i:i + 1, :]).astype(BF16) for i in range(half)], axis=1)
        hid = _dot(lo, w1_ref[j, 0]) + pltpu.roll(_dot(hi, w1_ref[j, 1]), nk - 1, 0)
        act = _silu(hid).astype(BF16)
        if j == 0:
            kco_ref[0] = _norm_rope(_dot(act, w2k_ref[...]), g_ref[...], cos_ref[0], sin_ref[0]).astype(BF16)
        else:
            vct_ref[0] = _dot_nt(w2v_ref[...], act).astype(BF16)


def _compress(kc, vc, pe, w1, w2k, w2v, g, cos, sin):
    b, s, _ = kc.shape
    nk = s // NSA_CMP_STRIDE
    full = lambda a: pl.BlockSpec(a.shape, lambda bi: (0,) * a.ndim)
    seq = pl.BlockSpec((1, s, LANES), lambda bi: (bi, 0, 0))
    tab = pl.BlockSpec((1, nk, LANES), lambda bi: (bi, 0, 0))
    return pl.pallas_call(
        functools.partial(_compress_kernel, nk=nk),
        grid=(b,),
        in_specs=[seq, seq, full(pe), full(w1), full(w2k), full(w2v), full(g), tab, tab],
        out_specs=[pl.BlockSpec((1, nk, LANES), lambda bi: (bi, 0, 0)),
                   pl.BlockSpec((1, LANES, nk), lambda bi: (bi, 0, 0))],
        out_shape=[jax.ShapeDtypeStruct((b, nk, LANES), BF16), jax.ShapeDtypeStruct((b, LANES, nk), BF16)],
        compiler_params=pltpu.CompilerParams(dimension_semantics=("arbitrary",), vmem_limit_bytes=VMEM_LIMIT),
        name="compress",
    )(kc, vc, pe, w1, w2k, w2v, g, cos, sin)


def _cmp_topk_kernel(q_ref, kc_ref, vct_ref, ocmp_ref, bias_ref, p_scr, imp_scr, rank_scr,
                     *, tt, nk, n_cmp, n_sel, top_n):
    t0 = pl.program_id(1) * tt
    kc = kc_ref[0]
    half = _iota((tt, LANES), 1) // HEAD_DIM
    nrow = _iota((nk, tt), 0)
    tcol = t0 + _iota((nk, tt), 1)
    valid_t = (nrow * NSA_CMP_STRIDE + (NSA_CMP_LEN - 1) <= tcol) & (nrow < n_cmp)
    brow = _iota((n_sel, tt), 0)
    tsel = t0 + _iota((n_sel, tt), 1)
    cur = tsel // NSA_SEL_LEN
    forced = (brow == 0) | (brow == cur) | (brow == cur - 1)
    in_past = brow * NSA_SEL_LEN <= tsel
    last_block = (t0 + tt - 1) // NSA_SEL_LEN
    sub = _iota((SUBLANES, tt), 0)
    heads_per_kv = N_HEADS // NSA_KV_HEADS
    scores = []
    for h in range(N_HEADS):
        qc = q_ref[0, :, (h % 2) * LANES:(h % 2 + 1) * LANES]
        scores.append(_dot_nt(kc, jnp.where(half == h // heads_per_kv, qc, jnp.zeros_like(qc))))
    probs = []
    for st in scores:
        st = jnp.where(valid_t, st, NEG)
        et = jnp.where(valid_t, jnp.exp(st - jnp.max(st, axis=0, keepdims=True)), 0.0)
        probs.append(et * (1.0 / jnp.maximum(jnp.sum(et, axis=0, keepdims=True), 1e-30)))
    outs = [_dot(vct_ref[0, (h // heads_per_kv) * HEAD_DIM:(h // heads_per_kv + 1) * HEAD_DIM, :],
                 probs[h].astype(BF16)) for h in range(N_HEADS)]
    groups = n_sel // SUBLANES
    imp_g = []
    for kh in range(NSA_KV_HEADS):
        psum = probs[kh * heads_per_kv]
        for g in range(1, heads_per_kv):
            psum = psum + probs[kh * heads_per_kv + g]
        p_scr[kh, SUBLANES:SUBLANES + nk, :] = psum
        p_scr[kh, 0:SUBLANES, :] = jnp.zeros((SUBLANES, tt), F32)
        ratio = NSA_SEL_LEN // NSA_CMP_STRIDE
        imp = p_scr[kh, pl.ds(SUBLANES - 1, n_sel, stride=ratio), :]
        for k in range(1, NSA_CMP_LEN // NSA_CMP_STRIDE + ratio - 1):
            imp = imp + p_scr[kh, pl.ds(SUBLANES - 1 + k, n_sel, stride=ratio), :]
        imp = jnp.where(forced, NSA_FORCE_SCORE, jnp.where(in_past, imp, -NSA_FORCE_SCORE))
        imp_scr[kh] = imp
        imp_g.append([imp[k * SUBLANES:(k + 1) * SUBLANES] for k in range(groups)])

    rank_scr[...] = jnp.zeros((NSA_KV_HEADS, n_sel, tt), F32)
    for m in range(groups):
        @pl.when((m * SUBLANES <= last_block) & (last_block >= top_n))
        def _(m=m):
            for kh in range(NSA_KV_HEADS):
                for k in range(m + 1):
                    mine = imp_g[kh][k]
                    cnt = jnp.zeros((SUBLANES, tt), F32)
                    for c in ([m] if k < m else range(m + 1)):
                        for r in range(SUBLANES):
                            row = imp_scr[kh, c * SUBLANES + r:c * SUBLANES + r + 1, :]
                            if k > c:
                                beats = row >= mine
                            elif k < c:
                                beats = row > mine
                            else:
                                beats = (row > mine) | ((row == mine) & (sub > r))
                            cnt = cnt + jnp.where(beats, 1.0, 0.0)
                    rank_scr[kh, k * SUBLANES:(k + 1) * SUBLANES, :] += cnt

    biases = [jnp.where(rank_scr[kh] < top_n, 0.0, SEL_BIAS) for kh in range(NSA_KV_HEADS)]
    ocmp_ref[0] = jnp.concatenate([outs[h] for h in NSA_HEAD_ORDER], axis=0).T
    pad = jnp.zeros((LANES - n_sel, tt), F32)
    bias_ref[0] = jnp.concatenate([biases[0], pad, biases[1], pad], axis=0).T.astype(BF16)


def _cmp_topk(q, kc, vct, n_cmp):
    b, s, _ = q.shape
    nk = kc.shape[1]
    n_sel = s // NSA_SEL_LEN
    tt = min(TT_CMP, s)
    kern = functools.partial(_cmp_topk_kernel, tt=tt, nk=nk, n_cmp=n_cmp, n_sel=n_sel, top_n=min(NSA_SEL_TOPN, n_sel))
    return pl.pallas_call(
        kern,
        grid=(b, s // tt),
        in_specs=[pl.BlockSpec((1, tt, WIDTH), lambda bi, ti: (bi, ti, 0)),
                  pl.BlockSpec((1, nk, LANES), lambda bi, ti: (bi, 0, 0)),
                  pl.BlockSpec((1, LANES, nk), lambda bi, ti: (bi, 0, 0))],
        out_specs=[pl.BlockSpec((1, tt, WIDTH), lambda bi, ti: (bi, ti, 0))] * 2,
        out_shape=[jax.ShapeDtypeStruct((b, s, WIDTH), F32), jax.ShapeDtypeStruct((b, s, WIDTH), BF16)],
        scratch_shapes=[pltpu.VMEM((NSA_KV_HEADS, nk + SUBLANES, tt), F32), pltpu.VMEM((NSA_KV_HEADS, n_sel, tt), F32),
                        pltpu.VMEM((NSA_KV_HEADS, n_sel, tt), F32)],
        compiler_params=pltpu.CompilerParams(dimension_semantics=("arbitrary", "arbitrary"),
                                             vmem_limit_bytes=VMEM_LIMIT),
        name="cmp_topk",
    )(q, kc, vct)


CAUSAL, WINDOW = 0, 1


def _selwin_kernel(q_ref, bias_ref, ocmp_ref, gate_ref, zn_ref, ks_ref, vs_ref, kw_ref, vw_ref, e_ref, cm_ref, out_ref,
                   acc_scr, m_scr, *, tq, tk):
    t0 = pl.program_id(1) * tq
    jd = t0 // tk
    heads_per_kv = N_HEADS // NSA_KV_HEADS
    half = _iota((tq, LANES), 1) // HEAD_DIM
    q_win, q_sel = [], []
    for h in range(N_HEADS):
        qc = q_ref[0, :, (h % 2) * LANES:(h % 2 + 1) * LANES]
        qh = jnp.where(half == h // heads_per_kv, qc, jnp.zeros_like(qc))
        q_win.append(qh)
        q_sel.append(jnp.concatenate([qh, bias_ref[0, :, (h // heads_per_kv) * LANES:(h // heads_per_kv + 1) * LANES]],
                                     axis=1))
    ones = jnp.ones((tk, LANES), BF16)
    lanes = lambda x, n: jnp.concatenate([x] * (n // LANES), axis=1)

    def tile(i, k_ref, v_ref, selected, masking, first):
        k0 = pl.multiple_of((jd - i) * tk, tk)
        kt = k_ref[0, pl.ds(k0, tk), :]
        if selected:
            kt = jnp.concatenate([kt, e_ref[pl.ds(k0, tk), :]], axis=1)
        scores = [_dot_nt((q_sel if selected else q_win)[h], kt) for h in range(N_HEADS)]
        vt = jnp.concatenate([v_ref[0, pl.ds(k0, tk), :], ones], axis=1)
        ps, alphas = [], []
        for h, s in enumerate(scores):
            blk = slice(h * tq, (h + 1) * tq)
            if masking is not None:
                s = s + cm_ref[masking]
            s_max = jnp.max(s, axis=-1, keepdims=True)
            if first:
                m_new = jnp.broadcast_to(s_max, (tq, LANES))
            else:
                m_old = m_scr[blk, :]
                m_new = jnp.maximum(m_old, s_max)
                alphas.append(lanes(jnp.exp(m_old - m_new), WIDTH))
            ps.append(jnp.exp(s - lanes(m_new, tk)).astype(BF16))
            m_scr[blk, :] = m_new
        pv = _dot(jnp.concatenate(ps, axis=0), vt)
        acc_scr[...] = pv if first else jnp.concatenate(alphas, axis=0) * acc_scr[...] + pv

    def result():
        res = [acc_scr[h * tq:(h + 1) * tq, :LANES] * (1.0 / acc_scr[h * tq:(h + 1) * tq, LANES:])
               for h in range(N_HEADS)]
        return jnp.concatenate([jnp.where(half == 0, res[c], res[c + heads_per_kv]) for c in range(2)], axis=1)

    tile(0, ks_ref, vs_ref, True, CAUSAL, True)

    def sel_body(i, carry):
        tile(i, ks_ref, vs_ref, True, None, False)
        return carry

    lax.fori_loop(1, jd + 1, sel_body, 0)
    o_slc = result()

    tile(0, kw_ref, vw_ref, False, CAUSAL, True)

    @pl.when(jd >= 1)
    def _():
        tile(1, kw_ref, vw_ref, False, WINDOW, False)

    o_win = result()
    gate = gate_ref[0]
    o = gate[:, :WIDTH] * ocmp_ref[0] + gate[:, WIDTH:2 * WIDTH] * o_slc + gate[:, 2 * WIDTH:] * o_win
    out_ref[0] = (o * zn_ref[0]).astype(BF16)


def _selwin(q, bias, ocmp, gate, zn, ks, vs, kw, vw):
    b, s, _ = q.shape
    tq, tk = min(TQ_SEL, s), min(TK_SEL, s)
    assert NSA_WINDOW == tk and tk == tq and s // NSA_SEL_LEN <= HEAD_DIM
    member = (jnp.arange(s)[:, None] // NSA_SEL_LEN == jnp.arange(LANES)[None, :]).astype(BF16)
    row, col = jnp.arange(tq)[:, None], jnp.arange(tk)[None, :]
    cmask = jnp.where(jnp.stack([col <= row, col > row]), 0.0, NEG).astype(F32)
    tok = lambda w: pl.BlockSpec((1, tq, w), lambda bi, qi: (bi, qi, 0))
    seq = pl.BlockSpec((1, s, LANES), lambda bi, qi: (bi, 0, 0))
    return pl.pallas_call(
        functools.partial(_selwin_kernel, tq=tq, tk=tk),
        grid=(b, s // tq),
        in_specs=[tok(WIDTH), tok(WIDTH), tok(WIDTH), tok(3 * WIDTH), tok(WIDTH), seq, seq, seq, seq,
                  pl.BlockSpec(member.shape, lambda bi, qi: (0, 0)),
                  pl.BlockSpec(cmask.shape, lambda bi, qi: (0, 0, 0))],
        out_specs=tok(WIDTH),
        out_shape=jax.ShapeDtypeStruct((b, s, WIDTH), BF16),
        scratch_shapes=[pltpu.VMEM((N_HEADS * tq, WIDTH), F32), pltpu.VMEM((N_HEADS * tq, LANES), F32)],
        compiler_params=pltpu.CompilerParams(dimension_semantics=("arbitrary", "arbitrary"),
                                             vmem_limit_bytes=VMEM_LIMIT),
        name="selwin",
    )(q, bias, ocmp, gate, zn, ks, vs, kw, vw, member, cmask)


def _stickbrk_kernel(q_ref, k_ref, v_ref, z_ref, mk_ref, out_ref, acc_scr, carry_scr, *, tq, tk):
    t0 = pl.program_id(1) * tq
    jd = t0 // tk
    q = q_ref[0]
    head = _iota(q.shape, 1) // HEAD_DIM
    qh = [jnp.where(head == h, q, jnp.zeros_like(q)) for h in range(N_HEADS)]
    tri = jnp.where(_iota((tk, tk), 0) >= _iota((tk, tk), 1), 1.0, 0.0).astype(BF16)
    tri2 = jnp.concatenate([tri, tri], axis=0)
    lanes = lambda x, n: jnp.concatenate([x] * (n // LANES), axis=1)

    def tile(i, diagonal):
        k0 = pl.multiple_of((jd - i) * tk, tk)
        kt = k_ref[0, pl.ds(k0, tk), :]
        zs = [_dot_nt(qh[h], kt) * LOG2_E for h in range(N_HEADS)]
        sps, sufs = [], []
        for z2 in zs:
            sp2 = jnp.maximum(z2, 0.0) + jnp.log2(1.0 + jnp.exp2(-jnp.abs(z2)))
            if diagonal:
                sp2 = sp2 * mk_ref[0]
            hi = sp2.astype(BF16)
            lo = (sp2 - hi.astype(F32)).astype(BF16)
            sps.append(sp2)
            sufs.append(_dot(jnp.concatenate([hi, lo], axis=1), tri2))
        vt = v_ref[0, pl.ds(k0, tk), :]
        dead = None
        for h in range(N_HEADS):
            blk = slice(h * tq, (h + 1) * tq)
            tile_sum = jnp.broadcast_to(jnp.sum(sps[h], axis=-1, keepdims=True), (tq, LANES))
            if diagonal:
                w = jnp.exp2(zs[h] - sufs[h] + mk_ref[1])
                acc_scr[blk, :] = _dot(w.astype(BF16), vt)
                carry = tile_sum
            else:
                carry = carry_scr[blk, :]
                w = jnp.exp2(zs[h] - sufs[h] - lanes(carry, tk))
                acc_scr[blk, :] += _dot(w.astype(BF16), vt)
                carry = carry + tile_sum
            carry_scr[blk, :] = carry
            head_dead = jnp.min(carry) > SB_DEAD_LOG2
            dead = head_dead if dead is None else dead & head_dead
        return dead.astype(jnp.int32)

    def cond(c):
        i, dead = c
        return (i <= jd) & (dead == 0)

    def body(c):
        return c[0] + 1, tile(c[0], False)

    lax.while_loop(cond, body, (jnp.int32(1), tile(0, True)))
    out_ref[0] = (_unstack_heads(acc_scr[...], tq) * z_ref[0]).astype(BF16)


def _stickbrk(q, k, v, z):
    b, s, _ = q.shape
    tq, tk = min(TQ_SB, s), min(TK_SB, s)
    assert tq == tk
    keep = jnp.arange(tk)[None, :] < jnp.arange(tq)[:, None]
    masks = jnp.stack([jnp.where(keep, 1.0, 0.0), jnp.where(keep, 0.0, NEG)]).astype(F32)
    tok = pl.BlockSpec((1, tq, WIDTH), lambda bi, qi: (bi, qi, 0))
    seq = pl.BlockSpec((1, s, WIDTH), lambda bi, qi: (bi, 0, 0))
    return pl.pallas_call(
        functools.partial(_stickbrk_kernel, tq=tq, tk=tk),
        grid=(b, s // tq),
        in_specs=[tok, seq, seq, tok, pl.BlockSpec(masks.shape, lambda bi, qi: (0, 0, 0))],
        out_specs=tok,
        out_shape=jax.ShapeDtypeStruct((b, s, WIDTH), BF16),
        scratch_shapes=[pltpu.VMEM((N_HEADS * tq, WIDTH), F32), pltpu.VMEM((N_HEADS * tq, LANES), F32)],
        compiler_params=pltpu.CompilerParams(dimension_semantics=("arbitrary", "arbitrary"),
                                             vmem_limit_bytes=VMEM_LIMIT),
        name="stickbrk",
    )(q, k, v, z, masks)


def _s5_disc_kernel(are_ref, aim_ref, ldt_ref, bre_ref, bim_ref, abre_ref, abim_ref, bbre_ref, bbim_ref):
    dt = jnp.exp(ldt_ref[...])
    lr, li = are_ref[...], aim_ref[...]
    mag = jnp.exp(lr * dt)
    ab_re, ab_im = mag * jnp.cos(li * dt), mag * jnp.sin(li * dt)
    den = lr * lr + li * li
    coef_re = ((ab_re - 1.0) * lr + ab_im * li) / den
    coef_im = (ab_im * lr - (ab_re - 1.0) * li) / den
    abre_ref[...] = ab_re
    abim_ref[...] = ab_im
    br, bi = bre_ref[...], bim_ref[...]
    bbre_ref[...] = coef_re[:, None, :] * br - coef_im[:, None, :] * bi
    bbim_ref[...] = coef_re[:, None, :] * bi + coef_im[:, None, :] * br


def _s5_discretise(a_re, a_im, log_dt, b_re, b_im):
    g, p = a_re.shape
    brt, bit = jnp.swapaxes(b_re, 1, 2), jnp.swapaxes(b_im, 1, 2)
    return pl.pallas_call(
        _s5_disc_kernel,
        out_shape=[jax.ShapeDtypeStruct((g, p), F32)] * 2 + [jax.ShapeDtypeStruct(brt.shape, F32)] * 2,
        name="s5_disc",
    )(a_re, a_im, log_dt[:, None], brt, bit)


def _s5_kernel(u_ref, z_ref, bmat_ref, a_ref, cmat_ref, d_ref, gw_ref, gb_ref, out_ref, xs_scr, state_scr,
               *, nb, ts, ns):
    @pl.when(pl.program_id(0) == 0)
    def _():
        state_scr[...] = jnp.zeros((nb, 2 * ns), F32)

    nc = 2 * ns // LANES
    for b in range(nb):
        bu = _dot(u_ref[b].astype(BF16), bmat_ref[...])
        for c in range(nc):
            xs_scr[c, pl.ds(b, ts, stride=nb), :] = bu[:, c * LANES:(c + 1) * LANES]

    a_re = jnp.broadcast_to(a_ref[0:1, :], (nb, ns))
    a_im = jnp.broadcast_to(a_ref[1:2, :], (nb, ns))

    def step(t, state):
        x_re, x_im = state
        r0 = pl.multiple_of(t * nb, nb)
        bu = jnp.concatenate([xs_scr[c, pl.ds(r0, nb), :] for c in range(nc)], axis=1)
        n_re = a_re * x_re - a_im * x_im + bu[:, :ns]
        n_im = a_re * x_im + a_im * x_re + bu[:, ns:]
        for c in range(nc // 2):
            xs_scr[c, pl.ds(r0, nb), :] = n_re[:, c * LANES:(c + 1) * LANES]
            xs_scr[nc // 2 + c, pl.ds(r0, nb), :] = n_im[:, c * LANES:(c + 1) * LANES]
        return n_re, n_im

    st = state_scr[...]
    x_re, x_im = lax.fori_loop(0, ts, step, (st[:, :ns], st[:, ns:]), unroll=4)
    state_scr[...] = jnp.concatenate([x_re, x_im], axis=1)

    width = u_ref.shape[-1]
    ys = []
    for b in range(nb):
        xs = jnp.concatenate([xs_scr[c, pl.ds(b, ts, stride=nb), :] for c in range(nc)], axis=1)
        ys.append((_dot(xs.astype(BF16), cmat_ref[...]) + d_ref[...] * u_ref[b]).astype(BF16))
    for b in range(nb):
        glu = _dot(ys[b], gw_ref[...]) + gb_ref[...]
        out_ref[b] = (glu[:, :width] * _sigmoid(glu[:, width:]) * z_ref[b]).astype(BF16)


def _s5(u, z, bmat, a_rows, cmat, d_row, glu_w, glu_b):
    nb, s, w = u.shape
    ts = min(TS_S5, s)
    ns = a_rows.shape[1]
    full = lambda a: pl.BlockSpec(a.shape, lambda i: (0, 0))
    tok = pl.BlockSpec((nb, ts, w), lambda i: (0, i, 0))
    return pl.pallas_call(
        functools.partial(_s5_kernel, nb=nb, ts=ts, ns=ns),
        grid=(s // ts,),
        in_specs=[tok, tok, full(bmat), full(a_rows), full(cmat), full(d_row), full(glu_w), full(glu_b)],
        out_specs=tok,
        out_shape=jax.ShapeDtypeStruct((nb, s, w), BF16),
        scratch_shapes=[pltpu.VMEM((2 * ns // LANES, ts * nb, LANES), F32), pltpu.VMEM((nb, 2 * ns), F32)],
        compiler_params=pltpu.CompilerParams(dimension_semantics=("arbitrary",), vmem_limit_bytes=VMEM_LIMIT),
        name="s5",
    )(u, z, bmat, a_rows, cmat, d_row, glu_w, glu_b)


def _merge_kernel(x_ref, ng_ref, o0_ref, o1_ref, o2_ref, o3_ref, wm_ref, wb_ref, wo_ref, out_ref):
    x = x_ref[...]
    ms = jnp.mean(x * x, axis=-1, keepdims=True)
    h = (x * lax.rsqrt(ms + NORM_EPS) * ng_ref[...]).astype(BF16)
    d = x.shape[1]
    mixed = None
    for m, o_ref in enumerate((o0_ref, o1_ref, o2_ref, o3_ref)):
        gate = _sigmoid(_dot(h, wm_ref[:, m * d:(m + 1) * d]))
        term = gate * _dot(o_ref[...], wb_ref[m])
        mixed = term if mixed is None else mixed + term
    out_ref[...] = x + _dot(mixed.astype(BF16), wo_ref[...])


def _merge(x2, ng, outs, wm, wb, wo):
    t, d = x2.shape
    tm = min(TM_PROJ, t)
    tok = lambda w: pl.BlockSpec((tm, w), lambda i: (i, 0))
    full = lambda a: pl.BlockSpec(a.shape, lambda i: (0,) * a.ndim)
    return pl.pallas_call(
        _merge_kernel,
        grid=(t // tm,),
        in_specs=[tok(d), full(ng)] + [tok(WIDTH)] * 4 + [full(wm), full(wb), full(wo)],
        out_specs=tok(d),
        out_shape=jax.ShapeDtypeStruct((t, d), F32),
        compiler_params=pltpu.CompilerParams(dimension_semantics=("arbitrary",), vmem_limit_bytes=VMEM_LIMIT),
        name="merge",
    )(x2, ng, *outs, wm, wb, wo)


def _wprep_kernel(w_ref, wqk_ref, wv_ref, wc_ref, wg_ref, wz_ref, wsc_ref, wsb_ref, ws5_ref, wm_ref, *, d):
    w = WIDTH
    o_q, o_kv, o_gate = 0, w, 4 * w
    o_nz = o_gate + 3 * N_HEADS
    o_sc, o_scz, o_sb, o_sbz, o_s5, o_s5z = o_nz + w, o_nz + 4 * w, o_nz + 5 * w, o_nz + 8 * w, o_nz + 9 * w, o_nz + 10 * w
    o_merge = o_nz + 11 * w
    col = lambda a, n: w_ref[0, :, a:a + n]
    kv = lambda i: o_kv + i * (w // 2)
    cat = lambda parts: jnp.concatenate(parts, axis=1).astype(BF16)

    def heads(a):
        return [col(a + h * HEAD_DIM, HEAD_DIM) for h in NSA_HEAD_ORDER]

    wqk_ref[...] = cat(heads(o_q) + [col(kv(2), w // 2), col(kv(4), w // 2)])
    wv_ref[...] = cat([col(kv(3), w // 2), col(kv(5), w // 2)])
    wc_ref[...] = col(kv(0), w).astype(BF16)
    gates = col(o_gate, 3 * N_HEADS)
    wg_ref[...] = cat([jnp.broadcast_to(gates[:, c:c + 1], (gates.shape[0], HEAD_DIM))
                       for c in (br * N_HEADS + h for br in range(3) for h in NSA_HEAD_ORDER)])
    wz_ref[...] = cat(heads(o_nz) + [col(o_scz, w), col(o_sbz, w), col(o_s5z, w)])
    wsc_ref[...] = col(o_sc, 3 * w).astype(BF16)
    wsb_ref[...] = col(o_sb, 3 * w).astype(BF16)
    ws5_ref[...] = col(o_s5, w).astype(BF16)
    wm_ref[...] = col(o_merge, N_HEADS * d).astype(BF16)


def _wprep(w_in, layer):
    _, d, n_in = w_in.shape
    tr = min(TR_WPREP, d)
    widths = [2 * WIDTH, WIDTH, WIDTH, 3 * WIDTH, 4 * WIDTH, 3 * WIDTH, 3 * WIDTH, WIDTH, N_HEADS * d]
    return pl.pallas_call(
        functools.partial(_wprep_kernel, d=d),
        grid=(d // tr,),
        in_specs=[pl.BlockSpec((1, tr, n_in), lambda i: (layer, i, 0))],
        out_specs=[pl.BlockSpec((tr, n), lambda i: (i, 0)) for n in widths],
        out_shape=[jax.ShapeDtypeStruct((d, n), BF16) for n in widths],
        compiler_params=pltpu.CompilerParams(dimension_semantics=("arbitrary",), vmem_limit_bytes=VMEM_LIMIT),
        name="wprep",
    )(w_in)


def _block_diag(blocks):
    g, r, c = blocks.shape
    eye = jnp.eye(g, dtype=blocks.dtype)
    return (eye[:, None, :, None] * blocks[:, :, None, :]).reshape(g * r, g * c)


def _block_diag_pair(w1):
    eye = jnp.eye(NSA_KV_HEADS, dtype=w1.dtype)
    bd = w1[..., :, None, :, None, :] * eye[:, None, :, None]
    return bd.reshape(w1.shape[:-3] + (w1.shape[-3] * NSA_KV_HEADS * HEAD_DIM, NSA_KV_HEADS * HEAD_DIM))


def _layer(x, cos, sin, cos_c, sin_c, n_cmp, norm_g, w_in_all, layer, qk_g, cmp_pe, cmp_w1, cmp_w2, conv_w,
           a_re, a_im, log_dt, b_re, b_im, c_re, c_im, d_skip, glu_w, glu_b, w_branch, w_out):
    b, s, d = x.shape
    w = WIDTH
    *weights, w_merge = _wprep(w_in_all, layer)
    qkg = jnp.concatenate([jnp.tile(qk_g[0], N_HEADS), jnp.tile(qk_g[2], NSA_KV_HEADS),
                           jnp.tile(qk_g[3], NSA_KV_HEADS)])[None, :]
    ng = norm_g[None, :]

    (q, ks, kw, vs, vw, gate, zn, zsb, zs5, sc_o, sbq, sbk, sbv, s5u, kc_in, vc_in) = _inproj(
        x, ng, weights, qkg, cos, sin, conv_w)

    half = NSA_CMP_STRIDE
    two = NSA_CMP_LEN // half
    pe = jnp.tile(cmp_pe.reshape(2, two, half, HEAD_DIM), (1, 1, 1, NSA_KV_HEADS))
    w1 = _block_diag_pair(cmp_w1.reshape(2, two, half, HEAD_DIM, HEAD_DIM))
    w2k, w2v = _block_diag(jnp.stack([cmp_w2[0]] * NSA_KV_HEADS)), _block_diag(jnp.stack([cmp_w2[1]] * NSA_KV_HEADS))
    kc, vct = _compress(kc_in, vc_in, pe, w1.astype(BF16), w2k.astype(BF16), w2v.T.astype(BF16),
                        jnp.tile(qk_g[1], NSA_KV_HEADS)[None, :], cos_c, sin_c)

    ocmp, bias = _cmp_topk(q, kc, vct, n_cmp)
    nsa_o = _selwin(q, bias, ocmp, gate, zn, ks, vs, kw, vw)
    sb_o = _stickbrk(sbq, sbk, sbv, zsb)

    ab_re, ab_im, bb_re, bb_im = _s5_discretise(a_re, a_im, log_dt, b_re, b_im)
    bmat = jnp.concatenate([_block_diag(bb_re), _block_diag(bb_im)], axis=1).astype(BF16)
    cmat = jnp.concatenate([_block_diag(jnp.swapaxes(c_re, 1, 2)),
                            -_block_diag(jnp.swapaxes(c_im, 1, 2))], axis=0).astype(BF16)
    a_rows = jnp.stack([ab_re.reshape(-1), ab_im.reshape(-1)])
    s5_o = _s5(s5u, zs5, bmat, a_rows, cmat, d_skip.reshape(1, -1), glu_w.astype(BF16), glu_b[None, :])

    wb_nsa = w_branch[0].reshape(N_HEADS, HEAD_DIM, d)[jnp.array(NSA_HEAD_ORDER)].reshape(w, d)
    wb = jnp.concatenate([wb_nsa[None], w_branch[1:]], axis=0).astype(BF16)
    out = _merge(x.reshape(b * s, d), ng,
                 [o.reshape(b * s, w) for o in (nsa_o, sc_o, sb_o, s5_o)],
                 w_merge, wb, w_out.astype(BF16))
    return out.reshape(b, s, d)


def kernel(x, positions, norm_g, w_in, nsa_qk_g, nsa_cmp_pe, nsa_cmp_w1, nsa_cmp_w2, sc_conv_w, s5_a_re, s5_a_im,
           s5_log_dt, s5_b_re, s5_b_im, s5_c_re, s5_c_im, s5_d, s5_glu_w, s5_glu_b, w_branch, w_out):
    b, s, _ = x.shape
    assert s % TK_SEL == 0 or s < TK_SB
    cos, sin = _rope_tables(positions.reshape(-1))
    cos, sin = cos.reshape(b, s, LANES), sin.reshape(b, s, LANES)
    nk = s // NSA_CMP_STRIDE
    n_cmp = (s - NSA_CMP_LEN) // NSA_CMP_STRIDE + 1
    pos_c = jnp.concatenate([positions[:, NSA_CMP_LEN - 1::NSA_CMP_STRIDE],
                             jnp.zeros((b, nk - n_cmp), positions.dtype)], axis=1)
    cos_c, sin_c = _rope_tables(pos_c.reshape(-1))
    cos_c, sin_c = cos_c.reshape(b, nk, LANES), sin_c.reshape(b, nk, LANES)
    for l in range(norm_g.shape[0]):
        x = _layer(x, cos, sin, cos_c, sin_c, n_cmp, norm_g[l], w_in, l, nsa_qk_g[l], nsa_cmp_pe[l], nsa_cmp_w1[l],
                   nsa_cmp_w2[l], sc_conv_w[l], s5_a_re[l], s5_a_im[l], s5_log_dt[l], s5_b_re[l], s5_b_im[l],
                   s5_c_re[l], s5_c_im[l], s5_d[l], s5_glu_w[l], s5_glu_b[l], w_branch[l], w_out[l])
    return x
```

```python
import functools

import jax
import jax.numpy as jnp
from jax import lax
from jax.experimental import pallas as pl
from jax.experimental.pallas import tpu as pltpu

F32 = jnp.float32
BF16 = jnp.bfloat16

HEAD_DIM = 64
N_HEADS = 4
WIDTH = N_HEADS * HEAD_DIM
NSA_KV_HEADS = 2
NSA_CMP_LEN = 32
NSA_CMP_STRIDE = 16
NSA_SEL_LEN = 64
NSA_SEL_TOPN = 16
NSA_WINDOW = 512
NSA_FORCE_SCORE = 1.0e4
S5_GROUPS = 16
S5_GROUP_CH = 16
S5_STATE = 64
ROPE_THETA = 10000.0
NORM_EPS = 1e-6
QK_SCALE = HEAD_DIM ** -0.5

LANES = 128
SUBLANES = 8
NEG = -1.0e30
SEL_BIAS = -30000.0
SB_DEAD_LOG2 = 160.0
LOG2_E = 1.4426950408889634
VMEM_LIMIT = 56 * 1024 * 1024

TM_PROJ = 512
TR_WPREP = 256
TT_CMP = 128
TQ_SB = 256
TK_SB = 256
TQ_SEL = 512
TK_SEL = 512
NSA_HEAD_ORDER = (0, 2, 1, 3)
TS_S5 = 128


def _dot(a, b):
    return jnp.dot(a, b, preferred_element_type=F32)


def _dot_nt(a, b):
    return lax.dot_general(a, b, (((1,), (1,)), ((), ())), preferred_element_type=F32)


def _silu(x):
    return x * (1.0 / (1.0 + jnp.exp(-x)))


def _sigmoid(x):
    return 1.0 / (1.0 + jnp.exp(-x))


def _iota(shape, dim):
    return lax.broadcasted_iota(jnp.int32, shape, dim)


def _group_mean_sq(x):
    outs = []
    lane = _iota((x.shape[0], LANES), 1)
    low = lane < HEAD_DIM
    for c in range(x.shape[1] // LANES):
        xc = x[:, c * LANES:(c + 1) * LANES]
        sq = xc * xc
        s_lo = jnp.sum(jnp.where(low, sq, 0.0), axis=-1, keepdims=True)
        s_hi = jnp.sum(jnp.where(low, 0.0, sq), axis=-1, keepdims=True)
        outs.append(jnp.where(low, s_lo, s_hi) * (1.0 / HEAD_DIM))
    return outs[0] if len(outs) == 1 else jnp.concatenate(outs, axis=1)


def _rot_half(y):
    w = y.shape[1]
    first = (_iota(y.shape, 1) & (HEAD_DIM // 2)) == 0
    return jnp.where(first, pltpu.roll(y, w - HEAD_DIM // 2, 1), pltpu.roll(y, HEAD_DIM // 2, 1))


def _norm_rope(x, gain, cos, sin_signed):
    reps = x.shape[1] // LANES
    y = x * lax.rsqrt(_group_mean_sq(x) + NORM_EPS) * gain
    c = cos if reps == 1 else jnp.concatenate([cos] * reps, axis=1)
    s = sin_signed if reps == 1 else jnp.concatenate([sin_signed] * reps, axis=1)
    return y * c + _rot_half(y) * s


def _stack_heads(q):
    head = _iota(q.shape, 1) // HEAD_DIM
    return jnp.concatenate([jnp.where(head == h, q, jnp.zeros_like(q)) for h in range(N_HEADS)], axis=0)


def _unstack_heads(o4, m):
    head = _iota((m, WIDTH), 1) // HEAD_DIM
    out = jnp.zeros((m, WIDTH), F32)
    for h in range(N_HEADS):
        out = jnp.where(head == h, o4[h * m:(h + 1) * m], out)
    return out


def _rope_table_kernel(pos_ref, freq_ref, sign_ref, cos_ref, sin_ref):
    ang = pos_ref[...].astype(F32) * freq_ref[...]
    cos_ref[...] = jnp.cos(ang)
    sin_ref[...] = jnp.sin(ang) * sign_ref[...]


def _rope_tables(pos_flat):
    n = pos_flat.shape[0]
    half = HEAD_DIM // 2
    inv_freq = jnp.power(ROPE_THETA, -jnp.arange(half, dtype=F32) / half)
    freq = jnp.tile(inv_freq, LANES // half)[None, :]
    sign = jnp.tile(jnp.concatenate([-jnp.ones((half,), F32), jnp.ones((half,), F32)]), LANES // HEAD_DIM)[None, :]
    tm = 512 if n % 512 == 0 else n
    return pl.pallas_call(
        _rope_table_kernel,
        grid=(n // tm,),
        in_specs=[pl.BlockSpec((tm, 1), lambda i: (i, 0)),
                  pl.BlockSpec((1, LANES), lambda i: (0, 0)),
                  pl.BlockSpec((1, LANES), lambda i: (0, 0))],
        out_specs=[pl.BlockSpec((tm, LANES), lambda i: (i, 0))] * 2,
        out_shape=[jax.ShapeDtypeStruct((n, LANES), F32)] * 2,
        name="rope_tables",
    )(pos_flat[:, None], freq, sign)


def _inproj_kernel(x_ref, ng_ref, wqk_ref, wv_ref, wc_ref, wg_ref, wz_ref, wsc_ref, wsb_ref, ws5_ref,
                   qkg_ref, cos_ref, sin_ref, convw_ref,
                   q_ref, ks_ref, kw_ref, vs_ref, vw_ref, gate_ref, zn_ref, zsb_ref, zs5_ref,
                   sc_ref, sbq_ref, sbk_ref, sbv_ref, s5u_ref, kc_ref, vc_ref, ubuf_ref, *, tm):
    x = x_ref[0]
    ms = jnp.mean(x * x, axis=-1, keepdims=True)
    h = (x * lax.rsqrt(ms + NORM_EPS) * ng_ref[...]).astype(BF16)

    qk = _norm_rope(_dot(h, wqk_ref[...]), qkg_ref[...], cos_ref[0], sin_ref[0])
    q_ref[0] = (qk[:, :WIDTH] * QK_SCALE).astype(BF16)
    ks_ref[0] = qk[:, WIDTH:WIDTH + LANES].astype(BF16)
    kw_ref[0] = qk[:, WIDTH + LANES:].astype(BF16)

    v = _dot(h, wv_ref[...])
    vs_ref[0] = v[:, :LANES].astype(BF16)
    vw_ref[0] = v[:, LANES:].astype(BF16)
    kvc = _dot(h, wc_ref[...])
    kc_ref[0] = kvc[:, :LANES]
    vc_ref[0] = kvc[:, LANES:]
    gate_ref[0] = _sigmoid(_dot(h, wg_ref[...]))

    z = _silu(_dot(h, wz_ref[...]))
    zn_ref[0] = z[:, :WIDTH]
    zsb_ref[0] = z[:, 2 * WIDTH:3 * WIDTH]
    zs5_ref[0] = z[:, 3 * WIDTH:]

    bcx = _dot(h, wsc_ref[...])
    u = bcx[:, WIDTH:2 * WIDTH] * bcx[:, 2 * WIDTH:]

    @pl.when(pl.program_id(1) == 0)
    def _():
        ubuf_ref[0:SUBLANES, :] = jnp.zeros((SUBLANES, WIDTH), F32)

    ubuf_ref[SUBLANES:SUBLANES + tm, :] = u
    u1 = ubuf_ref[SUBLANES - 1:SUBLANES - 1 + tm, :]
    u2 = ubuf_ref[SUBLANES - 2:SUBLANES - 2 + tm, :]
    cw = convw_ref[...]
    y = cw[2:3, :] * u + cw[1:2, :] * u1 + cw[0:1, :] * u2
    ubuf_ref[0:SUBLANES, :] = ubuf_ref[tm:tm + SUBLANES, :]
    sc_ref[0] = (bcx[:, :WIDTH] * y * z[:, WIDTH:2 * WIDTH]).astype(BF16)

    sb = _dot(h, wsb_ref[...])
    sbq_ref[0] = (sb[:, :WIDTH] * QK_SCALE).astype(BF16)
    sbk_ref[0] = sb[:, WIDTH:2 * WIDTH].astype(BF16)
    sbv_ref[0] = sb[:, 2 * WIDTH:].astype(BF16)
    s5u_ref[0] = _dot(h, ws5_ref[...])


def _inproj(x, ng, weights, qkg, cos, sin, convw):
    b, s, d = x.shape
    tm = min(TM_PROJ, s)
    full2 = lambda a: pl.BlockSpec(a.shape, lambda bi, si: (0, 0))
    tok = lambda w: pl.BlockSpec((1, tm, w), lambda bi, si: (bi, si, 0))
    out_widths = [(WIDTH, BF16)] + [(LANES, BF16)] * 4 + [(3 * WIDTH, F32)] + [(WIDTH, F32)] * 3 \
        + [(WIDTH, BF16)] * 4 + [(WIDTH, F32)] + [(LANES, F32)] * 2
    return pl.pallas_call(
        functools.partial(_inproj_kernel, tm=tm),
        grid=(b, s // tm),
        in_specs=[tok(d), full2(ng)] + [full2(w) for w in weights] + [full2(qkg), tok(LANES), tok(LANES), full2(convw)],
        out_specs=[tok(w) for w, _ in out_widths],
        out_shape=[jax.ShapeDtypeStruct((b, s, w), dt) for w, dt in out_widths],
        scratch_shapes=[pltpu.VMEM((tm + 2 * SUBLANES, WIDTH), F32)],
        compiler_params=pltpu.CompilerParams(dimension_semantics=("arbitrary", "arbitrary"),
                                             vmem_limit_bytes=VMEM_LIMIT),
        name="inproj",
    )(x, ng, *weights, qkg, cos, sin, convw)


def _compress_kernel(kc_ref, vc_ref, pe_ref, w1_ref, w2k_ref, w2v_ref, g_ref, cos_ref, sin_ref, kco_ref, vct_ref, *, nk):
    half = NSA_CMP_STRIDE
    for j, src in ((0, kc_ref), (1, vc_ref)):
        tok = [src[0, pl.ds(i, nk, stride=half), :] for i in range(half)]
        lo = jnp.concatenate([(tok[i] + pe_ref[j, 0, i:i + 1, :]).astype(BF16) for i in range(half)], axis=1)
        hi = jnp.concatenate([(tok[i] + pe_ref[j, 1, i:i + 1, :]).astype(BF16) for i in range(half)], axis=1)
        hid = _dot(lo, w1_ref[j, 0]) + pltpu.roll(_dot(hi, w1_ref[j, 1]), nk - 1, 0)
        act = _silu(hid).astype(BF16)
        if j == 0:
            kco_ref[0] = _norm_rope(_dot(act, w2k_ref[...]), g_ref[...], cos_ref[0], sin_ref[0]).astype(BF16)
        else:
            vct_ref[0] = _dot_nt(w2v_ref[...], act).astype(BF16)


def _compress(kc, vc, pe, w1, w2k, w2v, g, cos, sin):
    b, s, _ = kc.shape
    nk = s // NSA_CMP_STRIDE
    full = lambda a: pl.BlockSpec(a.shape, lambda bi: (0,) * a.ndim)
    seq = pl.BlockSpec((1, s, LANES), lambda bi: (bi, 0, 0))
    tab = pl.BlockSpec((1, nk, LANES), lambda bi: (bi, 0, 0))
    return pl.pallas_call(
        functools.partial(_compress_kernel, nk=nk),
        grid=(b,),
        in_specs=[seq, seq, full(pe), full(w1), full(w2k), full(w2v), full(g), tab, tab],
        out_specs=[pl.BlockSpec((1, nk, LANES), lambda bi: (bi, 0, 0)),
                   pl.BlockSpec((1, LANES, nk), lambda bi: (bi, 0, 0))],
        out_shape=[jax.ShapeDtypeStruct((b, nk, LANES), BF16), jax.ShapeDtypeStruct((b, LANES, nk), BF16)],
        compiler_params=pltpu.CompilerParams(dimension_semantics=("arbitrary",), vmem_limit_bytes=VMEM_LIMIT),
        name="compress",
    )(kc, vc, pe, w1, w2k, w2v, g, cos, sin)


def _cmp_topk_kernel(q_ref, kc_ref, vct_ref, ocmp_ref, bias_ref, p_scr, imp_scr, rank_scr,
                     *, tt, nk, n_cmp, n_sel, top_n):
    t0 = pl.program_id(1) * tt
    kc = kc_ref[0]
    half = _iota((tt, LANES), 1) // HEAD_DIM
    nrow = _iota((nk, tt), 0)
    tcol = t0 + _iota((nk, tt), 1)
    valid_t = (nrow * NSA_CMP_STRIDE + (NSA_CMP_LEN - 1) <= tcol) & (nrow < n_cmp)
    brow = _iota((n_sel, tt), 0)
    tsel = t0 + _iota((n_sel, tt), 1)
    cur = tsel // NSA_SEL_LEN
    forced = (brow == 0) | (brow == cur) | (brow == cur - 1)
    in_past = brow * NSA_SEL_LEN <= tsel
    last_block = (t0 + tt - 1) // NSA_SEL_LEN
    sub = _iota((SUBLANES, tt), 0)
    heads_per_kv = N_HEADS // NSA_KV_HEADS
    scores = []
    for h in range(N_HEADS):
        qc = q_ref[0, :, (h % 2) * LANES:(h % 2 + 1) * LANES]
        scores.append(_dot_nt(kc, jnp.where(half == h // heads_per_kv, qc, jnp.zeros_like(qc))))
    probs = []
    for st in scores:
        st = jnp.where(valid_t, st, NEG)
        et = jnp.where(valid_t, jnp.exp(st - jnp.max(st, axis=0, keepdims=True)), 0.0)
        probs.append(et * (1.0 / jnp.maximum(jnp.sum(et, axis=0, keepdims=True), 1e-30)))
    outs = [_dot(vct_ref[0, (h // heads_per_kv) * HEAD_DIM:(h // heads_per_kv + 1) * HEAD_DIM, :],
                 probs[h].astype(BF16)) for h in range(N_HEADS)]
    groups = n_sel // SUBLANES
    imp_g = []
    for kh in range(NSA_KV_HEADS):
        psum = probs[kh * heads_per_kv]
        for g in range(1, heads_per_kv):
            psum = psum + probs[kh * heads_per_kv + g]
        p_scr[kh, SUBLANES:SUBLANES + nk, :] = psum
        p_scr[kh, 0:SUBLANES, :] = jnp.zeros((SUBLANES, tt), F32)
        ratio = NSA_SEL_LEN // NSA_CMP_STRIDE
        imp = p_scr[kh, pl.ds(SUBLANES - 1, n_sel, stride=ratio), :]
        for k in range(1, NSA_CMP_LEN // NSA_CMP_STRIDE + ratio - 1):
            imp = imp + p_scr[kh, pl.ds(SUBLANES - 1 + k, n_sel, stride=ratio), :]
        imp = jnp.where(forced, NSA_FORCE_SCORE, jnp.where(in_past, imp, -NSA_FORCE_SCORE))
        imp_scr[kh] = imp
        imp_g.append([imp[k * SUBLANES:(k + 1) * SUBLANES] for k in range(groups)])

    rank_scr[...] = jnp.zeros((NSA_KV_HEADS, n_sel, tt), F32)
    for m in range(groups):
        @pl.when((m * SUBLANES <= last_block) & (last_block >= top_n))
        def _(m=m):
            for kh in range(NSA_KV_HEADS):
                for k in range(m + 1):
                    mine = imp_g[kh][k]
                    cnt = jnp.zeros((SUBLANES, tt), F32)
                    for c in ([m] if k < m else range(m + 1)):
                        for r in range(SUBLANES):
                            row = imp_scr[kh, c * SUBLANES + r:c * SUBLANES + r + 1, :]
                            if k > c:
                                beats = row >= mine
                            elif k < c:
                                beats = row > mine
                            else:
                                beats = (row > mine) | ((row == mine) & (sub > r))
                            cnt = cnt + jnp.where(beats, 1.0, 0.0)
                    rank_scr[kh, k * SUBLANES:(k + 1) * SUBLANES, :] += cnt

    biases = [jnp.where(rank_scr[kh] < top_n, 0.0, SEL_BIAS) for kh in range(NSA_KV_HEADS)]
    ocmp_ref[0] = jnp.concatenate([outs[h] for h in NSA_HEAD_ORDER], axis=0).T
    pad = jnp.zeros((LANES - n_sel, tt), F32)
    bias_ref[0] = jnp.concatenate([biases[0], pad, biases[1], pad], axis=0).T.astype(BF16)


def _cmp_topk(q, kc, vct, n_cmp):
    b, s, _ = q.shape
    nk = kc.shape[1]
    n_sel = s // NSA_SEL_LEN
    tt = min(TT_CMP, s)
    kern = functools.partial(_cmp_topk_kernel, tt=tt, nk=nk, n_cmp=n_cmp, n_sel=n_sel, top_n=min(NSA_SEL_TOPN, n_sel))
    return pl.pallas_call(
        kern,
        grid=(b, s // tt),
        in_specs=[pl.BlockSpec((1, tt, WIDTH), lambda bi, ti: (bi, ti, 0)),
                  pl.BlockSpec((1, nk, LANES), lambda bi, ti: (bi, 0, 0)),
                  pl.BlockSpec((1, LANES, nk), lambda bi, ti: (bi, 0, 0))],
        out_specs=[pl.BlockSpec((1, tt, WIDTH), lambda bi, ti: (bi, ti, 0))] * 2,
        out_shape=[jax.ShapeDtypeStruct((b, s, WIDTH), F32), jax.ShapeDtypeStruct((b, s, WIDTH), BF16)],
        scratch_shapes=[pltpu.VMEM((NSA_KV_HEADS, nk + SUBLANES, tt), F32), pltpu.VMEM((NSA_KV_HEADS, n_sel, tt), F32),
                        pltpu.VMEM((NSA_KV_HEADS, n_sel, tt), F32)],
        compiler_params=pltpu.CompilerParams(dimension_semantics=("arbitrary", "arbitrary"),
                                             vmem_limit_bytes=VMEM_LIMIT),
        name="cmp_topk",
    )(q, kc, vct)


CAUSAL, WINDOW = 0, 1


def _selwin_kernel(q_ref, bias_ref, ocmp_ref, gate_ref, zn_ref, ks_ref, vs_ref, kw_ref, vw_ref, e_ref, cm_ref, out_ref,
                   acc_scr, m_scr, *, tq, tk):
    t0 = pl.program_id(1) * tq
    jd = t0 // tk
    heads_per_kv = N_HEADS // NSA_KV_HEADS
    half = _iota((tq, LANES), 1) // HEAD_DIM
    q_win, q_sel = [], []
    for h in range(N_HEADS):
        qc = q_ref[0, :, (h % 2) * LANES:(h % 2 + 1) * LANES]
        qh = jnp.where(half == h // heads_per_kv, qc, jnp.zeros_like(qc))
        q_win.append(qh)
        q_sel.append(jnp.concatenate([qh, bias_ref[0, :, (h // heads_per_kv) * LANES:(h // heads_per_kv + 1) * LANES]],
                                     axis=1))
    ones = jnp.ones((tk, LANES), BF16)
    lanes = lambda x, n: jnp.concatenate([x] * (n // LANES), axis=1)

    def tile(i, k_ref, v_ref, selected, masking, first):
        k0 = pl.multiple_of((jd - i) * tk, tk)
        kt = k_ref[0, pl.ds(k0, tk), :]
        if selected:
            kt = jnp.concatenate([kt, e_ref[pl.ds(k0, tk), :]], axis=1)
        scores = [_dot_nt((q_sel if selected else q_win)[h], kt) for h in range(N_HEADS)]
        vt = jnp.concatenate([v_ref[0, pl.ds(k0, tk), :], ones], axis=1)
        ps, alphas = [], []
        for h, s in enumerate(scores):
            blk = slice(h * tq, (h + 1) * tq)
            if masking is not None:
                s = s + cm_ref[masking]
            s_max = jnp.max(s, axis=-1, keepdims=True)
            if first:
                m_new = jnp.broadcast_to(s_max, (tq, LANES))
            else:
                m_old = m_scr[blk, :]
                m_new = jnp.maximum(m_old, s_max)
                alphas.append(lanes(jnp.exp(m_old - m_new), WIDTH))
            ps.append(jnp.exp(s - lanes(m_new, tk)).astype(BF16))
            m_scr[blk, :] = m_new
        pv = _dot(jnp.concatenate(ps, axis=0), vt)
        acc_scr[...] = pv if first else jnp.concatenate(alphas, axis=0) * acc_scr[...] + pv

    def result():
        res = [acc_scr[h * tq:(h + 1) * tq, :LANES] * (1.0 / acc_scr[h * tq:(h + 1) * tq, LANES:])
               for h in range(N_HEADS)]
        return jnp.concatenate([jnp.where(half == 0, res[c], res[c + heads_per_kv]) for c in range(2)], axis=1)

    tile(0, ks_ref, vs_ref, True, CAUSAL, True)

    def sel_body(i, carry):
        tile(i, ks_ref, vs_ref, True, None, False)
        return carry

    lax.fori_loop(1, jd + 1, sel_body, 0)
    o_slc = result()

    tile(0, kw_ref, vw_ref, False, CAUSAL, True)

    @pl.when(jd >= 1)
    def _():
        tile(1, kw_ref, vw_ref, False, WINDOW, False)

    o_win = result()
    gate = gate_ref[0]
    o = gate[:, :WIDTH] * ocmp_ref[0] + gate[:, WIDTH:2 * WIDTH] * o_slc + gate[:, 2 * WIDTH:] * o_win
    out_ref[0] = (o * zn_ref[0]).astype(BF16)


def _selwin(q, bias, ocmp, gate, zn, ks, vs, kw, vw):
    b, s, _ = q.shape
    tq, tk = min(TQ_SEL, s), min(TK_SEL, s)
    assert NSA_WINDOW == tk and tk == tq and s // NSA_SEL_LEN <= HEAD_DIM
    member = (jnp.arange(s)[:, None] // NSA_SEL_LEN == jnp.arange(LANES)[None, :]).astype(BF16)
    row, col = jnp.arange(tq)[:, None], jnp.arange(tk)[None, :]
    cmask = jnp.where(jnp.stack([col <= row, col > row]), 0.0, NEG).astype(F32)
    tok = lambda w: pl.BlockSpec((1, tq, w), lambda bi, qi: (bi, qi, 0))
    seq = pl.BlockSpec((1, s, LANES), lambda bi, qi: (bi, 0, 0))
    return pl.pallas_call(
        functools.partial(_selwin_kernel, tq=tq, tk=tk),
        grid=(b, s // tq),
        in_specs=[tok(WIDTH), tok(WIDTH), tok(WIDTH), tok(3 * WIDTH), tok(WIDTH), seq, seq, seq, seq,
                  pl.BlockSpec(member.shape, lambda bi, qi: (0, 0)),
                  pl.BlockSpec(cmask.shape, lambda bi, qi: (0, 0, 0))],
        out_specs=tok(WIDTH),
        out_shape=jax.ShapeDtypeStruct((b, s, WIDTH), BF16),
        scratch_shapes=[pltpu.VMEM((N_HEADS * tq, WIDTH), F32), pltpu.VMEM((N_HEADS * tq, LANES), F32)],
        compiler_params=pltpu.CompilerParams(dimension_semantics=("arbitrary", "arbitrary"),
                                             vmem_limit_bytes=VMEM_LIMIT),
        name="selwin",
    )(q, bias, ocmp, gate, zn, ks, vs, kw, vw, member, cmask)


def _stickbrk_kernel(q_ref, k_ref, v_ref, z_ref, mk_ref, out_ref, acc_scr, carry_scr, *, tq, tk):
    t0 = pl.program_id(1) * tq
    jd = t0 // tk
    q = q_ref[0]
    head = _iota(q.shape, 1) // HEAD_DIM
    qh = [jnp.where(head == h, q, jnp.zeros_like(q)) for h in range(N_HEADS)]
    tri = jnp.where(_iota((tk, tk), 0) >= _iota((tk, tk), 1), 1.0, 0.0).astype(BF16)
    tri2 = jnp.concatenate([tri, tri], axis=0)
    lanes = lambda x, n: jnp.concatenate([x] * (n // LANES), axis=1)

    def tile(i, diagonal):
        k0 = pl.multiple_of((jd - i) * tk, tk)
        kt = k_ref[0, pl.ds(k0, tk), :]
        zs = [_dot_nt(qh[h], kt) * LOG2_E for h in range(N_HEADS)]
        sps, sufs = [], []
        for z2 in zs:
            sp2 = jnp.maximum(z2, 0.0) + jnp.log2(1.0 + jnp.exp2(-jnp.abs(z2)))
            if diagonal:
                sp2 = sp2 * mk_ref[0]
            hi = sp2.astype(BF16)
            lo = (sp2 - hi.astype(F32)).astype(BF16)
            sps.append(sp2)
            sufs.append(_dot(jnp.concatenate([hi, lo], axis=1), tri2))
        vt = v_ref[0, pl.ds(k0, tk), :]
        dead = None
        for h in range(N_HEADS):
            blk = slice(h * tq, (h + 1) * tq)
            tile_sum = jnp.broadcast_to(jnp.sum(sps[h], axis=-1, keepdims=True), (tq, LANES))
            if diagonal:
                w = jnp.exp2(zs[h] - sufs[h] + mk_ref[1])
                acc_scr[blk, :] = _dot(w.astype(BF16), vt)
                carry = tile_sum
            else:
                carry = carry_scr[blk, :]
                w = jnp.exp2(zs[h] - sufs[h] - lanes(carry, tk))
                acc_scr[blk, :] += _dot(w.astype(BF16), vt)
                carry = carry + tile_sum
            carry_scr[blk, :] = carry
            head_dead = jnp.min(carry) > SB_DEAD_LOG2
            dead = head_dead if dead is None else dead & head_dead
        return dead.astype(jnp.int32)

    def cond(c):
        i, dead = c
        return (i <= jd) & (dead == 0)

    def body(c):
        return c[0] + 1, tile(c[0], False)

    lax.while_loop(cond, body, (jnp.int32(1), tile(0, True)))
    out_ref[0] = (_unstack_heads(acc_scr[...], tq) * z_ref[0]).astype(BF16)


def _stickbrk(q, k, v, z):
    b, s, _ = q.shape
    tq, tk = min(TQ_SB, s), min(TK_SB, s)
    assert tq == tk
    keep = jnp.arange(tk)[None, :] < jnp.arange(tq)[:, None]
    masks = jnp.stack([jnp.where(keep, 1.0, 0.0), jnp.where(keep, 0.0, NEG)]).astype(F32)
    tok = pl.BlockSpec((1, tq, WIDTH), lambda bi, qi: (bi, qi, 0))
    seq = pl.BlockSpec((1, s, WIDTH), lambda bi, qi: (bi, 0, 0))
    return pl.pallas_call(
        functools.partial(_stickbrk_kernel, tq=tq, tk=tk),
        grid=(b, s // tq),
        in_specs=[tok, seq, seq, tok, pl.BlockSpec(masks.shape, lambda bi, qi: (0, 0, 0))],
        out_specs=tok,
        out_shape=jax.ShapeDtypeStruct((b, s, WIDTH), BF16),
        scratch_shapes=[pltpu.VMEM((N_HEADS * tq, WIDTH), F32), pltpu.VMEM((N_HEADS * tq, LANES), F32)],
        compiler_params=pltpu.CompilerParams(dimension_semantics=("arbitrary", "arbitrary"),
                                             vmem_limit_bytes=VMEM_LIMIT),
        name="stickbrk",
    )(q, k, v, z, masks)


def _s5_disc_kernel(are_ref, aim_ref, ldt_ref, bre_ref, bim_ref, abre_ref, abim_ref, bbre_ref, bbim_ref):
    dt = jnp.exp(ldt_ref[...])
    lr, li = are_ref[...], aim_ref[...]
    mag = jnp.exp(lr * dt)
    ab_re, ab_im = mag * jnp.cos(li * dt), mag * jnp.sin(li * dt)
    den = lr * lr + li * li
    coef_re = ((ab_re - 1.0) * lr + ab_im * li) / den
    coef_im = (ab_im * lr - (ab_re - 1.0) * li) / den
    abre_ref[...] = ab_re
    abim_ref[...] = ab_im
    br, bi = bre_ref[...], bim_ref[...]
    bbre_ref[...] = coef_re[:, None, :] * br - coef_im[:, None, :] * bi
    bbim_ref[...] = coef_re[:, None, :] * bi + coef_im[:, None, :] * br


def _s5_discretise(a_re, a_im, log_dt, b_re, b_im):
    g, p = a_re.shape
    brt, bit = jnp.swapaxes(b_re, 1, 2), jnp.swapaxes(b_im, 1, 2)
    return pl.pallas_call(
        _s5_disc_kernel,
        out_shape=[jax.ShapeDtypeStruct((g, p), F32)] * 2 + [jax.ShapeDtypeStruct(brt.shape, F32)] * 2,
        name="s5_disc",
    )(a_re, a_im, log_dt[:, None], brt, bit)


def _s5_kernel(u_ref, z_ref, bmat_ref, a_ref, cmat_ref, d_ref, gw_ref, gb_ref, out_ref, xs_scr, state_scr,
               *, nb, ts, ns):
    @pl.when(pl.program_id(0) == 0)
    def _():
        state_scr[...] = jnp.zeros((nb, 2 * ns), F32)

    nc = 2 * ns // LANES
    for b in range(nb):
        bu = _dot(u_ref[b].astype(BF16), bmat_ref[...])
        for c in range(nc):
            xs_scr[c, pl.ds(b, ts, stride=nb), :] = bu[:, c * LANES:(c + 1) * LANES]

    a_re = jnp.broadcast_to(a_ref[0:1, :], (nb, ns))
    a_im = jnp.broadcast_to(a_ref[1:2, :], (nb, ns))

    def step(t, state):
        x_re, x_im = state
        r0 = pl.multiple_of(t * nb, nb)
        bu = jnp.concatenate([xs_scr[c, pl.ds(r0, nb), :] for c in range(nc)], axis=1)
        n_re = a_re * x_re - a_im * x_im + bu[:, :ns]
        n_im = a_re * x_im + a_im * x_re + bu[:, ns:]
        for c in range(nc // 2):
            xs_scr[c, pl.ds(r0, nb), :] = n_re[:, c * LANES:(c + 1) * LANES]
            xs_scr[nc // 2 + c, pl.ds(r0, nb), :] = n_im[:, c * LANES:(c + 1) * LANES]
        return n_re, n_im

    st = state_scr[...]
    x_re, x_im = lax.fori_loop(0, ts, step, (st[:, :ns], st[:, ns:]), unroll=4)
    state_scr[...] = jnp.concatenate([x_re, x_im], axis=1)

    width = u_ref.shape[-1]
    ys = []
    for b in range(nb):
        xs = jnp.concatenate([xs_scr[c, pl.ds(b, ts, stride=nb), :] for c in range(nc)], axis=1)
        ys.append((_dot(xs.astype(BF16), cmat_ref[...]) + d_ref[...] * u_ref[b]).astype(BF16))
    for b in range(nb):
        glu = _dot(ys[b], gw_ref[...]) + gb_ref[...]
        out_ref[b] = (glu[:, :width] * _sigmoid(glu[:, width:]) * z_ref[b]).astype(BF16)


def _s5(u, z, bmat, a_rows, cmat, d_row, glu_w, glu_b):
    nb, s, w = u.shape
    ts = min(TS_S5, s)
    ns = a_rows.shape[1]
    full = lambda a: pl.BlockSpec(a.shape, lambda i: (0, 0))
    tok = pl.BlockSpec((nb, ts, w), lambda i: (0, i, 0))
    return pl.pallas_call(
        functools.partial(_s5_kernel, nb=nb, ts=ts, ns=ns),
        grid=(s // ts,),
        in_specs=[tok, tok, full(bmat), full(a_rows), full(cmat), full(d_row), full(glu_w), full(glu_b)],
        out_specs=tok,
        out_shape=jax.ShapeDtypeStruct((nb, s, w), BF16),
        scratch_shapes=[pltpu.VMEM((2 * ns // LANES, ts * nb, LANES), F32), pltpu.VMEM((nb, 2 * ns), F32)],
        compiler_params=pltpu.CompilerParams(dimension_semantics=("arbitrary",), vmem_limit_bytes=VMEM_LIMIT),
        name="s5",
    )(u, z, bmat, a_rows, cmat, d_row, glu_w, glu_b)


def _merge_kernel(x_ref, ng_ref, o0_ref, o1_ref, o2_ref, o3_ref, wm_ref, wb_ref, wo_ref, out_ref):
    x = x_ref[...]
    ms = jnp.mean(x * x, axis=-1, keepdims=True)
    h = (x * lax.rsqrt(ms + NORM_EPS) * ng_ref[...]).astype(BF16)
    d = x.shape[1]
    mixed = None
    for m, o_ref in enumerate((o0_ref, o1_ref, o2_ref, o3_ref)):
        gate = _sigmoid(_dot(h, wm_ref[:, m * d:(m + 1) * d]))
        term = gate * _dot(o_ref[...], wb_ref[m])
        mixed = term if mixed is None else mixed + term
    out_ref[...] = x + _dot(mixed.astype(BF16), wo_ref[...])


def _merge(x2, ng, outs, wm, wb, wo):
    t, d = x2.shape
    tm = min(TM_PROJ, t)
    tok = lambda w: pl.BlockSpec((tm, w), lambda i: (i, 0))
    full = lambda a: pl.BlockSpec(a.shape, lambda i: (0,) * a.ndim)
    return pl.pallas_call(
        _merge_kernel,
        grid=(t // tm,),
        in_specs=[tok(d), full(ng)] + [tok(WIDTH)] * 4 + [full(wm), full(wb), full(wo)],
        out_specs=tok(d),
        out_shape=jax.ShapeDtypeStruct((t, d), F32),
        compiler_params=pltpu.CompilerParams(dimension_semantics=("arbitrary",), vmem_limit_bytes=VMEM_LIMIT),
        name="merge",
    )(x2, ng, *outs, wm, wb, wo)


def _wprep_kernel(w_ref, wqk_ref, wv_ref, wc_ref, wg_ref, wz_ref, wsc_ref, wsb_ref, ws5_ref, wm_ref, *, d):
    w = WIDTH
    o_q, o_kv, o_gate = 0, w, 4 * w
    o_nz = o_gate + 3 * N_HEADS
    o_sc, o_scz, o_sb, o_sbz, o_s5, o_s5z = o_nz + w, o_nz + 4 * w, o_nz + 5 * w, o_nz + 8 * w, o_nz + 9 * w, o_nz + 10 * w
    o_merge = o_nz + 11 * w
    col = lambda a, n: w_ref[0, :, a:a + n]
    kv = lambda i: o_kv + i * (w // 2)
    cat = lambda parts: jnp.concatenate(parts, axis=1).astype(BF16)

    def heads(a):
        return [col(a + h * HEAD_DIM, HEAD_DIM) for h in NSA_HEAD_ORDER]

    wqk_ref[...] = cat(heads(o_q) + [col(kv(2), w // 2), col(kv(4), w // 2)])
    wv_ref[...] = cat([col(kv(3), w // 2), col(kv(5), w // 2)])
    wc_ref[...] = col(kv(0), w).astype(BF16)
    gates = col(o_gate, 3 * N_HEADS)
    wg_ref[...] = cat([jnp.broadcast_to(gates[:, c:c + 1], (gates.shape[0], HEAD_DIM))
                       for c in (br * N_HEADS + h for br in range(3) for h in NSA_HEAD_ORDER)])
    wz_ref[...] = cat(heads(o_nz) + [col(o_scz, w), col(o_sbz, w), col(o_s5z, w)])
    wsc_ref[...] = col(o_sc, 3 * w).astype(BF16)
    wsb_ref[...] = col(o_sb, 3 * w).astype(BF16)
    ws5_ref[...] = col(o_s5, w).astype(BF16)
    wm_ref[...] = col(o_merge, N_HEADS * d).astype(BF16)


def _wprep(w_in, layer):
    _, d, n_in = w_in.shape
    tr = min(TR_WPREP, d)
    widths = [2 * WIDTH, WIDTH, WIDTH, 3 * WIDTH, 4 * WIDTH, 3 * WIDTH, 3 * WIDTH, WIDTH, N_HEADS * d]
    return pl.pallas_call(
        functools.partial(_wprep_kernel, d=d),
        grid=(d // tr,),
        in_specs=[pl.BlockSpec((1, tr, n_in), lambda i: (layer, i, 0))],
        out_specs=[pl.BlockSpec((tr, n), lambda i: (i, 0)) for n in widths],
        out_shape=[jax.ShapeDtypeStruct((d, n), BF16) for n in widths],
        compiler_params=pltpu.CompilerParams(dimension_semantics=("arbitrary",), vmem_limit_bytes=VMEM_LIMIT),
        name="wprep",
    )(w_in)


def _block_diag(blocks):
    g, r, c = blocks.shape
    eye = jnp.eye(g, dtype=blocks.dtype)
    return (eye[:, None, :, None] * blocks[:, :, None, :]).reshape(g * r, g * c)


def _block_diag_pair(w1):
    eye = jnp.eye(NSA_KV_HEADS, dtype=w1.dtype)
    bd = w1[..., :, None, :, None, :] * eye[:, None, :, None]
    return bd.reshape(w1.shape[:-3] + (w1.shape[-3] * NSA_KV_HEADS * HEAD_DIM, NSA_KV_HEADS * HEAD_DIM))


def _layer(x, cos, sin, cos_c, sin_c, n_cmp, norm_g, w_in_all, layer, qk_g, cmp_pe, cmp_w1, cmp_w2, conv_w,
           a_re, a_im, log_dt, b_re, b_im, c_re, c_im, d_skip, glu_w, glu_b, w_branch, w_out):
    b, s, d = x.shape
    w = WIDTH
    *weights, w_merge = _wprep(w_in_all, layer)
    qkg = jnp.concatenate([jnp.tile(qk_g[0], N_HEADS), jnp.tile(qk_g[2], NSA_KV_HEADS),
                           jnp.tile(qk_g[3], NSA_KV_HEADS)])[None, :]
    ng = norm_g[None, :]

    (q, ks, kw, vs, vw, gate, zn, zsb, zs5, sc_o, sbq, sbk, sbv, s5u, kc_in, vc_in) = _inproj(
        x, ng, weights, qkg, cos, sin, conv_w)

    half = NSA_CMP_STRIDE
    two = NSA_CMP_LEN // half
    pe = jnp.tile(cmp_pe.reshape(2, two, half, HEAD_DIM), (1, 1, 1, NSA_KV_HEADS))
    w1 = _block_diag_pair(cmp_w1.reshape(2, two, half, HEAD_DIM, HEAD_DIM))
    w2k, w2v = _block_diag(jnp.stack([cmp_w2[0]] * NSA_KV_HEADS)), _block_diag(jnp.stack([cmp_w2[1]] * NSA_KV_HEADS))
    kc, vct = _compress(kc_in, vc_in, pe, w1.astype(BF16), w2k.astype(BF16), w2v.T.astype(BF16),
                        jnp.tile(qk_g[1], NSA_KV_HEADS)[None, :], cos_c, sin_c)

    ocmp, bias = _cmp_topk(q, kc, vct, n_cmp)
    nsa_o = _selwin(q, bias, ocmp, gate, zn, ks, vs, kw, vw)
    sb_o = _stickbrk(sbq, sbk, sbv, zsb)

    ab_re, ab_im, bb_re, bb_im = _s5_discretise(a_re, a_im, log_dt, b_re, b_im)
    bmat = jnp.concatenate([_block_diag(bb_re), _block_diag(bb_im)], axis=1).astype(BF16)
    cmat = jnp.concatenate([_block_diag(jnp.swapaxes(c_re, 1, 2)),
                            -_block_diag(jnp.swapaxes(c_im, 1, 2))], axis=0).astype(BF16)
    a_rows = jnp.stack([ab_re.reshape(-1), ab_im.reshape(-1)])
    s5_o = _s5(s5u, zs5, bmat, a_rows, cmat, d_skip.reshape(1, -1), glu_w.astype(BF16), glu_b[None, :])

    wb_nsa = w_branch[0].reshape(N_HEADS, HEAD_DIM, d)[jnp.array(NSA_HEAD_ORDER)].reshape(w, d)
    wb = jnp.concatenate([wb_nsa[None], w_branch[1:]], axis=0).astype(BF16)
    out = _merge(x.reshape(b * s, d), ng,
                 [o.reshape(b * s, w) for o in (nsa_o, sc_o, sb_o, s5_o)],
                 w_merge, wb, w_out.astype(BF16))
    return out.reshape(b, s, d)


def kernel(x, positions, norm_g, w_in, nsa_qk_g, nsa_cmp_pe, nsa_cmp_w1, nsa_cmp_w2, sc_conv_w, s5_a_re, s5_a_im,
           s5_log_dt, s5_b_re, s5_b_im, s5_c_re, s5_c_im, s5_d, s5_glu_w, s5_glu_b, w_branch, w_out):
    b, s, _ = x.shape
    assert s % TK_SEL == 0 or s < TK_SB
    cos, sin = _rope_tables(positions.reshape(-1))
    cos, sin = cos.reshape(b, s, LANES), sin.reshape(b, s, LANES)
    nk = s // NSA_CMP_STRIDE
    n_cmp = (s - NSA_CMP_LEN) // NSA_CMP_STRIDE + 1
    pos_c = jnp.concatenate([positions[:, NSA_CMP_LEN - 1::NSA_CMP_STRIDE],
                             jnp.zeros((b, nk - n_cmp), positions.dtype)], axis=1)
    cos_c, sin_c = _rope_tables(pos_c.reshape(-1))
    cos_c, sin_c = cos_c.reshape(b, nk, LANES), sin_c.reshape(b, nk, LANES)
    for l in range(norm_g.shape[0]):
        x = _layer(x, cos, sin, cos_c, sin_c, n_cmp, norm_g[l], w_in, l, nsa_qk_g[l], nsa_cmp_pe[l], nsa_cmp_w1[l],
                   nsa_cmp_w2[l], sc_conv_w[l], s5_a_re[l], s5_a_im[l], s5_log_dt[l], s5_b_re[l], s5_b_im[l],
                   s5_c_re[l], s5_c_im[l], s5_d[l], s5_glu_w[l], s5_glu_b[l], w_branch[l], w_out[l])
    return x
```
